```python
import math
import jax, jax.numpy as jnp
from jax import lax
import numpy as np

D_MODEL = 1024
BATCH = 8
SEQ = 4096
DEPTH = 2

N_MIXERS = 2
EXPAND = 2
D_INNER = EXPAND * D_MODEL
NORM_EPS = 1e-6
S5_GROUP = 16
S5_GROUPS = D_INNER // S5_GROUP
S5_STATE = 64
S5_CHUNK = 128
DT_MIN = 1e-3
DT_MAX = 1e-1
GDN_HEADS = 8
GDN_DK = D_MODEL // GDN_HEADS
GDN_DV = D_INNER // GDN_HEADS
GDN_CONV = 4
GDN_CHUNK = 64
GDN_QK = GDN_HEADS * GDN_DK
GDN_CONV_CH = 2 * GDN_QK + D_INNER
GDN_PROJ = GDN_CONV_CH + D_INNER + 2 * GDN_HEADS
N_S5 = (DEPTH + 1) // 2
N_GDN = DEPTH // 2

kernel_name = "hybrid_s5_gated_deltanet_adaln"

F32 = jnp.float32


def rms_norm(x, w):
    xf = x.astype(F32)
    y = xf * lax.rsqrt(jnp.mean(xf * xf, axis=-1, keepdims=True) + NORM_EPS) * w.astype(F32)
    return y.astype(x.dtype)


def l2norm(x):
    return x * lax.rsqrt(jnp.sum(x * x, axis=-1, keepdims=True) + NORM_EPS)


def _diag_combine(e1, e2):
    a1r, a1i, b1r, b1i = e1
    a2r, a2i, b2r, b2i = e2
    return (a2r * a1r - a2i * a1i,
            a2r * a1i + a2i * a1r,
            a2r * b1r - a2i * b1i + b2r,
            a2r * b1i + a2i * b1r + b2i)


def s5_mixer(h, w_in, lam_re, lam_im, log_dt, b_re, b_im, c_re, c_im, d_skip, w_glu, w_out):
    bsz, seqlen, _ = h.shape
    u, z = jnp.split(h @ w_in, 2, axis=-1)
    u = u.astype(F32)
    lam_re = lam_re.astype(F32); lam_im = lam_im.astype(F32)
    b_re = b_re.astype(F32); b_im = b_im.astype(F32)
    c_re = c_re.astype(F32); c_im = c_im.astype(F32)
    dt = jnp.exp(log_dt.astype(F32))[:, None]
    mag = jnp.exp(lam_re * dt)
    ab_re = mag * jnp.cos(lam_im * dt)
    ab_im = mag * jnp.sin(lam_im * dt)
    den = lam_re * lam_re + lam_im * lam_im
    nr = ab_re - 1.0
    ni = ab_im
    q_re = (nr * lam_re + ni * lam_im) / den
    q_im = (ni * lam_re - nr * lam_im) / den
    bb_re = q_re[..., None] * b_re - q_im[..., None] * b_im
    bb_im = q_re[..., None] * b_im + q_im[..., None] * b_re

    n_chunks = seqlen // S5_CHUNK
    u_chunks = u.reshape(bsz, n_chunks, S5_CHUNK, S5_GROUPS, S5_GROUP).transpose(1, 0, 2, 3, 4)
    a_re = jnp.broadcast_to(ab_re, (bsz, S5_CHUNK, S5_GROUPS, S5_STATE))
    a_im = jnp.broadcast_to(ab_im, (bsz, S5_CHUNK, S5_GROUPS, S5_STATE))

    def chunk_step(carry, u_c):
        s_re, s_im = carry
        bu_re = jnp.einsum('bcgm,gpm->bcgp', u_c, bb_re)
        bu_im = jnp.einsum('bcgm,gpm->bcgp', u_c, bb_im)
        bu_re = bu_re.at[:, 0].add(ab_re * s_re - ab_im * s_im)
        bu_im = bu_im.at[:, 0].add(ab_re * s_im + ab_im * s_re)
        _, _, x_re, x_im = lax.associative_scan(_diag_combine, (a_re, a_im, bu_re, bu_im), axis=1)
        y_c = (jnp.einsum('bcgp,gmp->bcgm', x_re, c_re)
               - jnp.einsum('bcgp,gmp->bcgm', x_im, c_im))
        return (x_re[:, -1], x_im[:, -1]), y_c

    init = (jnp.zeros((bsz, S5_GROUPS, S5_STATE), F32), jnp.zeros((bsz, S5_GROUPS, S5_STATE), F32))
    _, y = lax.scan(chunk_step, init, u_chunks)
    y = y.transpose(1, 0, 2, 3, 4).reshape(bsz, seqlen, D_INNER) + d_skip.astype(F32) * u
    y = jax.nn.gelu(y)
    y = y * jax.nn.sigmoid(y @ w_glu.astype(F32))
    y = y.astype(h.dtype) * jax.nn.silu(z)
    return y @ w_out


def _chunk_gated_delta_rule(q, k, v, beta, g):
    bsz, seqlen, nh, dk = q.shape
    dv = v.shape[-1]
    C = GDN_CHUNK
    nc = seqlen // C

    def to_chunks(t):
        return t.reshape((bsz, nc, C, nh) + t.shape[3:]).swapaxes(2, 3)

    q, k, v, beta, g = (to_chunks(t) for t in (q, k, v, beta, g))
    gc = jnp.cumsum(g, axis=-1)
    causal = jnp.tril(jnp.ones((C, C), bool))
    strict = jnp.tril(jnp.ones((C, C), bool), -1)
    decay = jnp.exp(jnp.where(causal, gc[..., :, None] - gc[..., None, :], -jnp.inf))
    kk = jnp.einsum('bnhid,bnhjd->bnhij', k, k)
    a_mat = jnp.where(strict, beta[..., None] * kk * decay, 0.0)
    lhs = a_mat + jnp.eye(C, dtype=F32)
    rhs_w = (beta * jnp.exp(gc))[..., None] * k
    rhs_u = beta[..., None] * v
    w = lax.linalg.triangular_solve(lhs, rhs_w, left_side=True, lower=True, unit_diagonal=True)
    u = lax.linalg.triangular_solve(lhs, rhs_u, left_side=True, lower=True, unit_diagonal=True)
    qk = jnp.einsum('bnhid,bnhjd->bnhij', q, k) * decay
    q_dec = q * jnp.exp(gc)[..., None]
    g_last = gc[..., -1]
    k_dec = k * jnp.exp(g_last[..., None] - gc)[..., None]

    def step(state, xs):
        q_c, w_c, u_c, qk_c, k_c, gl_c = xs
        v_new = u_c - jnp.einsum('bhcd,bhde->bhce', w_c, state)
        o_c = (jnp.einsum('bhcd,bhde->bhce', q_c, state)
               + jnp.einsum('bhij,bhje->bhie', qk_c, v_new))
        state = (jnp.exp(gl_c)[..., None, None] * state
                 + jnp.einsum('bhcd,bhce->bhde', k_c, v_new))
        return state, o_c

    xs = tuple(jnp.moveaxis(t, 1, 0) for t in (q_dec, w, u, qk, k_dec, g_last))
    init = jnp.zeros((bsz, nh, dk, dv), F32)
    _, o = lax.scan(step, init, xs)
    return o.transpose(1, 0, 3, 2, 4).reshape(bsz, seqlen, nh, dv)


def gdn_mixer(h, w_in, conv_w, a_log, dt_bias, norm_w, w_out):
    bsz, seqlen, _ = h.shape
    proj = h @ w_in
    qkv, z, b_logit, a_logit = jnp.split(
        proj, [GDN_CONV_CH, GDN_CONV_CH + D_INNER, GDN_CONV_CH + D_INNER + GDN_HEADS], axis=-1)
    qkv = lax.conv_general_dilated(
        qkv.astype(F32), conv_w.astype(F32)[:, None, :], (1,), [(GDN_CONV - 1, 0)],
        dimension_numbers=('NWC', 'WIO', 'NWC'), feature_group_count=GDN_CONV_CH)
    qkv = jax.nn.silu(qkv)
    q, k, v = jnp.split(qkv, [GDN_QK, 2 * GDN_QK], axis=-1)
    q = l2norm(q.reshape(bsz, seqlen, GDN_HEADS, GDN_DK)) * (GDN_DK ** -0.5)
    k = l2norm(k.reshape(bsz, seqlen, GDN_HEADS, GDN_DK))
    v = v.reshape(bsz, seqlen, GDN_HEADS, GDN_DV)
    beta = jax.nn.sigmoid(b_logit.astype(F32))
    g = -jnp.exp(a_log.astype(F32)) * jax.nn.softplus(a_logit.astype(F32) + dt_bias.astype(F32))
    o = _chunk_gated_delta_rule(q, k, v, beta, g)
    o = o * lax.rsqrt(jnp.mean(o * o, axis=-1, keepdims=True) + NORM_EPS) * norm_w.astype(F32)
    o = o.reshape(bsz, seqlen, D_INNER).astype(h.dtype) * jax.nn.silu(z)
    return o @ w_out


def setup_inputs(seed: int = 0) -> dict:
    key = jax.random.key(seed)
    ks = iter(jax.random.split(key, 32))

    def nrm(shape, scale):
        return scale * jax.random.normal(next(ks), shape, F32)

    s5_lambda_im = (math.pi * jnp.broadcast_to(jnp.arange(S5_STATE, dtype=F32), (N_S5, S5_GROUPS, S5_STATE))
                    + nrm((N_S5, S5_GROUPS, S5_STATE), 0.01))
    gdn_dt = jnp.exp(jax.random.uniform(next(ks), (N_GDN, GDN_HEADS), F32, math.log(DT_MIN), math.log(DT_MAX)))
    return {
        "x": nrm((BATCH, SEQ, D_MODEL), 1.0),
        "c": nrm((BATCH, D_MODEL), 1.0),
        "ada_w": nrm((DEPTH, D_MODEL, 3 * D_MODEL), D_MODEL ** -0.5),
        "ada_b": nrm((DEPTH, 3 * D_MODEL), 0.02),
        "norm_w": 1.0 + nrm((DEPTH, D_MODEL), 0.02),
        "s5_w_in": nrm((N_S5, D_MODEL, 2 * D_INNER), D_MODEL ** -0.5),
        "s5_lambda_re": -0.5 + nrm((N_S5, S5_GROUPS, S5_STATE), 0.01),
        "s5_lambda_im": s5_lambda_im,
        "s5_log_dt": jax.random.uniform(next(ks), (N_S5, S5_GROUPS), F32, math.log(DT_MIN), math.log(DT_MAX)),
        "s5_b_re": nrm((N_S5, S5_GROUPS, S5_STATE, S5_GROUP), (2 * S5_GROUP) ** -0.5),
        "s5_b_im": nrm((N_S5, S5_GROUPS, S5_STATE, S5_GROUP), (2 * S5_GROUP) ** -0.5),
        "s5_c_re": nrm((N_S5, S5_GROUPS, S5_GROUP, S5_STATE), S5_STATE ** -0.5),
        "s5_c_im": nrm((N_S5, S5_GROUPS, S5_GROUP, S5_STATE), S5_STATE ** -0.5),
        "s5_d": nrm((N_S5, D_INNER), 1.0),
        "s5_w_glu": nrm((N_S5, D_INNER, D_INNER), D_INNER ** -0.5),
        "s5_w_out": nrm((N_S5, D_INNER, D_MODEL), D_INNER ** -0.5),
        "gdn_w_in": nrm((N_GDN, D_MODEL, GDN_PROJ), D_MODEL ** -0.5),
        "gdn_conv_w": nrm((N_GDN, GDN_CONV, GDN_CONV_CH), GDN_CONV ** -0.5),
        "gdn_a_log": jnp.log(jax.random.uniform(next(ks), (N_GDN, GDN_HEADS), F32, 1.0, 16.0)),
        "gdn_dt_bias": gdn_dt + jnp.log(-jnp.expm1(-gdn_dt)),
        "gdn_norm_w": 1.0 + nrm((N_GDN, GDN_DV), 0.02),
        "gdn_w_out": nrm((N_GDN, D_INNER, D_MODEL), D_INNER ** -0.5),
        "final_norm_w": 1.0 + nrm((D_MODEL,), 0.02),
    }


def reference(x, c, ada_w, ada_b, norm_w, s5_w_in, s5_lambda_re, s5_lambda_im, s5_log_dt,
              s5_b_re, s5_b_im, s5_c_re, s5_c_im, s5_d, s5_w_glu, s5_w_out,
              gdn_w_in, gdn_conv_w, gdn_a_log, gdn_dt_bias, gdn_norm_w, gdn_w_out, final_norm_w):
    c_act = jax.nn.silu(c)
    for layer in range(DEPTH):
        mod = c_act @ ada_w[layer] + ada_b[layer]
        shift, scale, gate = jnp.split(mod, 3, axis=-1)
        h = rms_norm(x, norm_w[layer]) * (1.0 + scale[:, None, :]) + shift[:, None, :]
        j = layer // N_MIXERS
        if layer % N_MIXERS == 0:
            y = s5_mixer(h, s5_w_in[j], s5_lambda_re[j], s5_lambda_im[j], s5_log_dt[j],
                         s5_b_re[j], s5_b_im[j], s5_c_re[j], s5_c_im[j], s5_d[j],
                         s5_w_glu[j], s5_w_out[j])
        else:
            y = gdn_mixer(h, gdn_w_in[j], gdn_conv_w[j], gdn_a_log[j], gdn_dt_bias[j],
                          gdn_norm_w[j], gdn_w_out[j])
        x = x + (gate[:, None, :] * y).astype(x.dtype)
    return rms_norm(x, final_norm_w)
```

```python
import functools
import math

import jax
import jax.numpy as jnp
from jax import lax
from jax.experimental import pallas as pl
from jax.experimental.pallas import tpu as pltpu

F32 = jnp.float32
BF16 = jnp.bfloat16

NORM_EPS = 1e-6
S5_GROUP = 16
S5_STATE = 64
S5_TILE_GROUPS = 16
GDN_HEADS = 8
GDN_CONV = 4
GDN_CHUNK = 128
GDN_BASE_BLOCK = 8
SUBLANES = 8
LANES = 128
VMEM_LIMIT_BYTES = 56 * 1024 * 1024


def _sigmoid(x):
    return 1.0 / (1.0 + jnp.exp(-x))


def _silu(x):
    return x * _sigmoid(x)


def _gelu_tanh(x):
    c = math.sqrt(2.0 / math.pi)
    return 0.5 * x * (1.0 + jnp.tanh(c * (x + 0.044715 * (x * x * x))))


def _softplus(x):
    return jnp.maximum(x, 0.0) + jnp.log(1.0 + jnp.exp(-jnp.abs(x)))


def _rms(x):
    return x * lax.rsqrt(jnp.mean(x * x, axis=-1, keepdims=True) + NORM_EPS)


def _bdot(a, b):
    return jnp.dot(a.astype(BF16), b.astype(BF16), preferred_element_type=F32)


def _bdot_nt(a, b):
    return lax.dot_general(a.astype(BF16), b.astype(BF16), (((1,), (1,)), ((), ())),
                           preferred_element_type=F32)


def _bdot_tn(a, b):
    return lax.dot_general(a.astype(BF16), b.astype(BF16), (((0,), (0,)), ((), ())),
                           preferred_element_type=F32)


def _split3(x):
    hi = x.astype(BF16)
    r = x - hi.astype(F32)
    mid = r.astype(BF16)
    lo = (r - mid.astype(F32)).astype(BF16)
    return hi, mid, lo


def _params(*sem):
    return pltpu.CompilerParams(dimension_semantics=sem, vmem_limit_bytes=VMEM_LIMIT_BYTES)


def _resident(shape, index_map):
    return pl.BlockSpec(shape, index_map, pipeline_mode=pl.Buffered(1))


def _mod_kernel(c_ref, w_ref, b_ref, o_ref):
    o_ref[0] = _bdot(_silu(c_ref[...]), w_ref[0]) + b_ref[0]


def _modulation(c, ada_w, ada_b):
    depth, d, d3 = ada_w.shape
    bsz = c.shape[0]
    nb = d3 // d
    return pl.pallas_call(
        _mod_kernel,
        out_shape=jax.ShapeDtypeStruct((depth, bsz, d3), F32),
        grid=(depth, nb),
        in_specs=[
            pl.BlockSpec((bsz, d), lambda l, j: (0, 0)),
            pl.BlockSpec((1, d, d), lambda l, j: (l, 0, j)),
            pl.BlockSpec((1, 1, d), lambda l, j: (l, 0, j)),
        ],
        out_specs=pl.BlockSpec((1, bsz, d), lambda l, j: (l, 0, j)),
        compiler_params=_params("parallel", "parallel"),
        name="adaln_mod",
    )(c, ada_w.astype(BF16), ada_b.reshape(depth, 1, d3))


def _s5_disc_kernel(lre_ref, lim_ref, ldt_ref, are_ref, aim_ref, qre_ref, qim_ref):
    lre = lre_ref[...]
    lim = lim_ref[...]
    dt = jnp.exp(ldt_ref[...])
    mag = jnp.exp(lre * dt)
    are = mag * jnp.cos(lim * dt)
    aim = mag * jnp.sin(lim * dt)
    den = lre * lre + lim * lim
    nr = are - 1.0
    ni = aim
    are_ref[...] = are
    aim_ref[...] = aim
    qre_ref[...] = (nr * lre + ni * lim) / den
    qim_ref[...] = (ni * lre - nr * lim) / den


def _s5_bbar_kernel(qre_ref, qim_ref, bre_ref, bim_ref, ore_ref, oim_ref):
    qre = qre_ref[...]
    qim = qim_ref[...]
    bre = bre_ref[...]
    bim = bim_ref[...]
    ore_ref[...] = qre * bre - qim * bim
    oim_ref[...] = qre * bim + qim * bre


def _s5_discretise(lam_re, lam_im, log_dt, b_re, b_im):
    g, p = lam_re.shape
    m = b_re.shape[-1]
    a_re, a_im, q_re, q_im = pl.pallas_call(
        _s5_disc_kernel,
        out_shape=[jax.ShapeDtypeStruct((g, p), F32)] * 4,
        name="s5_disc",
    )(lam_re, lam_im, log_dt.reshape(g, 1))
    rows = g * p
    blk = 1024
    bb_re, bb_im = pl.pallas_call(
        _s5_bbar_kernel,
        out_shape=[jax.ShapeDtypeStruct((rows, m), F32)] * 2,
        grid=(rows // blk,),
        in_specs=[pl.BlockSpec((blk, 1), lambda i: (i, 0))] * 2 + [pl.BlockSpec((blk, m), lambda i: (i, 0))] * 2,
        out_specs=[pl.BlockSpec((blk, m), lambda i: (i, 0))] * 2,
        compiler_params=_params("parallel"),
        name="s5_bbar",
    )(q_re.reshape(rows, 1), q_im.reshape(rows, 1), b_re.reshape(rows, m), b_im.reshape(rows, m))
    return a_re, a_im, bb_re.reshape(g, p, m), bb_im.reshape(g, p, m)


def _s5_block_weights(a_re, a_im, bb_re, bb_im, c_re, c_im):
    g, p, m = bb_re.shape
    tg = S5_TILE_GROUPS
    nt = g // tg
    eye = jnp.eye(tg, dtype=F32)

    def in_blocks(bb):
        t = bb.reshape(nt, tg, p, m).transpose(0, 1, 3, 2)
        return (t[:, :, :, None, :] * eye[None, :, None, :, None]).reshape(nt, tg * m, tg * p)

    def out_blocks(cc):
        t = cc.reshape(nt, tg, m, p).transpose(0, 1, 3, 2)
        return (t[:, :, :, None, :] * eye[None, :, None, :, None]).reshape(nt, tg * p, tg * m)

    w_b = jnp.concatenate([in_blocks(bb_re), in_blocks(bb_im)], axis=2).astype(BF16)
    w_c = jnp.concatenate([out_blocks(c_re), -out_blocks(c_im)], axis=1).astype(BF16)
    a_tile = jnp.concatenate([a_re.reshape(nt, 1, tg * p), a_im.reshape(nt, 1, tg * p)], axis=2)
    a_tile = jnp.broadcast_to(a_tile, (nt, SUBLANES, 2 * tg * p))
    return w_b, w_c, a_tile


def _modulated_norm_rows(x, nw, scale, shift):
    r, d = x.shape
    y = (_rms(x) * nw).reshape(r // SUBLANES, SUBLANES, d)
    return (y * (1.0 + scale)[None] + shift[None]).reshape(r, d)


def _s5_in_kernel(x_ref, nw_ref, sc_ref, sh_ref, w_ref, u_ref, z_ref):
    h = _modulated_norm_rows(x_ref[...], nw_ref[...], sc_ref[...], sh_ref[...]).astype(BF16)
    e = u_ref.shape[1]
    u_ref[...] = jnp.dot(h, w_ref[:, :e], preferred_element_type=F32).astype(BF16)
    z_ref[...] = jnp.dot(h, w_ref[:, e:], preferred_element_type=F32).astype(BF16)


def _s5_in(x_tm, norm_w, scale, shift, w_in, rows):
    t, d = x_tm.shape
    e2 = w_in.shape[1]
    e = e2 // 2
    return pl.pallas_call(
        _s5_in_kernel,
        out_shape=[jax.ShapeDtypeStruct((t, e), BF16)] * 2,
        grid=(t // rows,),
        in_specs=[
            pl.BlockSpec((rows, d), lambda i: (i, 0)),
            _resident((1, d), lambda i: (0, 0)),
            _resident((SUBLANES, d), lambda i: (0, 0)),
            _resident((SUBLANES, d), lambda i: (0, 0)),
            _resident((d, e2), lambda i: (0, 0)),
        ],
        out_specs=[pl.BlockSpec((rows, e), lambda i: (i, 0))] * 2,
        compiler_params=_params("parallel"),
        name="s5_in",
    )(x_tm, norm_w.reshape(1, d), scale, shift, w_in.astype(BF16))


def _s5_scan_kernel(u_ref, wb_ref, wc_ref, a_ref, d_ref, y_ref, bu_ref, st_ref, *, steps):
    @pl.when(pl.program_id(1) == 0)
    def _():
        st_ref[...] = jnp.zeros_like(st_ref)

    u = u_ref[...]
    bu_ref[...] = jnp.dot(u, wb_ref[0], preferred_element_type=F32)
    half = bu_ref.shape[1] // 2
    nl = half // LANES
    a_re = [a_ref[0, :, j * LANES:(j + 1) * LANES] for j in range(nl)]
    a_im = [a_ref[0, :, half + j * LANES:half + (j + 1) * LANES] for j in range(nl)]
    x_re0 = tuple(st_ref[0, :, j * LANES:(j + 1) * LANES] for j in range(nl))
    x_im0 = tuple(st_ref[1, :, j * LANES:(j + 1) * LANES] for j in range(nl))

    def body(t, carry):
        x_re, x_im = carry
        r0 = pl.multiple_of(t * SUBLANES, SUBLANES)
        n_re, n_im = [], []
        for j in range(nl):
            b_re = bu_ref[pl.ds(r0, SUBLANES), j * LANES:(j + 1) * LANES]
            b_im = bu_ref[pl.ds(r0, SUBLANES), half + j * LANES:half + (j + 1) * LANES]
            v_re = a_re[j] * x_re[j] - a_im[j] * x_im[j] + b_re
            v_im = a_re[j] * x_im[j] + a_im[j] * x_re[j] + b_im
            bu_ref[pl.ds(r0, SUBLANES), j * LANES:(j + 1) * LANES] = v_re
            bu_ref[pl.ds(r0, SUBLANES), half + j * LANES:half + (j + 1) * LANES] = v_im
            n_re.append(v_re)
            n_im.append(v_im)
        return tuple(n_re), tuple(n_im)

    x_re, x_im = lax.fori_loop(0, steps, body, (x_re0, x_im0), unroll=2)
    for j in range(nl):
        st_ref[0, :, j * LANES:(j + 1) * LANES] = x_re[j]
        st_ref[1, :, j * LANES:(j + 1) * LANES] = x_im[j]

    y = jnp.dot(bu_ref[...].astype(BF16), wc_ref[0], preferred_element_type=F32)
    y_ref[...] = _gelu_tanh(y + d_ref[0] * u.astype(F32)).astype(BF16)


def _s5_scan(u_tm, w_b, w_c, a_tile, d_skip, steps):
    t, e = u_tm.shape
    nt, cw, sw = w_b.shape
    rows = steps * SUBLANES
    return pl.pallas_call(
        functools.partial(_s5_scan_kernel, steps=steps),
        out_shape=jax.ShapeDtypeStruct((t, e), BF16),
        grid=(nt, t // rows),
        in_specs=[
            pl.BlockSpec((rows, cw), lambda k, c: (c, k)),
            _resident((1, cw, sw), lambda k, c: (k, 0, 0)),
            _resident((1, sw, cw), lambda k, c: (k, 0, 0)),
            _resident((1, SUBLANES, sw), lambda k, c: (k, 0, 0)),
            _resident((1, 1, cw), lambda k, c: (k, 0, 0)),
        ],
        out_specs=pl.BlockSpec((rows, cw), lambda k, c: (c, k)),
        scratch_shapes=[
            pltpu.VMEM((rows, sw), F32),
            pltpu.VMEM((2, SUBLANES, sw // 2), F32),
        ],
        compiler_params=_params("parallel", "arbitrary"),
        name="s5_scan",
    )(u_tm, w_b, w_c, a_tile, d_skip.reshape(nt, 1, cw))


def _s5_out_kernel(y_ref, z_ref, x_ref, gate_ref, wg_ref, wo_ref, o_ref):
    y = y_ref[...]
    yf = y.astype(F32)
    y2 = yf * _sigmoid(jnp.dot(y, wg_ref[...], preferred_element_type=F32)) * _silu(z_ref[...].astype(F32))
    out = jnp.dot(y2.astype(BF16), wo_ref[...], preferred_element_type=F32)
    r, d = out.shape
    x = x_ref[...].reshape(r // SUBLANES, SUBLANES, d)
    o_ref[...] = (x + gate_ref[...][None] * out.reshape(r // SUBLANES, SUBLANES, d)).reshape(r, d)


def _s5_out(y, z, x_tm, gate, w_glu, w_out, rows):
    t, e = y.shape
    d = x_tm.shape[1]
    return pl.pallas_call(
        _s5_out_kernel,
        out_shape=jax.ShapeDtypeStruct((t, d), F32),
        grid=(t // rows,),
        in_specs=[
            pl.BlockSpec((rows, e), lambda i: (i, 0)),
            pl.BlockSpec((rows, e), lambda i: (i, 0)),
            pl.BlockSpec((rows, d), lambda i: (i, 0)),
            _resident((SUBLANES, d), lambda i: (0, 0)),
            _resident((e, e), lambda i: (0, 0)),
            _resident((e, d), lambda i: (0, 0)),
        ],
        out_specs=pl.BlockSpec((rows, d), lambda i: (i, 0)),
        compiler_params=_params("parallel"),
        name="s5_out",
    )(y, z, x_tm, gate, w_glu.astype(BF16), w_out.astype(BF16))


def _gdn_in_kernel(x_ref, nw_ref, sc_ref, sh_ref, w_ref, wg_ref, cw_ref, gp_ref,
                   q_ref, k_ref, v_ref, z_ref, g_ref, ext_ref, *, heads):
    halo = SUBLANES
    tm = x_ref.shape[1]
    cc = ext_ref.shape[1]

    @pl.when(pl.program_id(1) == 0)
    def _():
        ext_ref[0:halo, :] = jnp.zeros((halo, cc), F32)

    x = x_ref[0]
    h = ((_rms(x) * nw_ref[...]) * (1.0 + sc_ref[0]) + sh_ref[0]).astype(BF16)

    ext_ref[halo:halo + tm, :] = jnp.dot(h, w_ref[:, :cc], preferred_element_type=F32)
    conv = cw_ref[0:1, :] * ext_ref[pl.ds(halo - (GDN_CONV - 1), tm), :]
    for j in range(1, GDN_CONV):
        conv = conv + cw_ref[j:j + 1, :] * ext_ref[pl.ds(halo - (GDN_CONV - 1) + j, tm), :]
    ext_ref[0:halo, :] = ext_ref[tm:tm + halo, :]
    act = _silu(conv)

    qk = q_ref.shape[2]
    dk = qk // heads
    for hh in range(heads):
        qh = act[:, hh * dk:(hh + 1) * dk]
        kh = act[:, qk + hh * dk:qk + (hh + 1) * dk]
        qn = qh * lax.rsqrt(jnp.sum(qh * qh, axis=-1, keepdims=True) + NORM_EPS) * (dk ** -0.5)
        kn = kh * lax.rsqrt(jnp.sum(kh * kh, axis=-1, keepdims=True) + NORM_EPS)
        q_ref[0, :, hh * dk:(hh + 1) * dk] = qn.astype(BF16)
        k_ref[0, :, hh * dk:(hh + 1) * dk] = kn.astype(BF16)
    v_ref[0] = act[:, 2 * qk:].astype(BF16)
    z_ref[0] = jnp.dot(h, w_ref[:, cc:], preferred_element_type=F32).astype(BF16)

    logit = jnp.dot(h, wg_ref[...], preferred_element_type=F32)
    beta = _sigmoid(logit)
    g = -jnp.exp(gp_ref[0:1, :]) * _softplus(logit + gp_ref[1:2, :])
    lane = lax.broadcasted_iota(jnp.int32, logit.shape, 1)
    g_ref[0] = jnp.where(lane < heads, beta, g)[:, :2 * heads]


def _gdn_in(x, norm_w, scale, shift, w_in, conv_w, a_log, dt_bias, tm):
    bsz, seqlen, d = x.shape
    heads = a_log.shape[0]
    cc = conv_w.shape[1]
    e = (w_in.shape[1] - cc - 2 * heads)
    qk = (cc - e) // 2
    w_main = w_in[:, :cc + e].astype(BF16)
    w_gate = jnp.zeros((d, LANES), F32).at[:, :2 * heads].set(w_in[:, cc + e:]).astype(BF16)
    gate_par = jnp.zeros((2, LANES), F32)
    gate_par = gate_par.at[0, heads:2 * heads].set(a_log).at[1, heads:2 * heads].set(dt_bias)
    return pl.pallas_call(
        functools.partial(_gdn_in_kernel, heads=heads),
        out_shape=[
            jax.ShapeDtypeStruct((bsz, seqlen, qk), BF16),
            jax.ShapeDtypeStruct((bsz, seqlen, qk), BF16),
            jax.ShapeDtypeStruct((bsz, seqlen, e), BF16),
            jax.ShapeDtypeStruct((bsz, seqlen, e), BF16),
            jax.ShapeDtypeStruct((bsz, seqlen, 2 * heads), F32),
        ],
        grid=(bsz, seqlen // tm),
        in_specs=[
            pl.BlockSpec((1, tm, d), lambda b, t: (b, t, 0)),
            _resident((1, d), lambda b, t: (0, 0)),
            pl.BlockSpec((1, 1, d), lambda b, t: (b, 0, 0)),
            pl.BlockSpec((1, 1, d), lambda b, t: (b, 0, 0)),
            _resident((d, cc + e), lambda b, t: (0, 0)),
            _resident((d, LANES), lambda b, t: (0, 0)),
            _resident((GDN_CONV, cc), lambda b, t: (0, 0)),
            _resident((2, LANES), lambda b, t: (0, 0)),
        ],
        out_specs=[
            pl.BlockSpec((1, tm, qk), lambda b, t: (b, t, 0)),
            pl.BlockSpec((1, tm, qk), lambda b, t: (b, t, 0)),
            pl.BlockSpec((1, tm, e), lambda b, t: (b, t, 0)),
            pl.BlockSpec((1, tm, e), lambda b, t: (b, t, 0)),
            pl.BlockSpec((1, tm, 2 * heads), lambda b, t: (b, t, 0)),
        ],
        scratch_shapes=[pltpu.VMEM((tm + SUBLANES, cc), F32)],
        compiler_params=_params("parallel", "arbitrary"),
        name="gdn_in",
    )(x, norm_w.reshape(1, d), scale.reshape(bsz, 1, d), shift.reshape(bsz, 1, d),
      w_main, w_gate, conv_w, gate_par)


def _unit_lower_inverse(a, row, col):
    c = a.shape[0]
    eye = (row == col).astype(F32)

    def same_block(shift):
        return (row >> shift) == (col >> shift)

    shift = GDN_BASE_BLOCK.bit_length() - 1
    dblk = jnp.where(same_block(shift), a, 0.0)
    d2 = _bdot(dblk, dblk)
    d4 = _bdot(d2, d2)
    t = eye - dblk
    t = t + _bdot(t, d2)
    t = t + _bdot(t, d4)
    while (1 << shift) < c:
        off = jnp.where(same_block(shift + 1) & jnp.logical_not(same_block(shift)), a, 0.0)
        t = t - _bdot(t, _bdot(off, t))
        shift += 1
    return t


def _gdn_core_kernel(q_ref, k_ref, v_ref, z_ref, g_ref, nw_ref, y_ref, s_ref, *, heads):
    @pl.when(pl.program_id(1) == 0)
    def _():
        s_ref[...] = jnp.zeros_like(s_ref)

    c = q_ref.shape[1]
    dk = q_ref.shape[2] // heads
    dv = v_ref.shape[2] // heads
    row = lax.broadcasted_iota(jnp.int32, (c, c), 0)
    col = lax.broadcasted_iota(jnp.int32, (c, c), 1)
    causal = row >= col
    strict = row > col

    gb = g_ref[0]
    tri = causal.astype(BF16)
    parts = _split3(gb)
    gc = sum(jnp.dot(tri, p, preferred_element_type=F32) for p in parts)
    nh2 = gb.shape[1]
    eye_h = (lax.broadcasted_iota(jnp.int32, (nh2, nh2), 0)
             == lax.broadcasted_iota(jnp.int32, (nh2, nh2), 1)).astype(BF16)
    gc_t = sum(lax.dot_general(eye_h, p, (((1,), (1,)), ((), ())), preferred_element_type=F32)
               for p in _split3(gc))

    for h in range(heads):
        beta = gb[:, h:h + 1]
        gcol = gc[:, heads + h:heads + h + 1]
        grow = gc_t[heads + h:heads + h + 1, :]
        decay = jnp.where(causal, jnp.exp(gcol - grow), 0.0)
        qh = q_ref[0, :, h * dk:(h + 1) * dk]
        kh = k_ref[0, :, h * dk:(h + 1) * dk]
        vh = v_ref[0, :, h * dv:(h + 1) * dv]
        kk = _bdot_nt(kh, kh)
        qk = _bdot_nt(qh, kh) * decay
        a = jnp.where(strict, beta * kk * decay, 0.0)
        t = _unit_lower_inverse(a, row, col)
        gam = jnp.exp(gcol)
        khf = kh.astype(F32)
        rhs = jnp.concatenate([(beta * gam) * khf, beta * vh.astype(F32)], axis=1)
        wu = _bdot(t, rhs)
        w = wu[:, :dk]
        u = wu[:, dk:]
        s = s_ref[h]
        ws = _bdot(jnp.concatenate([w, qh.astype(F32) * gam], axis=0), s)
        v_new = u - ws[:c]
        o = ws[c:] + _bdot(qk, v_new)
        g_last = gcol[c - 1:c, :]
        k_dec = khf * jnp.exp(g_last - gcol)
        s_ref[h] = jnp.exp(g_last) * s + _bdot_tn(k_dec, v_new)
        on = _rms(o) * nw_ref[...]
        zh = z_ref[0, :, h * dv:(h + 1) * dv].astype(F32)
        y_ref[0, :, h * dv:(h + 1) * dv] = (on * _silu(zh)).astype(BF16)


def _gdn_core(q, k, v, z, gates, norm_w):
    bsz, seqlen, qk = q.shape
    e = v.shape[2]
    heads = gates.shape[2] // 2
    c = GDN_CHUNK
    return pl.pallas_call(
        functools.partial(_gdn_core_kernel, heads=heads),
        out_shape=jax.ShapeDtypeStruct((bsz, seqlen, e), BF16),
        grid=(bsz, seqlen // c),
        in_specs=[
            pl.BlockSpec((1, c, qk), lambda b, n: (b, n, 0)),
            pl.BlockSpec((1, c, qk), lambda b, n: (b, n, 0)),
            pl.BlockSpec((1, c, e), lambda b, n: (b, n, 0)),
            pl.BlockSpec((1, c, e), lambda b, n: (b, n, 0)),
            pl.BlockSpec((1, c, 2 * heads), lambda b, n: (b, n, 0)),
            _resident((1, e // heads), lambda b, n: (0, 0)),
        ],
        out_specs=pl.BlockSpec((1, c, e), lambda b, n: (b, n, 0)),
        scratch_shapes=[pltpu.VMEM((heads, qk // heads, e // heads), F32)],
        compiler_params=_params("parallel", "arbitrary"),
        name="gdn_core",
    )(q, k, v, z, gates, norm_w.reshape(1, e // heads))


def _gdn_out_kernel(y_ref, x_ref, gate_ref, w_ref, fw_ref, o_ref):
    out = jnp.dot(y_ref[0], w_ref[...], preferred_element_type=F32)
    xn = x_ref[0] + gate_ref[0] * out
    o_ref[0] = _rms(xn) * fw_ref[...]


def _gdn_out(y, x, gate, w_out, final_w, tm):
    bsz, seqlen, d = x.shape
    e = y.shape[2]
    return pl.pallas_call(
        _gdn_out_kernel,
        out_shape=jax.ShapeDtypeStruct((bsz, seqlen, d), F32),
        grid=(bsz, seqlen // tm),
        in_specs=[
            pl.BlockSpec((1, tm, e), lambda b, t: (b, t, 0)),
            pl.BlockSpec((1, tm, d), lambda b, t: (b, t, 0)),
            pl.BlockSpec((1, 1, d), lambda b, t: (b, 0, 0)),
            _resident((e, d), lambda b, t: (0, 0)),
            _resident((1, d), lambda b, t: (0, 0)),
        ],
        out_specs=pl.BlockSpec((1, tm, d), lambda b, t: (b, t, 0)),
        compiler_params=_params("parallel", "parallel"),
        name="gdn_out",
    )(y, x, gate.reshape(bsz, 1, d), w_out.astype(BF16), final_w.reshape(1, d))


def kernel(x, c, ada_w, ada_b, norm_w, s5_w_in, s5_lambda_re, s5_lambda_im, s5_log_dt, s5_b_re, s5_b_im,
           s5_c_re, s5_c_im, s5_d, s5_w_glu, s5_w_out, gdn_w_in, gdn_conv_w, gdn_a_log, gdn_dt_bias,
           gdn_norm_w, gdn_w_out, final_norm_w):
    bsz, seqlen, d = x.shape
    assert bsz == SUBLANES, "the S5 recurrence keeps one batch row per vector sublane"
    assert ada_w.shape[0] == 2 and s5_w_in.shape[0] == 1 and gdn_w_in.shape[0] == 1

    mod = _modulation(c, ada_w, ada_b)
    shift0, scale0, gate0 = mod[0, :, :d], mod[0, :, d:2 * d], mod[0, :, 2 * d:]
    shift1, scale1, gate1 = mod[1, :, :d], mod[1, :, d:2 * d], mod[1, :, 2 * d:]

    a_re, a_im, bb_re, bb_im = _s5_discretise(s5_lambda_re[0], s5_lambda_im[0], s5_log_dt[0], s5_b_re[0], s5_b_im[0])
    w_b, w_c, a_tile = _s5_block_weights(a_re, a_im, bb_re, bb_im, s5_c_re[0], s5_c_im[0])
    x_tm = x.transpose(1, 0, 2).reshape(seqlen * bsz, d)
    rows = min(512, seqlen * bsz)
    u, z = _s5_in(x_tm, norm_w[0], scale0, shift0, s5_w_in[0], rows)
    y = _s5_scan(u, w_b, w_c, a_tile, s5_d[0], steps=min(128, seqlen))
    x1_tm = _s5_out(y, z, x_tm, gate0, s5_w_glu[0], s5_w_out[0], rows)
    x1 = x1_tm.reshape(seqlen, bsz, d).transpose(1, 0, 2)

    tm = min(256, seqlen)
    q, k, v, zg, gates = _gdn_in(x1, norm_w[1], scale1, shift1, gdn_w_in[0], gdn_conv_w[0],
                                 gdn_a_log[0], gdn_dt_bias[0], tm)
    yg = _gdn_core(q, k, v, zg, gates, gdn_norm_w[0])
    return _gdn_out(yg, x1, gate1, gdn_w_out[0], final_norm_w, min(512, seqlen))
```

```python
import functools
import math

import jax
import jax.numpy as jnp
from jax import lax
from jax.experimental import pallas as pl
from jax.experimental.pallas import tpu as pltpu

F32 = jnp.float32
BF16 = jnp.bfloat16

NORM_EPS = 1e-6
S5_GROUP = 16
S5_STATE = 64
S5_TILE_GROUPS = 16
S5_STEPS = 64
S5_CHUNKS = 4
GDN_HEADS = 8
GDN_CONV = 4
GDN_CHUNK = 128
GDN_BASE_BLOCK = 8
GDN_BATCH_PER_STEP = 2
ROW_TILE = 512
GDN_IN_ROWS = 256
SUBLANES = 8
LANES = 128
VMEM_LIMIT_BYTES = 56 * 1024 * 1024


def _sigmoid(x):
    return 1.0 / (1.0 + jnp.exp(-x))


def _silu(x):
    return x * _sigmoid(x)


def _gelu_tanh(x):
    c = math.sqrt(2.0 / math.pi)
    return 0.5 * x * (1.0 + jnp.tanh(c * (x + 0.044715 * (x * x * x))))


def _softplus(x):
    return jnp.maximum(x, 0.0) + jnp.log(1.0 + jnp.exp(-jnp.abs(x)))


def _rms(x):
    return x * lax.rsqrt(jnp.mean(x * x, axis=-1, keepdims=True) + NORM_EPS)


def _bdot(a, b):
    return jnp.dot(a.astype(BF16), b.astype(BF16), preferred_element_type=F32)


def _bdot_nt(a, b):
    return lax.dot_general(a.astype(BF16), b.astype(BF16), (((1,), (1,)), ((), ())),
                           preferred_element_type=F32)


def _bdot_tn(a, b):
    return lax.dot_general(a.astype(BF16), b.astype(BF16), (((0,), (0,)), ((), ())),
                           preferred_element_type=F32)


def _split3(x):
    hi = x.astype(BF16)
    r = x - hi.astype(F32)
    mid = r.astype(BF16)
    lo = (r - mid.astype(F32)).astype(BF16)
    return hi, mid, lo


def _params(*sem):
    return pltpu.CompilerParams(dimension_semantics=sem, vmem_limit_bytes=VMEM_LIMIT_BYTES)


def _resident(shape, index_map):
    return pl.BlockSpec(shape, index_map, pipeline_mode=pl.Buffered(1))


def _mod_kernel(c_ref, w_ref, b_ref, o_ref):
    o_ref[0] = _bdot(_silu(c_ref[...]), w_ref[0]) + b_ref[0]


def _modulation(c, ada_w, ada_b):
    depth, d, d3 = ada_w.shape
    bsz = c.shape[0]
    nb = d3 // d
    return pl.pallas_call(
        _mod_kernel,
        out_shape=jax.ShapeDtypeStruct((depth, bsz, d3), F32),
        grid=(depth, nb),
        in_specs=[
            pl.BlockSpec((bsz, d), lambda l, j: (0, 0)),
            pl.BlockSpec((1, d, d), lambda l, j: (l, 0, j)),
            pl.BlockSpec((1, 1, d), lambda l, j: (l, 0, j)),
        ],
        out_specs=pl.BlockSpec((1, bsz, d), lambda l, j: (l, 0, j)),
        compiler_params=_params("parallel", "parallel"),
        name="adaln_mod",
    )(c, ada_w.astype(BF16), ada_b.reshape(depth, 1, d3))


def _s5_disc_kernel(lre_ref, lim_ref, ldt_ref, are_ref, aim_ref, qre_ref, qim_ref):
    lre = lre_ref[...]
    lim = lim_ref[...]
    dt = jnp.exp(ldt_ref[...])
    mag = jnp.exp(lre * dt)
    are = mag * jnp.cos(lim * dt)
    aim = mag * jnp.sin(lim * dt)
    den = lre * lre + lim * lim
    nr = are - 1.0
    ni = aim
    are_ref[...] = are
    aim_ref[...] = aim
    qre_ref[...] = (nr * lre + ni * lim) / den
    qim_ref[...] = (ni * lre - nr * lim) / den


def _s5_bbar_kernel(qre_ref, qim_ref, bre_ref, bim_ref, ore_ref, oim_ref):
    qre = qre_ref[...]
    qim = qim_ref[...]
    bre = bre_ref[...]
    bim = bim_ref[...]
    ore_ref[...] = qre * bre - qim * bim
    oim_ref[...] = qre * bim + qim * bre


def _s5_discretise(lam_re, lam_im, log_dt, b_re, b_im):
    g, p = lam_re.shape
    m = b_re.shape[-1]
    a_re, a_im, q_re, q_im = pl.pallas_call(
        _s5_disc_kernel,
        out_shape=[jax.ShapeDtypeStruct((g, p), F32)] * 4,
        name="s5_disc",
    )(lam_re, lam_im, log_dt.reshape(g, 1))
    rows = g * p
    blk = 1024
    bb_re, bb_im = pl.pallas_call(
        _s5_bbar_kernel,
        out_shape=[jax.ShapeDtypeStruct((rows, m), F32)] * 2,
        grid=(rows // blk,),
        in_specs=[pl.BlockSpec((blk, 1), lambda i: (i, 0))] * 2 + [pl.BlockSpec((blk, m), lambda i: (i, 0))] * 2,
        out_specs=[pl.BlockSpec((blk, m), lambda i: (i, 0))] * 2,
        compiler_params=_params("parallel"),
        name="s5_bbar",
    )(q_re.reshape(rows, 1), q_im.reshape(rows, 1), b_re.reshape(rows, m), b_im.reshape(rows, m))
    return a_re, a_im, bb_re.reshape(g, p, m), bb_im.reshape(g, p, m)


def _s5_block_weights(a_re, a_im, bb_re, bb_im, c_re, c_im):
    g, p, m = bb_re.shape
    tg = S5_TILE_GROUPS
    nt = g // tg
    eye = jnp.eye(tg, dtype=F32)

    def in_blocks(bb):
        t = bb.reshape(nt, tg, p, m).transpose(0, 1, 3, 2)
        return (t[:, :, :, None, :] * eye[None, :, None, :, None]).reshape(nt, tg * m, tg * p)

    def out_blocks(cc):
        t = cc.reshape(nt, tg, m, p).transpose(0, 1, 3, 2)
        return (t[:, :, :, None, :] * eye[None, :, None, :, None]).reshape(nt, tg * p, tg * m)

    w_b = jnp.concatenate([in_blocks(bb_re), in_blocks(bb_im)], axis=2).astype(BF16)
    w_c = jnp.concatenate([out_blocks(c_re), -out_blocks(c_im)], axis=1).astype(BF16)
    a_tile = jnp.concatenate([a_re.reshape(nt, 1, tg * p), a_im.reshape(nt, 1, tg * p)], axis=2)
    a_tile = jnp.broadcast_to(a_tile, (nt, SUBLANES, 2 * tg * p))
    return w_b, w_c, a_tile


def _s5_in_kernel(x_ref, nw_ref, sc_ref, sh_ref, w_ref, u_ref, z_ref):
    h = ((_rms(x_ref[0]) * nw_ref[...]) * (1.0 + sc_ref[0]) + sh_ref[0]).astype(BF16)
    e = u_ref.shape[1]
    u_ref[...] = jnp.dot(h, w_ref[:, :e], preferred_element_type=F32).astype(BF16)
    z_ref[...] = jnp.dot(h, w_ref[:, e:], preferred_element_type=F32).astype(BF16)


def _s5_in(x, norm_w, scale, shift, w_in):
    bsz, seqlen, d = x.shape
    e = w_in.shape[1] // 2
    tm = min(ROW_TILE, seqlen)
    return pl.pallas_call(
        _s5_in_kernel,
        out_shape=[jax.ShapeDtypeStruct((seqlen, bsz * e), BF16)] * 2,
        grid=(bsz, seqlen // tm),
        in_specs=[
            pl.BlockSpec((1, tm, d), lambda b, t: (b, t, 0)),
            _resident((1, d), lambda b, t: (0, 0)),
            pl.BlockSpec((1, 1, d), lambda b, t: (b, 0, 0)),
            pl.BlockSpec((1, 1, d), lambda b, t: (b, 0, 0)),
            _resident((d, 2 * e), lambda b, t: (0, 0)),
        ],
        out_specs=[pl.BlockSpec((tm, e), lambda b, t: (t, b))] * 2,
        compiler_params=_params("parallel", "parallel"),
        name="s5_in",
    )(x, norm_w.reshape(1, d), scale.reshape(bsz, 1, d), shift.reshape(bsz, 1, d), w_in.astype(BF16))


def _s5_scan_kernel(u_ref, wb_ref, wc_ref, a_ref, d_ref, y_ref, bu_ref, xs_ref, st_ref, *, steps, chunks):
    @pl.when(pl.program_id(1) == 0)
    def _():
        st_ref[...] = jnp.zeros_like(st_ref)

    rows = steps * SUBLANES
    half = bu_ref.shape[2] // 2
    nl = half // LANES
    a_re = [a_ref[0, :, j * LANES:(j + 1) * LANES] for j in range(nl)]
    a_im = [a_ref[0, :, half + j * LANES:half + (j + 1) * LANES] for j in range(nl)]
    x_re = [st_ref[0, :, j * LANES:(j + 1) * LANES] for j in range(nl)]
    x_im = [st_ref[1, :, j * LANES:(j + 1) * LANES] for j in range(nl)]

    def project_in(ci):
        u = u_ref[ci * rows:(ci + 1) * rows, :]
        slot = ci % 2
        bu_ref[slot, :, :half] = jnp.dot(u, wb_ref[0, :, :half], preferred_element_type=F32)
        bu_ref[slot, :, half:] = jnp.dot(u, wb_ref[0, :, half:], preferred_element_type=F32)

    def recur(ci):
        slot = ci % 2
        for tt in range(steps // 2):
            r0 = 2 * tt * SUBLANES
            for j in range(nl):
                lo, hi = j * LANES, (j + 1) * LANES
                pair_re, pair_im = [], []
                for s in range(2):
                    b_re = bu_ref[slot, r0 + s * SUBLANES:r0 + (s + 1) * SUBLANES, lo:hi]
                    b_im = bu_ref[slot, r0 + s * SUBLANES:r0 + (s + 1) * SUBLANES, half + lo:half + hi]
                    v_re = a_re[j] * x_re[j] - a_im[j] * x_im[j] + b_re
                    v_im = a_re[j] * x_im[j] + a_im[j] * x_re[j] + b_im
                    x_re[j], x_im[j] = v_re, v_im
                    pair_re.append(v_re)
                    pair_im.append(v_im)
                xs_ref[slot, r0:r0 + 2 * SUBLANES, lo:hi] = jnp.concatenate(pair_re, axis=0).astype(BF16)
                xs_ref[slot, r0:r0 + 2 * SUBLANES, half + lo:half + hi] = jnp.concatenate(pair_im, axis=0).astype(BF16)

    def project_out(ci):
        slot = ci % 2
        y = (jnp.dot(xs_ref[slot, :, :half], wc_ref[0, :half, :], preferred_element_type=F32)
             + jnp.dot(xs_ref[slot, :, half:], wc_ref[0, half:, :], preferred_element_type=F32))
        u = u_ref[ci * rows:(ci + 1) * rows, :].astype(F32)
        y_ref[ci * rows:(ci + 1) * rows, :] = _gelu_tanh(y + d_ref[0] * u).astype(BF16)

    project_in(0)
    for ci in range(chunks):
        if ci + 1 < chunks:
            project_in(ci + 1)
        recur(ci)
        project_out(ci)

    for j in range(nl):
        st_ref[0, :, j * LANES:(j + 1) * LANES] = x_re[j]
        st_ref[1, :, j * LANES:(j + 1) * LANES] = x_im[j]


def _s5_scan(u_tm, w_b, w_c, a_tile, d_skip, steps, chunks):
    t, e = u_tm.shape
    nt, cw, sw = w_b.shape
    rows = steps * SUBLANES
    return pl.pallas_call(
        functools.partial(_s5_scan_kernel, steps=steps, chunks=chunks),
        out_shape=jax.ShapeDtypeStruct((t, e), BF16),
        grid=(nt, t // (rows * chunks)),
        in_specs=[
            pl.BlockSpec((rows * chunks, cw), lambda k, c: (c, k)),
            _resident((1, cw, sw), lambda k, c: (k, 0, 0)),
            _resident((1, sw, cw), lambda k, c: (k, 0, 0)),
            _resident((1, SUBLANES, sw), lambda k, c: (k, 0, 0)),
            _resident((1, 1, cw), lambda k, c: (k, 0, 0)),
        ],
        out_specs=pl.BlockSpec((rows * chunks, cw), lambda k, c: (c, k)),
        scratch_shapes=[
            pltpu.VMEM((2, rows, sw), F32),
            pltpu.VMEM((2, rows, sw), BF16),
            pltpu.VMEM((2, SUBLANES, sw // 2), F32),
        ],
        compiler_params=_params("parallel", "arbitrary"),
        name="s5_scan",
    )(u_tm, w_b, w_c, a_tile, d_skip.reshape(nt, 1, cw))


def _s5_out_kernel(y_ref, z_ref, x_ref, gate_ref, wg_ref, wo_ref, o_ref):
    y = y_ref[...]
    yf = y.astype(F32)
    y2 = yf * _sigmoid(jnp.dot(y, wg_ref[...], preferred_element_type=F32)) * _silu(z_ref[...].astype(F32))
    out = jnp.dot(y2.astype(BF16), wo_ref[...], preferred_element_type=F32)
    o_ref[...] = x_ref[0] + gate_ref[0] * out


def _s5_out(y, z, x, gate, w_glu, w_out):
    bsz, seqlen, d = x.shape
    e = w_glu.shape[0]
    tm = min(ROW_TILE, seqlen)
    return pl.pallas_call(
        _s5_out_kernel,
        out_shape=jax.ShapeDtypeStruct((seqlen, bsz * d), F32),
        grid=(bsz, seqlen // tm),
        in_specs=[
            pl.BlockSpec((tm, e), lambda b, t: (t, b)),
            pl.BlockSpec((tm, e), lambda b, t: (t, b)),
            pl.BlockSpec((1, tm, d), lambda b, t: (b, t, 0)),
            pl.BlockSpec((1, 1, d), lambda b, t: (b, 0, 0)),
            _resident((e, e), lambda b, t: (0, 0)),
            _resident((e, d), lambda b, t: (0, 0)),
        ],
        out_specs=pl.BlockSpec((tm, d), lambda b, t: (t, b)),
        compiler_params=_params("parallel", "parallel"),
        name="s5_out",
    )(y, z, x, gate.reshape(bsz, 1, d), w_glu.astype(BF16), w_out.astype(BF16))


def _gdn_in_kernel(x_ref, nw_ref, sc_ref, sh_ref, w_ref, wg_ref, cw_ref, gp_ref,
                   q_ref, k_ref, v_ref, z_ref, g_ref, ext_ref, *, heads):
    halo = (GDN_CONV - 1) * SUBLANES
    r, d = x_ref.shape
    cc = ext_ref.shape[1]

    @pl.when(pl.program_id(0) == 0)
    def _():
        ext_ref[0:halo, :] = jnp.zeros((halo, cc), F32)

    y = (_rms(x_ref[...]) * nw_ref[...]).reshape(r // SUBLANES, SUBLANES, d)
    h = (y * (1.0 + sc_ref[...])[None] + sh_ref[...][None]).reshape(r, d).astype(BF16)

    ext_ref[halo:halo + r, :] = jnp.dot(h, w_ref[:, :cc], preferred_element_type=F32)
    conv = cw_ref[0:1, :] * ext_ref[0:r, :]
    for j in range(1, GDN_CONV):
        conv = conv + cw_ref[j:j + 1, :] * ext_ref[j * SUBLANES:j * SUBLANES + r, :]
    ext_ref[0:halo, :] = ext_ref[r:r + halo, :]
    act = _silu(conv)

    qk = q_ref.shape[1]
    dk = qk // heads
    for hh in range(heads):
        qh = act[:, hh * dk:(hh + 1) * dk]
        kh = act[:, qk + hh * dk:qk + (hh + 1) * dk]
        qn = qh * lax.rsqrt(jnp.sum(qh * qh, axis=-1, keepdims=True) + NORM_EPS) * (dk ** -0.5)
        kn = kh * lax.rsqrt(jnp.sum(kh * kh, axis=-1, keepdims=True) + NORM_EPS)
        q_ref[:, hh * dk:(hh + 1) * dk] = qn.astype(BF16)
        k_ref[:, hh * dk:(hh + 1) * dk] = kn.astype(BF16)
    v_ref[...] = act[:, 2 * qk:].astype(BF16)
    z_ref[...] = jnp.dot(h, w_ref[:, cc:], preferred_element_type=F32).astype(BF16)

    logit = jnp.dot(h, wg_ref[...], preferred_element_type=F32)
    beta = _sigmoid(logit)
    g = -jnp.exp(gp_ref[0:1, :]) * _softplus(logit + gp_ref[1:2, :])
    lane = lax.broadcasted_iota(jnp.int32, logit.shape, 1)
    g_ref[...] = jnp.where(lane < heads, beta, g)


def _gdn_in(x_tm, norm_w, scale, shift, w_in, conv_w, a_log, dt_bias):
    t, d = x_tm.shape
    heads = a_log.shape[0]
    cc = conv_w.shape[1]
    e = (w_in.shape[1] - cc - 2 * heads)
    qk = (cc - e) // 2
    rows = min(GDN_IN_ROWS, t)
    w_main = w_in[:, :cc + e].astype(BF16)
    w_gate = jnp.zeros((d, LANES), F32).at[:, :2 * heads].set(w_in[:, cc + e:]).astype(BF16)
    gate_par = jnp.zeros((2, LANES), F32)
    gate_par = gate_par.at[0, heads:2 * heads].set(a_log).at[1, heads:2 * heads].set(dt_bias)
    row_spec = lambda c: pl.BlockSpec((rows, c), lambda i: (i, 0))
    return pl.pallas_call(
        functools.partial(_gdn_in_kernel, heads=heads),
        out_shape=[
            jax.ShapeDtypeStruct((t, qk), BF16),
            jax.ShapeDtypeStruct((t, qk), BF16),
            jax.ShapeDtypeStruct((t, e), BF16),
            jax.ShapeDtypeStruct((t, e), BF16),
            jax.ShapeDtypeStruct((t, LANES), F32),
        ],
        grid=(t // rows,),
        in_specs=[
            row_spec(d),
            _resident((1, d), lambda i: (0, 0)),
            _resident((SUBLANES, d), lambda i: (0, 0)),
            _resident((SUBLANES, d), lambda i: (0, 0)),
            _resident((d, cc + e), lambda i: (0, 0)),
            _resident((d, LANES), lambda i: (0, 0)),
            _resident((GDN_CONV, cc), lambda i: (0, 0)),
            _resident((2, LANES), lambda i: (0, 0)),
        ],
        out_specs=[row_spec(qk), row_spec(qk), row_spec(e), row_spec(e), row_spec(LANES)],
        scratch_shapes=[pltpu.VMEM((rows + (GDN_CONV - 1) * SUBLANES, cc), F32)],
        compiler_params=_params("arbitrary"),
        name="gdn_in",
    )(x_tm, norm_w.reshape(1, d), scale, shift, w_main, w_gate, conv_w, gate_par)


def _pair_block_diag(x):
    c = x.shape[0]
    lane = lax.broadcasted_iota(jnp.int32, x.shape, 1)
    zero = jnp.zeros_like(x)
    return jnp.concatenate([jnp.where(lane < c, x, zero), jnp.where(lane >= c, x, zero)], axis=0)


def _pair_mm(x, y):
    return jnp.dot(x.astype(BF16), _pair_block_diag(y.astype(BF16)), preferred_element_type=F32)


def _gdn_core_kernel(q_ref, k_ref, v_ref, z_ref, g_ref, nw_ref, y_ref, s_ref, *, heads, nb):
    @pl.when(pl.program_id(1) == 0)
    def _():
        s_ref[...] = jnp.zeros_like(s_ref)

    c = q_ref.shape[0]
    qk_w = q_ref.shape[1] // nb
    e_w = v_ref.shape[1] // nb
    dk = qk_w // heads
    dv = e_w // heads
    units = [(b, p) for b in range(nb) for p in range(heads // 2)]

    row = lax.broadcasted_iota(jnp.int32, (c, 2 * c), 0)
    col = lax.broadcasted_iota(jnp.int32, (c, 2 * c), 1) & (c - 1)
    causal = row >= col
    strict = row > col
    eye = (row == col).astype(F32)
    row1 = lax.broadcasted_iota(jnp.int32, (c, c), 0)
    col1 = lax.broadcasted_iota(jnp.int32, (c, c), 1)
    tri = (row1 >= col1).astype(BF16)
    nh2 = 2 * heads
    eye_h = (lax.broadcasted_iota(jnp.int32, (nh2, nh2), 0)
             == lax.broadcasted_iota(jnp.int32, (nh2, nh2), 1)).astype(BF16)

    def same_block(shift):
        return (row >> shift) == (col >> shift)

    gb, gc, gc_t = [], [], []
    for b in range(nb):
        g = g_ref[:, b * LANES:b * LANES + nh2]
        cs = sum(jnp.dot(tri, part, preferred_element_type=F32) for part in _split3(g))
        gb.append(g)
        gc.append(cs)
        gc_t.append(sum(lax.dot_general(eye_h, part, (((1,), (1,)), ((), ())), preferred_element_type=F32)
                        for part in _split3(cs)))

    def lanes2(ref, b, p, width, per_batch):
        return ref[:, b * per_batch + 2 * p * width:b * per_batch + 2 * (p + 1) * width]

    def pair_cols(b, p, fn, width):
        return jnp.concatenate([jnp.broadcast_to(fn(b, 2 * p + i), (c, width)) for i in range(2)], axis=1)

    beta_c = lambda b, h: gb[b][:, h:h + 1]
    gcol = lambda b, h: gc[b][:, heads + h:heads + h + 1]
    grow = lambda b, h: gc_t[b][heads + h:heads + h + 1, :]

    a_mat, qk_mat = {}, {}
    for (b, p) in units:
        qp = lanes2(q_ref, b, p, dk, qk_w)
        kp = lanes2(k_ref, b, p, dk, qk_w)
        res = lax.dot_general(jnp.concatenate([qp, kp], axis=0), _pair_block_diag(kp),
                              (((1,), (1,)), ((), ())), preferred_element_type=F32)
        decay = jnp.where(causal, jnp.exp(jnp.concatenate(
            [gcol(b, 2 * p + i) - grow(b, 2 * p + i) for i in range(2)], axis=1)), 0.0)
        qk_mat[b, p] = res[:c] * decay
        a_mat[b, p] = jnp.where(strict, pair_cols(b, p, beta_c, c) * res[c:] * decay, 0.0)

    shift0 = GDN_BASE_BLOCK.bit_length() - 1
    dblk = {u: jnp.where(same_block(shift0), a_mat[u], 0.0) for u in units}
    d2 = {u: _pair_mm(dblk[u], dblk[u]) for u in units}
    d4 = {u: _pair_mm(d2[u], d2[u]) for u in units}
    t_inv = {u: eye - dblk[u] for u in units}
    for pw in (d2, d4):
        upd = {u: _pair_mm(t_inv[u], pw[u]) for u in units}
        t_inv = {u: t_inv[u] + upd[u] for u in units}
    shift = shift0
    while (1 << shift) < c:
        level = same_block(shift + 1) & jnp.logical_not(same_block(shift))
        xt = {u: _pair_mm(jnp.where(level, a_mat[u], 0.0), t_inv[u]) for u in units}
        upd = {u: _pair_mm(t_inv[u], xt[u]) for u in units}
        t_inv = {u: t_inv[u] - upd[u] for u in units}
        shift += 1

    w_pk, u_pk, gam_pk = {}, {}, {}
    for (b, p) in units:
        gam_pk[b, p] = jnp.exp(pair_cols(b, p, gcol, dk))
        kp = lanes2(k_ref, b, p, dk, qk_w).astype(F32)
        vp = lanes2(v_ref, b, p, dv, e_w).astype(F32)
        rw = (pair_cols(b, p, beta_c, dk) * gam_pk[b, p] * kp).astype(BF16)
        ru = (pair_cols(b, p, beta_c, dv) * vp).astype(BF16)
        zk = jnp.zeros((c, dk), BF16)
        zv = jnp.zeros((c, dv), BF16)
        rhs = jnp.concatenate([
            jnp.concatenate([rw[:, :dk], zk, ru[:, :dv], zv], axis=1),
            jnp.concatenate([zk, rw[:, dk:], zv, ru[:, dv:]], axis=1)], axis=0)
        wu = jnp.dot(t_inv[b, p].astype(BF16), rhs, preferred_element_type=F32)
        w_pk[b, p] = wu[:, :2 * dk]
        u_pk[b, p] = wu[:, 2 * dk:]

    hunits = [(b, h) for b in range(nb) for h in range(heads)]
    ws, states = {}, {}
    for (b, h) in hunits:
        p, i = divmod(h, 2)
        qd = lanes2(q_ref, b, p, dk, qk_w).astype(F32) * gam_pk[b, p]
        lhs = jnp.concatenate([w_pk[b, p][:, i * dk:(i + 1) * dk], qd[:, i * dk:(i + 1) * dk]], axis=0)
        states[b, h] = s_ref[b * heads + h]
        ws[b, h] = _bdot(lhs, states[b, h])
    v_new = {(b, h): u_pk[b, h // 2][:, (h % 2) * dv:(h % 2 + 1) * dv] - ws[b, h][:c] for (b, h) in hunits}
    o_intra = {(b, h): _bdot(qk_mat[b, h // 2][:, (h % 2) * c:(h % 2 + 1) * c], v_new[b, h]) for (b, h) in hunits}
    for (b, h) in hunits:
        g_last = gcol(b, h)[c - 1:c, :]
        kh = k_ref[:, b * qk_w + h * dk:b * qk_w + (h + 1) * dk].astype(F32)
        k_dec = kh * jnp.exp(g_last - gcol(b, h))
        s_ref[b * heads + h] = jnp.exp(g_last) * states[b, h] + _bdot_tn(k_dec, v_new[b, h])
    for (b, h) in hunits:
        o = ws[b, h][c:] + o_intra[b, h]
        on = _rms(o) * nw_ref[...]
        zh = z_ref[:, b * e_w + h * dv:b * e_w + (h + 1) * dv].astype(F32)
        y_ref[:, b * e_w + h * dv:b * e_w + (h + 1) * dv] = (on * _silu(zh)).astype(BF16)


def _gdn_core(q, k, v, z, gates, norm_w, heads):
    seqlen = q.shape[0]
    nb = GDN_BATCH_PER_STEP
    bsz = gates.shape[1] // LANES
    qk = q.shape[1] // bsz
    e = v.shape[1] // bsz
    c = min(GDN_CHUNK, seqlen)
    col_spec = lambda w: pl.BlockSpec((c, nb * w), lambda b, n: (n, b))
    return pl.pallas_call(
        functools.partial(_gdn_core_kernel, heads=heads, nb=nb),
        out_shape=jax.ShapeDtypeStruct((seqlen, bsz * e), BF16),
        grid=(bsz // nb, seqlen // c),
        in_specs=[col_spec(qk), col_spec(qk), col_spec(e), col_spec(e), col_spec(LANES),
                  _resident((1, e // heads), lambda b, n: (0, 0))],
        out_specs=col_spec(e),
        scratch_shapes=[pltpu.VMEM((nb * heads, qk // heads, e // heads), F32)],
        compiler_params=_params("parallel", "arbitrary"),
        name="gdn_core",
    )(q, k, v, z, gates, norm_w.reshape(1, e // heads))


def _gdn_out_kernel(y_ref, x_ref, gate_ref, w_ref, fw_ref, o_ref):
    out = jnp.dot(y_ref[...], w_ref[...], preferred_element_type=F32)
    xn = x_ref[...] + gate_ref[0] * out
    o_ref[0] = _rms(xn) * fw_ref[...]


def _gdn_out(y, x_tm, gate, w_out, final_w):
    e, d = w_out.shape
    seqlen = y.shape[0]
    bsz = y.shape[1] // e
    tm = min(ROW_TILE, seqlen)
    return pl.pallas_call(
        _gdn_out_kernel,
        out_shape=jax.ShapeDtypeStruct((bsz, seqlen, d), F32),
        grid=(bsz, seqlen // tm),
        in_specs=[
            pl.BlockSpec((tm, e), lambda b, t: (t, b)),
            pl.BlockSpec((tm, d), lambda b, t: (t, b)),
            pl.BlockSpec((1, 1, d), lambda b, t: (b, 0, 0)),
            _resident((e, d), lambda b, t: (0, 0)),
            _resident((1, d), lambda b, t: (0, 0)),
        ],
        out_specs=pl.BlockSpec((1, tm, d), lambda b, t: (b, t, 0)),
        compiler_params=_params("parallel", "parallel"),
        name="gdn_out",
    )(y, x_tm, gate.reshape(bsz, 1, d), w_out.astype(BF16), final_w.reshape(1, d))


def kernel(x, c, ada_w, ada_b, norm_w, s5_w_in, s5_lambda_re, s5_lambda_im, s5_log_dt, s5_b_re, s5_b_im,
           s5_c_re, s5_c_im, s5_d, s5_w_glu, s5_w_out, gdn_w_in, gdn_conv_w, gdn_a_log, gdn_dt_bias,
           gdn_norm_w, gdn_w_out, final_norm_w):
    bsz, seqlen, d = x.shape
    assert bsz == SUBLANES, "the time-major layout keeps one batch row per vector sublane"
    assert ada_w.shape[0] == 2 and s5_w_in.shape[0] == 1 and gdn_w_in.shape[0] == 1
    t = seqlen * bsz

    mod = _modulation(c, ada_w, ada_b)
    shift0, scale0, gate0 = mod[0, :, :d], mod[0, :, d:2 * d], mod[0, :, 2 * d:]
    shift1, scale1, gate1 = mod[1, :, :d], mod[1, :, d:2 * d], mod[1, :, 2 * d:]

    a_re, a_im, bb_re, bb_im = _s5_discretise(s5_lambda_re[0], s5_lambda_im[0], s5_log_dt[0], s5_b_re[0], s5_b_im[0])
    w_b, w_c, a_tile = _s5_block_weights(a_re, a_im, bb_re, bb_im, s5_c_re[0], s5_c_im[0])
    u, z = _s5_in(x, norm_w[0], scale0, shift0, s5_w_in[0])
    e = u.shape[1] // bsz
    steps = min(S5_STEPS, seqlen)
    chunks = min(S5_CHUNKS, seqlen // steps)
    y = _s5_scan(u.reshape(t, e), w_b, w_c, a_tile, s5_d[0], steps, chunks).reshape(seqlen, bsz * e)
    x1 = _s5_out(y, z, x, gate0, s5_w_glu[0], s5_w_out[0])

    heads = gdn_a_log.shape[1]
    q, k, v, zg, gates = _gdn_in(x1.reshape(t, d), norm_w[1], scale1, shift1, gdn_w_in[0], gdn_conv_w[0],
                                 gdn_a_log[0], gdn_dt_bias[0])
    tm_view = lambda a: a.reshape(seqlen, bsz * a.shape[1])
    yg = _gdn_core(tm_view(q), tm_view(k), tm_view(v), tm_view(zg), tm_view(gates), gdn_norm_w[0], heads)
    return _gdn_out(yg, x1, gate1, gdn_w_out[0], final_norm_w)
```

```python
import functools
import math

import jax
import jax.numpy as jnp
from jax import lax
from jax.experimental import pallas as pl
from jax.experimental.pallas import tpu as pltpu

F32 = jnp.float32
BF16 = jnp.bfloat16

NORM_EPS = 1e-6
S5_GROUP = 16
S5_STATE = 64
S5_TILE_GROUPS = 16
S5_STEPS = 64
S5_CHUNKS = 4
GDN_HEADS = 8
GDN_CONV = 4
GDN_CHUNK = 128
GDN_BASE_BLOCK = 8
GDN_BATCH_PER_STEP = 2
ROW_TILE = 512
GDN_IN_ROWS = 256
SUBLANES = 8
LANES = 128
VMEM_LIMIT_BYTES = 56 * 1024 * 1024


def _sigmoid(x):
    return 1.0 / (1.0 + jnp.exp(-x))


def _silu(x):
    return x * _sigmoid(x)


def _gelu_tanh(x):
    c = math.sqrt(2.0 / math.pi)
    return 0.5 * x * (1.0 + jnp.tanh(c * (x + 0.044715 * (x * x * x))))


def _softplus(x):
    return jnp.maximum(x, 0.0) + jnp.log(1.0 + jnp.exp(-jnp.abs(x)))


def _rms(x):
    return x * lax.rsqrt(jnp.mean(x * x, axis=-1, keepdims=True) + NORM_EPS)


def _bdot(a, b):
    return jnp.dot(a.astype(BF16), b.astype(BF16), preferred_element_type=F32)


def _bdot_nt(a, b):
    return lax.dot_general(a.astype(BF16), b.astype(BF16), (((1,), (1,)), ((), ())),
                           preferred_element_type=F32)


def _bdot_tn(a, b):
    return lax.dot_general(a.astype(BF16), b.astype(BF16), (((0,), (0,)), ((), ())),
                           preferred_element_type=F32)


def _split3(x):
    hi = x.astype(BF16)
    r = x - hi.astype(F32)
    mid = r.astype(BF16)
    lo = (r - mid.astype(F32)).astype(BF16)
    return hi, mid, lo


def _params(*sem):
    return pltpu.CompilerParams(dimension_semantics=sem, vmem_limit_bytes=VMEM_LIMIT_BYTES)


def _resident(shape, index_map):
    return pl.BlockSpec(shape, index_map, pipeline_mode=pl.Buffered(1))


def _mod_kernel(c_ref, w_ref, b_ref, o_ref):
    o_ref[0] = _bdot(_silu(c_ref[...]), w_ref[0]) + b_ref[0]


def _modulation(c, ada_w, ada_b):
    depth, d, d3 = ada_w.shape
    bsz = c.shape[0]
    nb = d3 // d
    return pl.pallas_call(
        _mod_kernel,
        out_shape=jax.ShapeDtypeStruct((depth, bsz, d3), F32),
        grid=(depth, nb),
        in_specs=[
            pl.BlockSpec((bsz, d), lambda l, j: (0, 0)),
            pl.BlockSpec((1, d, d), lambda l, j: (l, 0, j)),
            pl.BlockSpec((1, 1, d), lambda l, j: (l, 0, j)),
        ],
        out_specs=pl.BlockSpec((1, bsz, d), lambda l, j: (l, 0, j)),
        compiler_params=_params("parallel", "parallel"),
        name="adaln_mod",
    )(c, ada_w.astype(BF16), ada_b.reshape(depth, 1, d3))


def _s5_disc_kernel(lre_ref, lim_ref, ldt_ref, are_ref, aim_ref, qre_ref, qim_ref):
    lre = lre_ref[...]
    lim = lim_ref[...]
    dt = jnp.exp(ldt_ref[...])
    mag = jnp.exp(lre * dt)
    are = mag * jnp.cos(lim * dt)
    aim = mag * jnp.sin(lim * dt)
    den = lre * lre + lim * lim
    nr = are - 1.0
    ni = aim
    are_ref[...] = are
    aim_ref[...] = aim
    qre_ref[...] = (nr * lre + ni * lim) / den
    qim_ref[...] = (ni * lre - nr * lim) / den


def _s5_bbar_kernel(qre_ref, qim_ref, bre_ref, bim_ref, ore_ref, oim_ref):
    qre = qre_ref[...]
    qim = qim_ref[...]
    bre = bre_ref[...]
    bim = bim_ref[...]
    ore_ref[...] = qre * bre - qim * bim
    oim_ref[...] = qre * bim + qim * bre


def _s5_discretise(lam_re, lam_im, log_dt, b_re, b_im):
    g, p = lam_re.shape
    m = b_re.shape[-1]
    a_re, a_im, q_re, q_im = pl.pallas_call(
        _s5_disc_kernel,
        out_shape=[jax.ShapeDtypeStruct((g, p), F32)] * 4,
        name="s5_disc",
    )(lam_re, lam_im, log_dt.reshape(g, 1))
    rows = g * p
    blk = 1024
    bb_re, bb_im = pl.pallas_call(
        _s5_bbar_kernel,
        out_shape=[jax.ShapeDtypeStruct((rows, m), F32)] * 2,
        grid=(rows // blk,),
        in_specs=[pl.BlockSpec((blk, 1), lambda i: (i, 0))] * 2 + [pl.BlockSpec((blk, m), lambda i: (i, 0))] * 2,
        out_specs=[pl.BlockSpec((blk, m), lambda i: (i, 0))] * 2,
        compiler_params=_params("parallel"),
        name="s5_bbar",
    )(q_re.reshape(rows, 1), q_im.reshape(rows, 1), b_re.reshape(rows, m), b_im.reshape(rows, m))
    return a_re, a_im, bb_re.reshape(g, p, m), bb_im.reshape(g, p, m)


def _s5_block_weights(a_re, a_im, bb_re, bb_im, c_re, c_im):
    g, p, m = bb_re.shape
    tg = S5_TILE_GROUPS
    nt = g // tg
    eye = jnp.eye(tg, dtype=F32)

    def in_blocks(bb):
        t = bb.reshape(nt, tg, p, m).transpose(0, 1, 3, 2)
        return (t[:, :, :, None, :] * eye[None, :, None, :, None]).reshape(nt, tg * m, tg * p)

    def out_blocks(cc):
        t = cc.reshape(nt, tg, m, p).transpose(0, 1, 3, 2)
        return (t[:, :, :, None, :] * eye[None, :, None, :, None]).reshape(nt, tg * p, tg * m)

    w_b = jnp.concatenate([in_blocks(bb_re), in_blocks(bb_im)], axis=2).astype(BF16)
    w_c = jnp.concatenate([out_blocks(c_re), -out_blocks(c_im)], axis=1).astype(BF16)
    a_tile = jnp.concatenate([a_re.reshape(nt, 1, tg * p), a_im.reshape(nt, 1, tg * p)], axis=2)
    a_tile = jnp.broadcast_to(a_tile, (nt, SUBLANES, 2 * tg * p))
    return w_b, w_c, a_tile


def _modulated_norm_rows(x, nw, scale, shift):
    r, d = x.shape
    y = (_rms(x) * nw).reshape(r // SUBLANES, SUBLANES, d)
    return (y * (1.0 + scale)[None] + shift[None]).reshape(r, d)


def _sequence_rows(slab_ref, b, tmt, lo, hi):
    return jnp.concatenate([slab_ref[j, pl.ds(b, tmt, stride=SUBLANES), :]
                            for j in range(lo // LANES, hi // LANES)], axis=1)


def _s5_in_kernel(x_ref, nw_ref, sc_ref, sh_ref, w_ref, xt_ref, u_ref, z_ref, slab_ref):
    bsz, tmt, d = x_ref.shape
    for j in range(d // LANES):
        for b in range(bsz):
            slab_ref[j, pl.ds(b, tmt, stride=bsz), :] = x_ref[b, :, j * LANES:(j + 1) * LANES]
    x_tm = jnp.concatenate([slab_ref[j] for j in range(d // LANES)], axis=1)
    xt_ref[...] = x_tm
    h = _modulated_norm_rows(x_tm, nw_ref[...], sc_ref[...], sh_ref[...]).astype(BF16)
    e = u_ref.shape[1]
    u_ref[...] = jnp.dot(h, w_ref[:, :e], preferred_element_type=F32).astype(BF16)
    z_ref[...] = jnp.dot(h, w_ref[:, e:], preferred_element_type=F32).astype(BF16)


def _s5_in(x, norm_w, scale, shift, w_in):
    bsz, seqlen, d = x.shape
    e = w_in.shape[1] // 2
    tmt = min(ROW_TILE // bsz, seqlen)
    rows = tmt * bsz
    return pl.pallas_call(
        _s5_in_kernel,
        out_shape=[jax.ShapeDtypeStruct((seqlen * bsz, d), F32)] + [jax.ShapeDtypeStruct((seqlen * bsz, e), BF16)] * 2,
        grid=(seqlen // tmt,),
        in_specs=[
            pl.BlockSpec((bsz, tmt, d), lambda i: (0, i, 0)),
            _resident((1, d), lambda i: (0, 0)),
            _resident((bsz, d), lambda i: (0, 0)),
            _resident((bsz, d), lambda i: (0, 0)),
            _resident((d, 2 * e), lambda i: (0, 0)),
        ],
        out_specs=[pl.BlockSpec((rows, d), lambda i: (i, 0))] + [pl.BlockSpec((rows, e), lambda i: (i, 0))] * 2,
        scratch_shapes=[pltpu.VMEM((d // LANES, rows, LANES), F32)],
        compiler_params=_params("parallel"),
        name="s5_in",
    )(x, norm_w.reshape(1, d), scale, shift, w_in.astype(BF16))


def _s5_scan_kernel(u_ref, wb_ref, wc_ref, a_ref, d_ref, y_ref, bu_ref, xs_ref, st_ref, *, steps, chunks):
    @pl.when(pl.program_id(1) == 0)
    def _():
        st_ref[...] = jnp.zeros_like(st_ref)

    rows = steps * SUBLANES
    half = bu_ref.shape[2] // 2
    nl = half // LANES
    a_re = [a_ref[0, :, j * LANES:(j + 1) * LANES] for j in range(nl)]
    a_im = [a_ref[0, :, half + j * LANES:half + (j + 1) * LANES] for j in range(nl)]
    x_re = [st_ref[0, :, j * LANES:(j + 1) * LANES] for j in range(nl)]
    x_im = [st_ref[1, :, j * LANES:(j + 1) * LANES] for j in range(nl)]

    def project_in(ci):
        u = u_ref[ci * rows:(ci + 1) * rows, :]
        slot = ci % 2
        bu_ref[slot, :, :half] = jnp.dot(u, wb_ref[0, :, :half], preferred_element_type=F32)
        bu_ref[slot, :, half:] = jnp.dot(u, wb_ref[0, :, half:], preferred_element_type=F32)

    def recur(ci):
        slot = ci % 2
        for tt in range(steps // 2):
            r0 = 2 * tt * SUBLANES
            for j in range(nl):
                lo, hi = j * LANES, (j + 1) * LANES
                pair_re, pair_im = [], []
                for s in range(2):
                    b_re = bu_ref[slot, r0 + s * SUBLANES:r0 + (s + 1) * SUBLANES, lo:hi]
                    b_im = bu_ref[slot, r0 + s * SUBLANES:r0 + (s + 1) * SUBLANES, half + lo:half + hi]
                    v_re = a_re[j] * x_re[j] - a_im[j] * x_im[j] + b_re
                    v_im = a_re[j] * x_im[j] + a_im[j] * x_re[j] + b_im
                    x_re[j], x_im[j] = v_re, v_im
                    pair_re.append(v_re)
                    pair_im.append(v_im)
                xs_ref[slot, r0:r0 + 2 * SUBLANES, lo:hi] = jnp.concatenate(pair_re, axis=0).astype(BF16)
                xs_ref[slot, r0:r0 + 2 * SUBLANES, half + lo:half + hi] = jnp.concatenate(pair_im, axis=0).astype(BF16)

    def project_out(ci):
        slot = ci % 2
        y = (jnp.dot(xs_ref[slot, :, :half], wc_ref[0, :half, :], preferred_element_type=F32)
             + jnp.dot(xs_ref[slot, :, half:], wc_ref[0, half:, :], preferred_element_type=F32))
        u = u_ref[ci * rows:(ci + 1) * rows, :].astype(F32)
        y_ref[ci * rows:(ci + 1) * rows, :] = _gelu_tanh(y + d_ref[0] * u).astype(BF16)

    project_in(0)
    for ci in range(chunks):
        if ci + 1 < chunks:
            project_in(ci + 1)
        recur(ci)
        project_out(ci)

    for j in range(nl):
        st_ref[0, :, j * LANES:(j + 1) * LANES] = x_re[j]
        st_ref[1, :, j * LANES:(j + 1) * LANES] = x_im[j]


def _s5_scan(u_tm, w_b, w_c, a_tile, d_skip, steps, chunks):
    t, e = u_tm.shape
    nt, cw, sw = w_b.shape
    rows = steps * SUBLANES
    return pl.pallas_call(
        functools.partial(_s5_scan_kernel, steps=steps, chunks=chunks),
        out_shape=jax.ShapeDtypeStruct((t, e), BF16),
        grid=(nt, t // (rows * chunks)),
        in_specs=[
            pl.BlockSpec((rows * chunks, cw), lambda k, c: (c, k)),
            _resident((1, cw, sw), lambda k, c: (k, 0, 0)),
            _resident((1, sw, cw), lambda k, c: (k, 0, 0)),
            _resident((1, SUBLANES, sw), lambda k, c: (k, 0, 0)),
            _resident((1, 1, cw), lambda k, c: (k, 0, 0)),
        ],
        out_specs=pl.BlockSpec((rows * chunks, cw), lambda k, c: (c, k)),
        scratch_shapes=[
            pltpu.VMEM((2, rows, sw), F32),
            pltpu.VMEM((2, rows, sw), BF16),
            pltpu.VMEM((2, SUBLANES, sw // 2), F32),
        ],
        compiler_params=_params("parallel", "arbitrary"),
        name="s5_scan",
    )(u_tm, w_b, w_c, a_tile, d_skip.reshape(nt, 1, cw))


def _s5_out_kernel(y_ref, z_ref, x_ref, gate_ref, wg_ref, wo_ref, o_ref):
    y = y_ref[...]
    yf = y.astype(F32)
    y2 = yf * _sigmoid(jnp.dot(y, wg_ref[...], preferred_element_type=F32)) * _silu(z_ref[...].astype(F32))
    out = jnp.dot(y2.astype(BF16), wo_ref[...], preferred_element_type=F32)
    r, d = out.shape
    x = x_ref[...].reshape(r // SUBLANES, SUBLANES, d)
    o_ref[...] = (x + gate_ref[...][None] * out.reshape(r // SUBLANES, SUBLANES, d)).reshape(r, d)


def _s5_out(y, z, x_tm, gate, w_glu, w_out):
    t, d = x_tm.shape
    e = w_glu.shape[0]
    rows = min(ROW_TILE, t)
    row_spec = lambda c: pl.BlockSpec((rows, c), lambda i: (i, 0))
    return pl.pallas_call(
        _s5_out_kernel,
        out_shape=jax.ShapeDtypeStruct((t, d), F32),
        grid=(t // rows,),
        in_specs=[
            row_spec(e), row_spec(e), row_spec(d),
            _resident((SUBLANES, d), lambda i: (0, 0)),
            _resident((e, e), lambda i: (0, 0)),
            _resident((e, d), lambda i: (0, 0)),
        ],
        out_specs=row_spec(d),
        compiler_params=_params("parallel"),
        name="s5_out",
    )(y, z, x_tm, gate, w_glu.astype(BF16), w_out.astype(BF16))


def _gdn_in_kernel(x_ref, nw_ref, sc_ref, sh_ref, w_ref, wg_ref, cw_ref, gp_ref,
                   q_ref, k_ref, v_ref, z_ref, g_ref, ext_ref, st_ref, *, heads):
    halo = (GDN_CONV - 1) * SUBLANES
    r, d = x_ref.shape
    cc = ext_ref.shape[1]
    bsz = SUBLANES
    tmt = r // bsz
    qk = q_ref.shape[1] // bsz
    e = v_ref.shape[1] // bsz
    dk = qk // heads

    @pl.when(pl.program_id(0) == 0)
    def _():
        ext_ref[0:halo, :] = jnp.zeros((halo, cc), F32)

    h = _modulated_norm_rows(x_ref[...], nw_ref[...], sc_ref[...], sh_ref[...]).astype(BF16)

    ext_ref[halo:halo + r, :] = jnp.dot(h, w_ref[:, :cc], preferred_element_type=F32)
    conv = cw_ref[0:1, :] * ext_ref[0:r, :]
    for j in range(1, GDN_CONV):
        conv = conv + cw_ref[j:j + 1, :] * ext_ref[j * SUBLANES:j * SUBLANES + r, :]
    ext_ref[0:halo, :] = ext_ref[r:r + halo, :]
    act = _silu(conv)

    def stage(lo, val):
        for j in range(val.shape[1] // LANES):
            st_ref[lo // LANES + j] = val[:, j * LANES:(j + 1) * LANES]

    for hh in range(heads):
        qh = act[:, hh * dk:(hh + 1) * dk]
        kh = act[:, qk + hh * dk:qk + (hh + 1) * dk]
        stage(hh * dk, qh * lax.rsqrt(jnp.sum(qh * qh, axis=-1, keepdims=True) + NORM_EPS) * (dk ** -0.5))
        stage(qk + hh * dk, kh * lax.rsqrt(jnp.sum(kh * kh, axis=-1, keepdims=True) + NORM_EPS))
    stage(2 * qk, act[:, 2 * qk:])
    stage(cc, jnp.dot(h, w_ref[:, cc:], preferred_element_type=F32))

    logit = jnp.dot(h, wg_ref[...], preferred_element_type=F32)
    beta = _sigmoid(logit)
    g = -jnp.exp(gp_ref[0:1, :]) * _softplus(logit + gp_ref[1:2, :])
    lane = lax.broadcasted_iota(jnp.int32, logit.shape, 1)
    stage(cc + e, jnp.where(lane < heads, beta, g))

    for b in range(bsz):
        seq = lambda lo, hi: _sequence_rows(st_ref, b, tmt, lo, hi)
        q_ref[:, b * qk:(b + 1) * qk] = seq(0, qk).astype(BF16)
        k_ref[:, b * qk:(b + 1) * qk] = seq(qk, 2 * qk).astype(BF16)
        v_ref[:, b * e:(b + 1) * e] = seq(2 * qk, cc).astype(BF16)
        z_ref[:, b * e:(b + 1) * e] = seq(cc, cc + e).astype(BF16)
        g_ref[:, b * LANES:(b + 1) * LANES] = seq(cc + e, cc + e + LANES)


def _gdn_in(x_tm, norm_w, scale, shift, w_in, conv_w, a_log, dt_bias):
    t, d = x_tm.shape
    bsz = SUBLANES
    seqlen = t // bsz
    heads = a_log.shape[0]
    cc = conv_w.shape[1]
    e = (w_in.shape[1] - cc - 2 * heads)
    qk = (cc - e) // 2
    rows = min(GDN_IN_ROWS, t)
    tmt = rows // bsz
    w_main = w_in[:, :cc + e].astype(BF16)
    w_gate = jnp.zeros((d, LANES), F32).at[:, :2 * heads].set(w_in[:, cc + e:]).astype(BF16)
    gate_par = jnp.zeros((2, LANES), F32)
    gate_par = gate_par.at[0, heads:2 * heads].set(a_log).at[1, heads:2 * heads].set(dt_bias)
    seq_spec = lambda c: pl.BlockSpec((tmt, bsz * c), lambda i: (i, 0))
    return pl.pallas_call(
        functools.partial(_gdn_in_kernel, heads=heads),
        out_shape=[
            jax.ShapeDtypeStruct((seqlen, bsz * qk), BF16),
            jax.ShapeDtypeStruct((seqlen, bsz * qk), BF16),
            jax.ShapeDtypeStruct((seqlen, bsz * e), BF16),
            jax.ShapeDtypeStruct((seqlen, bsz * e), BF16),
            jax.ShapeDtypeStruct((seqlen, bsz * LANES), F32),
        ],
        grid=(t // rows,),
        in_specs=[
            pl.BlockSpec((rows, d), lambda i: (i, 0)),
            _resident((1, d), lambda i: (0, 0)),
            _resident((SUBLANES, d), lambda i: (0, 0)),
            _resident((SUBLANES, d), lambda i: (0, 0)),
            _resident((d, cc + e), lambda i: (0, 0)),
            _resident((d, LANES), lambda i: (0, 0)),
            _resident((GDN_CONV, cc), lambda i: (0, 0)),
            _resident((2, LANES), lambda i: (0, 0)),
        ],
        out_specs=[seq_spec(qk), seq_spec(qk), seq_spec(e), seq_spec(e), seq_spec(LANES)],
        scratch_shapes=[
            pltpu.VMEM((rows + (GDN_CONV - 1) * SUBLANES, cc), F32),
            pltpu.VMEM(((cc + e + LANES) // LANES, rows, LANES), F32),
        ],
        compiler_params=_params("arbitrary"),
        name="gdn_in",
    )(x_tm, norm_w.reshape(1, d), scale, shift, w_main, w_gate, conv_w, gate_par)


def _pair_block_diag(x):
    c = x.shape[0]
    lane = lax.broadcasted_iota(jnp.int32, x.shape, 1)
    zero = jnp.zeros_like(x)
    return jnp.concatenate([jnp.where(lane < c, x, zero), jnp.where(lane >= c, x, zero)], axis=0)


def _pair_mm(x, y):
    return jnp.dot(x.astype(BF16), _pair_block_diag(y.astype(BF16)), preferred_element_type=F32)


def _gdn_core_kernel(q_ref, k_ref, v_ref, z_ref, g_ref, nw_ref, y_ref, s_ref, *, heads, nb):
    @pl.when(pl.program_id(1) == 0)
    def _():
        s_ref[...] = jnp.zeros_like(s_ref)

    c = q_ref.shape[0]
    qk_w = q_ref.shape[1] // nb
    e_w = v_ref.shape[1] // nb
    dk = qk_w // heads
    dv = e_w // heads
    units = [(b, p) for b in range(nb) for p in range(heads // 2)]

    row = lax.broadcasted_iota(jnp.int32, (c, 2 * c), 0)
    col = lax.broadcasted_iota(jnp.int32, (c, 2 * c), 1) & (c - 1)
    causal = row >= col
    strict = row > col
    eye = (row == col).astype(F32)
    row1 = lax.broadcasted_iota(jnp.int32, (c, c), 0)
    col1 = lax.broadcasted_iota(jnp.int32, (c, c), 1)
    tri = (row1 >= col1).astype(BF16)
    nh2 = 2 * heads
    eye_h = (lax.broadcasted_iota(jnp.int32, (nh2, nh2), 0)
             == lax.broadcasted_iota(jnp.int32, (nh2, nh2), 1)).astype(BF16)

    def same_block(shift):
        return (row >> shift) == (col >> shift)

    gb, gc, gc_t = [], [], []
    for b in range(nb):
        g = g_ref[:, b * LANES:b * LANES + nh2]
        cs = sum(jnp.dot(tri, part, preferred_element_type=F32) for part in _split3(g))
        gb.append(g)
        gc.append(cs)
        gc_t.append(sum(lax.dot_general(eye_h, part, (((1,), (1,)), ((), ())), preferred_element_type=F32)
                        for part in _split3(cs)))

    def lanes2(ref, b, p, width, per_batch):
        return ref[:, b * per_batch + 2 * p * width:b * per_batch + 2 * (p + 1) * width]

    def pair_cols(b, p, fn, width):
        return jnp.concatenate([jnp.broadcast_to(fn(b, 2 * p + i), (c, width)) for i in range(2)], axis=1)

    beta_c = lambda b, h: gb[b][:, h:h + 1]
    gcol = lambda b, h: gc[b][:, heads + h:heads + h + 1]
    grow = lambda b, h: gc_t[b][heads + h:heads + h + 1, :]

    a_mat, qk_mat = {}, {}
    for (b, p) in units:
        qp = lanes2(q_ref, b, p, dk, qk_w)
        kp = lanes2(k_ref, b, p, dk, qk_w)
        res = lax.dot_general(jnp.concatenate([qp, kp], axis=0), _pair_block_diag(kp),
                              (((1,), (1,)), ((), ())), preferred_element_type=F32)
        decay = jnp.where(causal, jnp.exp(jnp.concatenate(
            [gcol(b, 2 * p + i) - grow(b, 2 * p + i) for i in range(2)], axis=1)), 0.0)
        qk_mat[b, p] = res[:c] * decay
        a_mat[b, p] = jnp.where(strict, pair_cols(b, p, beta_c, c) * res[c:] * decay, 0.0)

    shift0 = GDN_BASE_BLOCK.bit_length() - 1
    dblk = {u: jnp.where(same_block(shift0), a_mat[u], 0.0) for u in units}
    d2 = {u: _pair_mm(dblk[u], dblk[u]) for u in units}
    d4 = {u: _pair_mm(d2[u], d2[u]) for u in units}
    t_inv = {u: eye - dblk[u] for u in units}
    for pw in (d2, d4):
        upd = {u: _pair_mm(t_inv[u], pw[u]) for u in units}
        t_inv = {u: t_inv[u] + upd[u] for u in units}
    shift = shift0
    while (1 << shift) < c:
        level = same_block(shift + 1) & jnp.logical_not(same_block(shift))
        xt = {u: _pair_mm(jnp.where(level, a_mat[u], 0.0), t_inv[u]) for u in units}
        upd = {u: _pair_mm(t_inv[u], xt[u]) for u in units}
        t_inv = {u: t_inv[u] - upd[u] for u in units}
        shift += 1

    w_pk, u_pk, gam_pk = {}, {}, {}
    for (b, p) in units:
        gam_pk[b, p] = jnp.exp(pair_cols(b, p, gcol, dk))
        kp = lanes2(k_ref, b, p, dk, qk_w).astype(F32)
        vp = lanes2(v_ref, b, p, dv, e_w).astype(F32)
        rw = (pair_cols(b, p, beta_c, dk) * gam_pk[b, p] * kp).astype(BF16)
        ru = (pair_cols(b, p, beta_c, dv) * vp).astype(BF16)
        zk = jnp.zeros((c, dk), BF16)
        zv = jnp.zeros((c, dv), BF16)
        rhs = jnp.concatenate([
            jnp.concatenate([rw[:, :dk], zk, ru[:, :dv], zv], axis=1),
            jnp.concatenate([zk, rw[:, dk:], zv, ru[:, dv:]], axis=1)], axis=0)
        wu = jnp.dot(t_inv[b, p].astype(BF16), rhs, preferred_element_type=F32)
        w_pk[b, p] = wu[:, :2 * dk]
        u_pk[b, p] = wu[:, 2 * dk:]

    hunits = [(b, h) for b in range(nb) for h in range(heads)]
    ws, states = {}, {}
    for (b, h) in hunits:
        p, i = divmod(h, 2)
        qd = lanes2(q_ref, b, p, dk, qk_w).astype(F32) * gam_pk[b, p]
        lhs = jnp.concatenate([w_pk[b, p][:, i * dk:(i + 1) * dk], qd[:, i * dk:(i + 1) * dk]], axis=0)
        states[b, h] = s_ref[b * heads + h]
        ws[b, h] = _bdot(lhs, states[b, h])
    v_new = {(b, h): u_pk[b, h // 2][:, (h % 2) * dv:(h % 2 + 1) * dv] - ws[b, h][:c] for (b, h) in hunits}
    o_intra = {(b, h): _bdot(qk_mat[b, h // 2][:, (h % 2) * c:(h % 2 + 1) * c], v_new[b, h]) for (b, h) in hunits}
    for (b, h) in hunits:
        g_last = gcol(b, h)[c - 1:c, :]
        kh = k_ref[:, b * qk_w + h * dk:b * qk_w + (h + 1) * dk].astype(F32)
        k_dec = kh * jnp.exp(g_last - gcol(b, h))
        s_ref[b * heads + h] = jnp.exp(g_last) * states[b, h] + _bdot_tn(k_dec, v_new[b, h])
    for (b, h) in hunits:
        o = ws[b, h][c:] + o_intra[b, h]
        on = _rms(o) * nw_ref[...]
        zh = z_ref[:, b * e_w + h * dv:b * e_w + (h + 1) * dv].astype(F32)
        y_ref[:, b * e_w + h * dv:b * e_w + (h + 1) * dv] = (on * _silu(zh)).astype(BF16)


def _gdn_core(q, k, v, z, gates, norm_w, heads):
    seqlen = q.shape[0]
    nb = GDN_BATCH_PER_STEP
    bsz = gates.shape[1] // LANES
    qk = q.shape[1] // bsz
    e = v.shape[1] // bsz
    c = min(GDN_CHUNK, seqlen)
    col_spec = lambda w: pl.BlockSpec((c, nb * w), lambda b, n: (n, b))
    return pl.pallas_call(
        functools.partial(_gdn_core_kernel, heads=heads, nb=nb),
        out_shape=jax.ShapeDtypeStruct((seqlen, bsz * e), BF16),
        grid=(bsz // nb, seqlen // c),
        in_specs=[col_spec(qk), col_spec(qk), col_spec(e), col_spec(e), col_spec(LANES),
                  _resident((1, e // heads), lambda b, n: (0, 0))],
        out_specs=col_spec(e),
        scratch_shapes=[pltpu.VMEM((nb * heads, qk // heads, e // heads), F32)],
        compiler_params=_params("parallel", "arbitrary"),
        name="gdn_core",
    )(q, k, v, z, gates, norm_w.reshape(1, e // heads))


def _gdn_out_kernel(y_ref, x_ref, gate_ref, w_ref, fw_ref, o_ref, slab_ref):
    bsz, tmt, d = o_ref.shape
    e = y_ref.shape[1] // bsz
    for j in range(d // LANES):
        slab_ref[j] = x_ref[:, j * LANES:(j + 1) * LANES]
    y = jnp.concatenate([y_ref[:, b * e:(b + 1) * e] for b in range(bsz)], axis=0)
    out = jnp.dot(y, w_ref[...], preferred_element_type=F32).reshape(bsz, tmt, d)
    for b in range(bsz):
        xn = _sequence_rows(slab_ref, b, tmt, 0, d) + gate_ref[b:b + 1, :] * out[b]
        o_ref[b] = _rms(xn) * fw_ref[...]


def _gdn_out(y, x_tm, gate, w_out, final_w):
    e, d = w_out.shape
    seqlen = y.shape[0]
    bsz = y.shape[1] // e
    tmt = min(ROW_TILE // bsz, seqlen)
    return pl.pallas_call(
        _gdn_out_kernel,
        out_shape=jax.ShapeDtypeStruct((bsz, seqlen, d), F32),
        grid=(seqlen // tmt,),
        in_specs=[
            pl.BlockSpec((tmt, bsz * e), lambda i: (i, 0)),
            pl.BlockSpec((tmt * bsz, d), lambda i: (i, 0)),
            _resident((bsz, d), lambda i: (0, 0)),
            _resident((e, d), lambda i: (0, 0)),
            _resident((1, d), lambda i: (0, 0)),
        ],
        out_specs=pl.BlockSpec((bsz, tmt, d), lambda i: (0, i, 0)),
        scratch_shapes=[pltpu.VMEM((d // LANES, tmt * bsz, LANES), F32)],
        compiler_params=_params("parallel"),
        name="gdn_out",
    )(y, x_tm, gate, w_out.astype(BF16), final_w.reshape(1, d))


def kernel(x, c, ada_w, ada_b, norm_w, s5_w_in, s5_lambda_re, s5_lambda_im, s5_log_dt, s5_b_re, s5_b_im,
           s5_c_re, s5_c_im, s5_d, s5_w_glu, s5_w_out, gdn_w_in, gdn_conv_w, gdn_a_log, gdn_dt_bias,
           gdn_norm_w, gdn_w_out, final_norm_w):
    bsz, seqlen, d = x.shape
    assert bsz == SUBLANES, "the time-major layout keeps one batch row per vector sublane"
    assert ada_w.shape[0] == 2 and s5_w_in.shape[0] == 1 and gdn_w_in.shape[0] == 1
    t = seqlen * bsz

    mod = _modulation(c, ada_w, ada_b)
    shift0, scale0, gate0 = mod[0, :, :d], mod[0, :, d:2 * d], mod[0, :, 2 * d:]
    shift1, scale1, gate1 = mod[1, :, :d], mod[1, :, d:2 * d], mod[1, :, 2 * d:]

    a_re, a_im, bb_re, bb_im = _s5_discretise(s5_lambda_re[0], s5_lambda_im[0], s5_log_dt[0], s5_b_re[0], s5_b_im[0])
    w_b, w_c, a_tile = _s5_block_weights(a_re, a_im, bb_re, bb_im, s5_c_re[0], s5_c_im[0])
    x_tm, u, z = _s5_in(x, norm_w[0], scale0, shift0, s5_w_in[0])
    steps = min(S5_STEPS, seqlen)
    chunks = min(S5_CHUNKS, seqlen // steps)
    y = _s5_scan(u, w_b, w_c, a_tile, s5_d[0], steps, chunks)
    x1_tm = _s5_out(y, z, x_tm, gate0, s5_w_glu[0], s5_w_out[0])

    heads = gdn_a_log.shape[1]
    q, k, v, zg, gates = _gdn_in(x1_tm, norm_w[1], scale1, shift1, gdn_w_in[0], gdn_conv_w[0],
                                 gdn_a_log[0], gdn_dt_bias[0])
    yg = _gdn_core(q, k, v, zg, gates, gdn_norm_w[0], heads)
    return _gdn_out(yg, x1_tm, gate1, gdn_w_out[0], final_norm_w)
```

```python
import functools
import math

import jax
import jax.numpy as jnp
from jax import lax
from jax.experimental import pallas as pl
from jax.experimental.pallas import tpu as pltpu

F32 = jnp.float32
BF16 = jnp.bfloat16

NORM_EPS = 1e-6
S5_GROUP = 16
S5_STATE = 64
S5_TILE_GROUPS = 16
S5_STEPS = 64
S5_CHUNKS = 8
GDN_HEADS = 8
GDN_CONV = 4
GDN_CHUNK = 128
GDN_BASE_BLOCK = 8
GDN_BATCH_PER_STEP = 4
ROW_TILE = 512
GDN_IN_ROWS = 256
SUBLANES = 8
LANES = 128
VMEM_LIMIT_BYTES = 56 * 1024 * 1024


def _sigmoid(x):
    return 1.0 / (1.0 + jnp.exp(-x))


def _silu(x):
    return x * _sigmoid(x)


def _gelu_tanh(x):
    c = math.sqrt(2.0 / math.pi)
    return 0.5 * x * (1.0 + jnp.tanh(c * (x + 0.044715 * (x * x * x))))


def _softplus(x):
    return jnp.maximum(x, 0.0) + jnp.log(1.0 + jnp.exp(-jnp.abs(x)))


def _rms(x):
    return x * lax.rsqrt(jnp.mean(x * x, axis=-1, keepdims=True) + NORM_EPS)


def _bdot(a, b):
    return jnp.dot(a.astype(BF16), b.astype(BF16), preferred_element_type=F32)


def _bdot_nt(a, b):
    return lax.dot_general(a.astype(BF16), b.astype(BF16), (((1,), (1,)), ((), ())),
                           preferred_element_type=F32)


def _bdot_tn(a, b):
    return lax.dot_general(a.astype(BF16), b.astype(BF16), (((0,), (0,)), ((), ())),
                           preferred_element_type=F32)


def _split3(x):
    hi = x.astype(BF16)
    r = x - hi.astype(F32)
    mid = r.astype(BF16)
    lo = (r - mid.astype(F32)).astype(BF16)
    return hi, mid, lo


def _params(*sem):
    return pltpu.CompilerParams(dimension_semantics=sem, vmem_limit_bytes=VMEM_LIMIT_BYTES)


def _resident(shape, index_map):
    return pl.BlockSpec(shape, index_map, pipeline_mode=pl.Buffered(1))


MOD_SHIFT, MOD_SCALE, MOD_GATE = 0, 1, 2


def _mod_kernel(c_ref, w_ref, b_ref, o_ref):
    o_ref[...] = _bdot(_silu(c_ref[...]), w_ref[0]) + b_ref[0]


def _modulation(c, ada_w, ada_b):
    depth, d, d3 = ada_w.shape
    bsz = c.shape[0]
    nb = d3 // d
    return pl.pallas_call(
        _mod_kernel,
        out_shape=jax.ShapeDtypeStruct((depth * bsz, d3), F32),
        grid=(depth, nb),
        in_specs=[
            pl.BlockSpec((bsz, d), lambda l, j: (0, 0)),
            pl.BlockSpec((1, d, d), lambda l, j: (l, 0, j)),
            pl.BlockSpec((1, 1, d), lambda l, j: (l, 0, j)),
        ],
        out_specs=pl.BlockSpec((bsz, d), lambda l, j: (l, j)),
        compiler_params=_params("parallel", "parallel"),
        name="adaln_mod",
    )(c, ada_w.astype(BF16), ada_b.reshape(depth, 1, d3))


def _s5_disc_kernel(lre_ref, lim_ref, ldt_ref, are_ref, aim_ref, qre_ref, qim_ref):
    lre = lre_ref[...]
    lim = lim_ref[...]
    dt = jnp.exp(ldt_ref[...])
    mag = jnp.exp(lre * dt)
    are = mag * jnp.cos(lim * dt)
    aim = mag * jnp.sin(lim * dt)
    den = lre * lre + lim * lim
    nr = are - 1.0
    ni = aim
    are_ref[...] = are
    aim_ref[...] = aim
    qre_ref[...] = (nr * lre + ni * lim) / den
    qim_ref[...] = (ni * lre - nr * lim) / den


def _s5_bbar_kernel(qre_ref, qim_ref, bre_ref, bim_ref, ore_ref, oim_ref):
    qre = qre_ref[...]
    qim = qim_ref[...]
    bre = bre_ref[...]
    bim = bim_ref[...]
    ore_ref[...] = qre * bre - qim * bim
    oim_ref[...] = qre * bim + qim * bre


def _s5_discretise(lam_re, lam_im, log_dt, b_re, b_im):
    g, p = lam_re.shape
    m = b_re.shape[-1]
    a_re, a_im, q_re, q_im = pl.pallas_call(
        _s5_disc_kernel,
        out_shape=[jax.ShapeDtypeStruct((g, p), F32)] * 4,
        name="s5_disc",
    )(lam_re, lam_im, log_dt.reshape(g, 1))
    rows = g * p
    blk = 1024
    bb_re, bb_im = pl.pallas_call(
        _s5_bbar_kernel,
        out_shape=[jax.ShapeDtypeStruct((rows, m), F32)] * 2,
        grid=(rows // blk,),
        in_specs=[pl.BlockSpec((blk, 1), lambda i: (i, 0))] * 2 + [pl.BlockSpec((blk, m), lambda i: (i, 0))] * 2,
        out_specs=[pl.BlockSpec((blk, m), lambda i: (i, 0))] * 2,
        compiler_params=_params("parallel"),
        name="s5_bbar",
    )(q_re.reshape(rows, 1), q_im.reshape(rows, 1), b_re.reshape(rows, m), b_im.reshape(rows, m))
    return a_re, a_im, bb_re.reshape(g, p, m), bb_im.reshape(g, p, m)


def _s5_block_weights(a_re, a_im, bb_re, bb_im, c_re, c_im):
    g, p, m = bb_re.shape
    tg = S5_TILE_GROUPS
    nt = g // tg
    eye = jnp.eye(tg, dtype=F32)

    def in_blocks(bb):
        t = bb.reshape(nt, tg, p, m).transpose(0, 1, 3, 2)
        return (t[:, :, :, None, :] * eye[None, :, None, :, None]).reshape(nt, tg * m, tg * p)

    def out_blocks(cc):
        t = cc.reshape(nt, tg, m, p).transpose(0, 1, 3, 2)
        return (t[:, :, :, None, :] * eye[None, :, None, :, None]).reshape(nt, tg * p, tg * m)

    w_b = jnp.concatenate([in_blocks(bb_re), in_blocks(bb_im)], axis=2).astype(BF16)
    w_c = jnp.concatenate([out_blocks(c_re), -out_blocks(c_im)], axis=1).astype(BF16)
    a_tile = jnp.concatenate([a_re.reshape(nt, 1, tg * p), a_im.reshape(nt, 1, tg * p)], axis=2)
    a_tile = jnp.broadcast_to(a_tile, (nt, SUBLANES, 2 * tg * p))
    return w_b, w_c, a_tile


def _modulated_norm_rows(x, nw, scale, shift):
    r, d = x.shape
    y = (_rms(x) * nw).reshape(r // SUBLANES, SUBLANES, d)
    return (y * (1.0 + scale)[None] + shift[None]).reshape(r, d)


def _sequence_rows(slab_ref, b, tmt, lo, hi):
    return jnp.concatenate([slab_ref[j, pl.ds(b, tmt, stride=SUBLANES), :]
                            for j in range(lo // LANES, hi // LANES)], axis=1)


def _s5_in_kernel(x_ref, nw_ref, sc_ref, sh_ref, w_ref, xt_ref, u_ref, z_ref, slab_ref):
    bsz, tmt, d = x_ref.shape
    for j in range(d // LANES):
        for b in range(bsz):
            slab_ref[j, pl.ds(b, tmt, stride=bsz), :] = x_ref[b, :, j * LANES:(j + 1) * LANES]
    x_tm = jnp.concatenate([slab_ref[j] for j in range(d // LANES)], axis=1)
    xt_ref[...] = x_tm
    h = _modulated_norm_rows(x_tm, nw_ref[...], sc_ref[...], sh_ref[...]).astype(BF16)
    e = z_ref.shape[1]
    nt, _, cw = u_ref.shape
    u = jnp.dot(h, w_ref[:, :e], preferred_element_type=F32).astype(BF16)
    for k in range(nt):
        u_ref[k] = u[:, k * cw:(k + 1) * cw]
    z_ref[...] = jnp.dot(h, w_ref[:, e:], preferred_element_type=F32).astype(BF16)


def _s5_in(x, norm_w, mod, layer, w_in):
    bsz, seqlen, d = x.shape
    e = w_in.shape[1] // 2
    tmt = min(ROW_TILE // bsz, seqlen)
    rows = tmt * bsz
    t = seqlen * bsz
    cw = S5_TILE_GROUPS * S5_GROUP
    nt = e // cw
    return pl.pallas_call(
        _s5_in_kernel,
        out_shape=[jax.ShapeDtypeStruct((t, d), F32), jax.ShapeDtypeStruct((nt, t, cw), BF16),
                   jax.ShapeDtypeStruct((t, e), BF16)],
        grid=(seqlen // tmt,),
        in_specs=[
            pl.BlockSpec((bsz, tmt, d), lambda i: (0, i, 0)),
            _resident((1, d), lambda i: (0, 0)),
            _resident((bsz, d), lambda i: (layer, MOD_SCALE)),
            _resident((bsz, d), lambda i: (layer, MOD_SHIFT)),
            _resident((d, 2 * e), lambda i: (0, 0)),
        ],
        out_specs=[pl.BlockSpec((rows, d), lambda i: (i, 0)), pl.BlockSpec((nt, rows, cw), lambda i: (0, i, 0)),
                   pl.BlockSpec((rows, e), lambda i: (i, 0))],
        scratch_shapes=[pltpu.VMEM((d // LANES, rows, LANES), F32)],
        compiler_params=_params("parallel"),
        name="s5_in",
    )(x, norm_w.reshape(1, d), mod, mod, w_in.astype(BF16))


def _s5_scan_kernel(u_ref, wb_ref, wc_ref, a_ref, d_ref, y_ref, bu_ref, xs_ref, st_ref, *, steps, chunks):
    @pl.when(pl.program_id(1) == 0)
    def _():
        st_ref[...] = jnp.zeros_like(st_ref)

    rows = steps * SUBLANES
    half = bu_ref.shape[2] // 2
    nl = half // LANES
    a_re = [a_ref[0, :, j * LANES:(j + 1) * LANES] for j in range(nl)]
    a_im = [a_ref[0, :, half + j * LANES:half + (j + 1) * LANES] for j in range(nl)]
    x_re = [st_ref[0, :, j * LANES:(j + 1) * LANES] for j in range(nl)]
    x_im = [st_ref[1, :, j * LANES:(j + 1) * LANES] for j in range(nl)]

    def project_in(ci):
        u = u_ref[0, ci * rows:(ci + 1) * rows, :]
        slot = ci % 2
        bu_ref[slot, :, :half] = jnp.dot(u, wb_ref[0, :, :half], preferred_element_type=F32)
        bu_ref[slot, :, half:] = jnp.dot(u, wb_ref[0, :, half:], preferred_element_type=F32)

    def recur(ci):
        slot = ci % 2
        for tt in range(steps // 2):
            r0 = 2 * tt * SUBLANES
            for j in range(nl):
                lo, hi = j * LANES, (j + 1) * LANES
                pair_re, pair_im = [], []
                for s in range(2):
                    b_re = bu_ref[slot, r0 + s * SUBLANES:r0 + (s + 1) * SUBLANES, lo:hi]
                    b_im = bu_ref[slot, r0 + s * SUBLANES:r0 + (s + 1) * SUBLANES, half + lo:half + hi]
                    v_re = a_re[j] * x_re[j] - a_im[j] * x_im[j] + b_re
                    v_im = a_re[j] * x_im[j] + a_im[j] * x_re[j] + b_im
                    x_re[j], x_im[j] = v_re, v_im
                    pair_re.append(v_re)
                    pair_im.append(v_im)
                xs_ref[slot, r0:r0 + 2 * SUBLANES, lo:hi] = jnp.concatenate(pair_re, axis=0).astype(BF16)
                xs_ref[slot, r0:r0 + 2 * SUBLANES, half + lo:half + hi] = jnp.concatenate(pair_im, axis=0).astype(BF16)

    def project_out(ci):
        slot = ci % 2
        y = (jnp.dot(xs_ref[slot, :, :half], wc_ref[0, :half, :], preferred_element_type=F32)
             + jnp.dot(xs_ref[slot, :, half:], wc_ref[0, half:, :], preferred_element_type=F32))
        u = u_ref[0, ci * rows:(ci + 1) * rows, :].astype(F32)
        y_ref[0, ci * rows:(ci + 1) * rows, :] = _gelu_tanh(y + d_ref[0] * u).astype(BF16)

    project_in(0)
    for ci in range(chunks):
        if ci + 1 < chunks:
            project_in(ci + 1)
        recur(ci)
        project_out(ci)

    for j in range(nl):
        st_ref[0, :, j * LANES:(j + 1) * LANES] = x_re[j]
        st_ref[1, :, j * LANES:(j + 1) * LANES] = x_im[j]


def _s5_scan(u_tiles, w_b, w_c, a_tile, d_skip, steps, chunks):
    nt, cw, sw = w_b.shape
    t = u_tiles.shape[1]
    rows = steps * SUBLANES
    return pl.pallas_call(
        functools.partial(_s5_scan_kernel, steps=steps, chunks=chunks),
        out_shape=jax.ShapeDtypeStruct((nt, t, cw), BF16),
        grid=(nt, t // (rows * chunks)),
        in_specs=[
            pl.BlockSpec((1, rows * chunks, cw), lambda k, c: (k, c, 0)),
            _resident((1, cw, sw), lambda k, c: (k, 0, 0)),
            _resident((1, sw, cw), lambda k, c: (k, 0, 0)),
            _resident((1, SUBLANES, sw), lambda k, c: (k, 0, 0)),
            _resident((1, 1, cw), lambda k, c: (k, 0, 0)),
        ],
        out_specs=pl.BlockSpec((1, rows * chunks, cw), lambda k, c: (k, c, 0)),
        scratch_shapes=[
            pltpu.VMEM((2, rows, sw), F32),
            pltpu.VMEM((2, rows, sw), BF16),
            pltpu.VMEM((2, SUBLANES, sw // 2), F32),
        ],
        compiler_params=_params("parallel", "arbitrary"),
        name="s5_scan",
    )(u_tiles, w_b, w_c, a_tile, d_skip.reshape(nt, 1, cw))


def _s5_out_kernel(y_ref, z_ref, x_ref, gate_ref, wg_ref, wo_ref, o_ref):
    y = jnp.concatenate([y_ref[k] for k in range(y_ref.shape[0])], axis=1)
    yf = y.astype(F32)
    y2 = yf * _sigmoid(jnp.dot(y, wg_ref[...], preferred_element_type=F32)) * _silu(z_ref[...].astype(F32))
    out = jnp.dot(y2.astype(BF16), wo_ref[...], preferred_element_type=F32)
    r, d = out.shape
    x = x_ref[...].reshape(r // SUBLANES, SUBLANES, d)
    o_ref[...] = (x + gate_ref[...][None] * out.reshape(r // SUBLANES, SUBLANES, d)).reshape(r, d)


def _s5_out(y, z, x_tm, mod, layer, w_glu, w_out):
    t, d = x_tm.shape
    e = w_glu.shape[0]
    nt, _, cw = y.shape
    rows = min(ROW_TILE, t)
    row_spec = lambda c: pl.BlockSpec((rows, c), lambda i: (i, 0))
    return pl.pallas_call(
        _s5_out_kernel,
        out_shape=jax.ShapeDtypeStruct((t, d), F32),
        grid=(t // rows,),
        in_specs=[
            pl.BlockSpec((nt, rows, cw), lambda i: (0, i, 0)), row_spec(e), row_spec(d),
            _resident((SUBLANES, d), lambda i: (layer, MOD_GATE)),
            _resident((e, e), lambda i: (0, 0)),
            _resident((e, d), lambda i: (0, 0)),
        ],
        out_specs=row_spec(d),
        compiler_params=_params("parallel"),
        name="s5_out",
    )(y, z, x_tm, mod, w_glu.astype(BF16), w_out.astype(BF16))


def _gdn_in_kernel(x_ref, nw_ref, sc_ref, sh_ref, w_ref, wg_ref, cw_ref, gp_ref,
                   q_ref, k_ref, v_ref, z_ref, g_ref, ext_ref, st_ref, *, heads):
    halo = (GDN_CONV - 1) * SUBLANES
    r, d = x_ref.shape
    cc = ext_ref.shape[1]
    bsz = SUBLANES
    tmt = r // bsz
    qk = q_ref.shape[1] // bsz
    e = v_ref.shape[1] // bsz
    dk = qk // heads

    @pl.when(pl.program_id(0) == 0)
    def _():
        ext_ref[0:halo, :] = jnp.zeros((halo, cc), F32)

    h = _modulated_norm_rows(x_ref[...], nw_ref[...], sc_ref[...], sh_ref[...]).astype(BF16)

    ext_ref[halo:halo + r, :] = jnp.dot(h, w_ref[:, :cc], preferred_element_type=F32)
    conv = cw_ref[0:1, :] * ext_ref[0:r, :]
    for j in range(1, GDN_CONV):
        conv = conv + cw_ref[j:j + 1, :] * ext_ref[j * SUBLANES:j * SUBLANES + r, :]
    ext_ref[0:halo, :] = ext_ref[r:r + halo, :]
    act = _silu(conv)

    def stage(lo, val):
        for j in range(val.shape[1] // LANES):
            st_ref[lo // LANES + j] = val[:, j * LANES:(j + 1) * LANES]

    for hh in range(heads):
        qh = act[:, hh * dk:(hh + 1) * dk]
        kh = act[:, qk + hh * dk:qk + (hh + 1) * dk]
        stage(hh * dk, qh * lax.rsqrt(jnp.sum(qh * qh, axis=-1, keepdims=True) + NORM_EPS) * (dk ** -0.5))
        stage(qk + hh * dk, kh * lax.rsqrt(jnp.sum(kh * kh, axis=-1, keepdims=True) + NORM_EPS))
    stage(2 * qk, act[:, 2 * qk:])
    stage(cc, jnp.dot(h, w_ref[:, cc:], preferred_element_type=F32))

    logit = jnp.dot(h, wg_ref[...], preferred_element_type=F32)
    beta = _sigmoid(logit)
    g = -jnp.exp(gp_ref[0:1, :]) * _softplus(logit + gp_ref[1:2, :])
    lane = lax.broadcasted_iota(jnp.int32, logit.shape, 1)
    stage(cc + e, jnp.where(lane < heads, beta, g))

    for b in range(bsz):
        seq = lambda lo, hi: _sequence_rows(st_ref, b, tmt, lo, hi)
        q_ref[:, b * qk:(b + 1) * qk] = seq(0, qk).astype(BF16)
        k_ref[:, b * qk:(b + 1) * qk] = seq(qk, 2 * qk).astype(BF16)
        v_ref[:, b * e:(b + 1) * e] = seq(2 * qk, cc).astype(BF16)
        z_ref[:, b * e:(b + 1) * e] = seq(cc, cc + e).astype(BF16)
        g_ref[:, b * LANES:(b + 1) * LANES] = seq(cc + e, cc + e + LANES)


def _gdn_in(x_tm, norm_w, mod, layer, w_in, conv_w, a_log, dt_bias):
    t, d = x_tm.shape
    bsz = SUBLANES
    seqlen = t // bsz
    heads = a_log.shape[0]
    cc = conv_w.shape[1]
    e = (w_in.shape[1] - cc - 2 * heads)
    qk = (cc - e) // 2
    rows = min(GDN_IN_ROWS, t)
    tmt = rows // bsz
    w_main = w_in[:, :cc + e].astype(BF16)
    w_gate = jnp.zeros((d, LANES), F32).at[:, :2 * heads].set(w_in[:, cc + e:]).astype(BF16)
    gate_par = jnp.zeros((2, LANES), F32)
    gate_par = gate_par.at[0, heads:2 * heads].set(a_log).at[1, heads:2 * heads].set(dt_bias)
    seq_spec = lambda c: pl.BlockSpec((tmt, bsz * c), lambda i: (i, 0))
    return pl.pallas_call(
        functools.partial(_gdn_in_kernel, heads=heads),
        out_shape=[
            jax.ShapeDtypeStruct((seqlen, bsz * qk), BF16),
            jax.ShapeDtypeStruct((seqlen, bsz * qk), BF16),
            jax.ShapeDtypeStruct((seqlen, bsz * e), BF16),
            jax.ShapeDtypeStruct((seqlen, bsz * e), BF16),
            jax.ShapeDtypeStruct((seqlen, bsz * LANES), F32),
        ],
        grid=(t // rows,),
        in_specs=[
            pl.BlockSpec((rows, d), lambda i: (i, 0)),
            _resident((1, d), lambda i: (0, 0)),
            _resident((SUBLANES, d), lambda i: (layer, MOD_SCALE)),
            _resident((SUBLANES, d), lambda i: (layer, MOD_SHIFT)),
            _resident((d, cc + e), lambda i: (0, 0)),
            _resident((d, LANES), lambda i: (0, 0)),
            _resident((GDN_CONV, cc), lambda i: (0, 0)),
            _resident((2, LANES), lambda i: (0, 0)),
        ],
        out_specs=[seq_spec(qk), seq_spec(qk), seq_spec(e), seq_spec(e), seq_spec(LANES)],
        scratch_shapes=[
            pltpu.VMEM((rows + (GDN_CONV - 1) * SUBLANES, cc), F32),
            pltpu.VMEM(((cc + e + LANES) // LANES, rows, LANES), F32),
        ],
        compiler_params=_params("arbitrary"),
        name="gdn_in",
    )(x_tm, norm_w.reshape(1, d), mod, mod, w_main, w_gate, conv_w, gate_par)


def _pair_block_diag(x):
    c = x.shape[0]
    lane = lax.broadcasted_iota(jnp.int32, x.shape, 1)
    zero = jnp.zeros_like(x)
    return jnp.concatenate([jnp.where(lane < c, x, zero), jnp.where(lane >= c, x, zero)], axis=0)


def _pair_mm(x, y):
    return jnp.dot(x.astype(BF16), _pair_block_diag(y.astype(BF16)), preferred_element_type=F32)


def _gdn_core_kernel(q_ref, k_ref, v_ref, z_ref, g_ref, nw_ref, y_ref, s_ref, *, heads, nb):
    @pl.when(pl.program_id(1) == 0)
    def _():
        s_ref[...] = jnp.zeros_like(s_ref)

    c = q_ref.shape[0]
    qk_w = q_ref.shape[1] // nb
    e_w = v_ref.shape[1] // nb
    dk = qk_w // heads
    dv = e_w // heads
    units = [(b, p) for b in range(nb) for p in range(heads // 2)]

    row = lax.broadcasted_iota(jnp.int32, (c, 2 * c), 0)
    col = lax.broadcasted_iota(jnp.int32, (c, 2 * c), 1) & (c - 1)
    causal = row >= col
    strict = row > col
    eye = (row == col).astype(F32)
    row1 = lax.broadcasted_iota(jnp.int32, (c, c), 0)
    col1 = lax.broadcasted_iota(jnp.int32, (c, c), 1)
    tri = (row1 >= col1).astype(BF16)
    nh2 = 2 * heads
    eye_h = (lax.broadcasted_iota(jnp.int32, (nh2, nh2), 0)
             == lax.broadcasted_iota(jnp.int32, (nh2, nh2), 1)).astype(BF16)

    def same_block(shift):
        return (row >> shift) == (col >> shift)

    gb, gc, gc_t = [], [], []
    for b in range(nb):
        g = g_ref[:, b * LANES:b * LANES + nh2]
        cs = sum(jnp.dot(tri, part, preferred_element_type=F32) for part in _split3(g))
        gb.append(g)
        gc.append(cs)
        gc_t.append(sum(lax.dot_general(eye_h, part, (((1,), (1,)), ((), ())), preferred_element_type=F32)
                        for part in _split3(cs)))

    def lanes2(ref, b, p, width, per_batch):
        return ref[:, b * per_batch + 2 * p * width:b * per_batch + 2 * (p + 1) * width]

    def pair_cols(b, p, fn, width):
        return jnp.concatenate([jnp.broadcast_to(fn(b, 2 * p + i), (c, width)) for i in range(2)], axis=1)

    beta_c = lambda b, h: gb[b][:, h:h + 1]
    gcol = lambda b, h: gc[b][:, heads + h:heads + h + 1]
    grow = lambda b, h: gc_t[b][heads + h:heads + h + 1, :]

    a_mat, qk_mat = {}, {}
    for (b, p) in units:
        qp = lanes2(q_ref, b, p, dk, qk_w)
        kp = lanes2(k_ref, b, p, dk, qk_w)
        res = lax.dot_general(jnp.concatenate([qp, kp], axis=0), _pair_block_diag(kp),
                              (((1,), (1,)), ((), ())), preferred_element_type=F32)
        decay = jnp.where(causal, jnp.exp(jnp.concatenate(
            [gcol(b, 2 * p + i) - grow(b, 2 * p + i) for i in range(2)], axis=1)), 0.0)
        qk_mat[b, p] = res[:c] * decay
        a_mat[b, p] = jnp.where(strict, pair_cols(b, p, beta_c, c) * res[c:] * decay, 0.0)

    shift0 = GDN_BASE_BLOCK.bit_length() - 1
    dblk = {u: jnp.where(same_block(shift0), a_mat[u], 0.0) for u in units}
    d2 = {u: _pair_mm(dblk[u], dblk[u]) for u in units}
    d4 = {u: _pair_mm(d2[u], d2[u]) for u in units}
    t_inv = {u: eye - dblk[u] for u in units}
    for pw in (d2, d4):
        upd = {u: _pair_mm(t_inv[u], pw[u]) for u in units}
        t_inv = {u: t_inv[u] + upd[u] for u in units}
    shift = shift0
    while (1 << shift) < c:
        level = same_block(shift + 1) & jnp.logical_not(same_block(shift))
        xt = {u: _pair_mm(jnp.where(level, a_mat[u], 0.0), t_inv[u]) for u in units}
        upd = {u: _pair_mm(t_inv[u], xt[u]) for u in units}
        t_inv = {u: t_inv[u] - upd[u] for u in units}
        shift += 1

    w_pk, u_pk, gam_pk = {}, {}, {}
    for (b, p) in units:
        gam_pk[b, p] = jnp.exp(pair_cols(b, p, gcol, dk))
        kp = lanes2(k_ref, b, p, dk, qk_w).astype(F32)
        vp = lanes2(v_ref, b, p, dv, e_w).astype(F32)
        rw = (pair_cols(b, p, beta_c, dk) * gam_pk[b, p] * kp).astype(BF16)
        ru = (pair_cols(b, p, beta_c, dv) * vp).astype(BF16)
        zk = jnp.zeros((c, dk), BF16)
        zv = jnp.zeros((c, dv), BF16)
        rhs = jnp.concatenate([
            jnp.concatenate([rw[:, :dk], zk, ru[:, :dv], zv], axis=1),
            jnp.concatenate([zk, rw[:, dk:], zv, ru[:, dv:]], axis=1)], axis=0)
        wu = jnp.dot(t_inv[b, p].astype(BF16), rhs, preferred_element_type=F32)
        w_pk[b, p] = wu[:, :2 * dk]
        u_pk[b, p] = wu[:, 2 * dk:]

    hunits = [(b, h) for b in range(nb) for h in range(heads)]
    ws, states = {}, {}
    for (b, h) in hunits:
        p, i = divmod(h, 2)
        qd = lanes2(q_ref, b, p, dk, qk_w).astype(F32) * gam_pk[b, p]
        lhs = jnp.concatenate([w_pk[b, p][:, i * dk:(i + 1) * dk], qd[:, i * dk:(i + 1) * dk]], axis=0)
        states[b, h] = s_ref[b * heads + h]
        ws[b, h] = _bdot(lhs, states[b, h])
    v_new = {(b, h): u_pk[b, h // 2][:, (h % 2) * dv:(h % 2 + 1) * dv] - ws[b, h][:c] for (b, h) in hunits}
    o_intra = {(b, h): _bdot(qk_mat[b, h // 2][:, (h % 2) * c:(h % 2 + 1) * c], v_new[b, h]) for (b, h) in hunits}
    for (b, h) in hunits:
        g_last = gcol(b, h)[c - 1:c, :]
        kh = k_ref[:, b * qk_w + h * dk:b * qk_w + (h + 1) * dk].astype(F32)
        k_dec = kh * jnp.exp(g_last - gcol(b, h))
        s_ref[b * heads + h] = jnp.exp(g_last) * states[b, h] + _bdot_tn(k_dec, v_new[b, h])
    for (b, h) in hunits:
        o = ws[b, h][c:] + o_intra[b, h]
        on = _rms(o) * nw_ref[...]
        zh = z_ref[:, b * e_w + h * dv:b * e_w + (h + 1) * dv].astype(F32)
        y_ref[:, b * e_w + h * dv:b * e_w + (h + 1) * dv] = (on * _silu(zh)).astype(BF16)


def _gdn_core(q, k, v, z, gates, norm_w, heads):
    seqlen = q.shape[0]
    nb = GDN_BATCH_PER_STEP
    bsz = gates.shape[1] // LANES
    qk = q.shape[1] // bsz
    e = v.shape[1] // bsz
    c = min(GDN_CHUNK, seqlen)
    col_spec = lambda w: pl.BlockSpec((c, nb * w), lambda b, n: (n, b))
    return pl.pallas_call(
        functools.partial(_gdn_core_kernel, heads=heads, nb=nb),
        out_shape=jax.ShapeDtypeStruct((seqlen, bsz * e), BF16),
        grid=(bsz // nb, seqlen // c),
        in_specs=[col_spec(qk), col_spec(qk), col_spec(e), col_spec(e), col_spec(LANES),
                  _resident((1, e // heads), lambda b, n: (0, 0))],
        out_specs=col_spec(e),
        scratch_shapes=[pltpu.VMEM((nb * heads, qk // heads, e // heads), F32)],
        compiler_params=_params("parallel", "arbitrary"),
        name="gdn_core",
    )(q, k, v, z, gates, norm_w.reshape(1, e // heads))


def _gdn_out_kernel(y_ref, x_ref, gate_ref, w_ref, fw_ref, o_ref, slab_ref):
    bsz, tmt, d = o_ref.shape
    e = y_ref.shape[1] // bsz
    for j in range(d // LANES):
        slab_ref[j] = x_ref[:, j * LANES:(j + 1) * LANES]
    y = jnp.concatenate([y_ref[:, b * e:(b + 1) * e] for b in range(bsz)], axis=0)
    out = jnp.dot(y, w_ref[...], preferred_element_type=F32).reshape(bsz, tmt, d)
    for b in range(bsz):
        xn = _sequence_rows(slab_ref, b, tmt, 0, d) + gate_ref[b:b + 1, :] * out[b]
        o_ref[b] = _rms(xn) * fw_ref[...]


def _gdn_out(y, x_tm, mod, layer, w_out, final_w):
    e, d = w_out.shape
    seqlen = y.shape[0]
    bsz = y.shape[1] // e
    tmt = min(ROW_TILE // bsz, seqlen)
    return pl.pallas_call(
        _gdn_out_kernel,
        out_shape=jax.ShapeDtypeStruct((bsz, seqlen, d), F32),
        grid=(seqlen // tmt,),
        in_specs=[
            pl.BlockSpec((tmt, bsz * e), lambda i: (i, 0)),
            pl.BlockSpec((tmt * bsz, d), lambda i: (i, 0)),
            _resident((bsz, d), lambda i: (layer, MOD_GATE)),
            _resident((e, d), lambda i: (0, 0)),
            _resident((1, d), lambda i: (0, 0)),
        ],
        out_specs=pl.BlockSpec((bsz, tmt, d), lambda i: (0, i, 0)),
        scratch_shapes=[pltpu.VMEM((d // LANES, tmt * bsz, LANES), F32)],
        compiler_params=_params("parallel"),
        name="gdn_out",
    )(y, x_tm, mod, w_out.astype(BF16), final_w.reshape(1, d))


def kernel(x, c, ada_w, ada_b, norm_w, s5_w_in, s5_lambda_re, s5_lambda_im, s5_log_dt, s5_b_re, s5_b_im,
           s5_c_re, s5_c_im, s5_d, s5_w_glu, s5_w_out, gdn_w_in, gdn_conv_w, gdn_a_log, gdn_dt_bias,
           gdn_norm_w, gdn_w_out, final_norm_w):
    bsz, seqlen, d = x.shape
    assert bsz == SUBLANES, "the time-major layout keeps one batch row per vector sublane"
    assert ada_w.shape[0] == 2 and s5_w_in.shape[0] == 1 and gdn_w_in.shape[0] == 1

    mod = _modulation(c, ada_w, ada_b)

    a_re, a_im, bb_re, bb_im = _s5_discretise(s5_lambda_re[0], s5_lambda_im[0], s5_log_dt[0], s5_b_re[0], s5_b_im[0])
    w_b, w_c, a_tile = _s5_block_weights(a_re, a_im, bb_re, bb_im, s5_c_re[0], s5_c_im[0])
    x_tm, u, z = _s5_in(x, norm_w[0], mod, 0, s5_w_in[0])
    steps = min(S5_STEPS, seqlen)
    chunks = min(S5_CHUNKS, seqlen // steps)
    y = _s5_scan(u, w_b, w_c, a_tile, s5_d[0], steps, chunks)
    x1_tm = _s5_out(y, z, x_tm, mod, 0, s5_w_glu[0], s5_w_out[0])

    heads = gdn_a_log.shape[1]
    q, k, v, zg, gates = _gdn_in(x1_tm, norm_w[1], mod, 1, gdn_w_in[0], gdn_conv_w[0],
                                 gdn_a_log[0], gdn_dt_bias[0])
    yg = _gdn_core(q, k, v, zg, gates, gdn_norm_w[0], heads)
    return _gdn_out(yg, x1_tm, mod, 1, gdn_w_out[0], final_norm_w)
```

```python
import functools
import math

import jax
import jax.numpy as jnp
from jax import lax
from jax.experimental import pallas as pl
from jax.experimental.pallas import tpu as pltpu

F32 = jnp.float32
BF16 = jnp.bfloat16

NORM_EPS = 1e-6
S5_GROUP = 16
S5_STATE = 64
S5_TILE_GROUPS = 16
S5_BLOCK_GROUPS = 8
S5_TILE_BLOCKS = S5_TILE_GROUPS // S5_BLOCK_GROUPS
S5_PAIRS = 64
S5_CHUNKS = 4
GDN_HEADS = 8
GDN_CONV = 4
GDN_CHUNK = 128
GDN_BASE_BLOCK = 8
GDN_BATCH_PER_STEP = 4
ROW_TILE = 512
GDN_IN_ROWS = 256
SUBLANES = 8
LANES = 128
VMEM_LIMIT_BYTES = 56 * 1024 * 1024


def _sigmoid(x):
    return 1.0 / (1.0 + jnp.exp(-x))


def _silu(x):
    return x * _sigmoid(x)


def _gelu_tanh(x):
    c = math.sqrt(2.0 / math.pi)
    return 0.5 * x * (1.0 + jnp.tanh(c * (x + 0.044715 * (x * x * x))))


def _softplus(x):
    return jnp.maximum(x, 0.0) + jnp.log(1.0 + jnp.exp(-jnp.abs(x)))


def _rms(x):
    return x * lax.rsqrt(jnp.mean(x * x, axis=-1, keepdims=True) + NORM_EPS)


def _bdot(a, b):
    return jnp.dot(a.astype(BF16), b.astype(BF16), preferred_element_type=F32)


def _bdot_nt(a, b):
    return lax.dot_general(a.astype(BF16), b.astype(BF16), (((1,), (1,)), ((), ())),
                           preferred_element_type=F32)


def _bdot_tn(a, b):
    return lax.dot_general(a.astype(BF16), b.astype(BF16), (((0,), (0,)), ((), ())),
                           preferred_element_type=F32)


def _split3(x):
    hi = x.astype(BF16)
    r = x - hi.astype(F32)
    mid = r.astype(BF16)
    lo = (r - mid.astype(F32)).astype(BF16)
    return hi, mid, lo


def _params(*sem):
    return pltpu.CompilerParams(dimension_semantics=sem, vmem_limit_bytes=VMEM_LIMIT_BYTES)


def _resident(shape, index_map):
    return pl.BlockSpec(shape, index_map, pipeline_mode=pl.Buffered(1))


MOD_SHIFT, MOD_SCALE, MOD_GATE = 0, 1, 2


def _mod_kernel(c_ref, w_ref, b_ref, o_ref):
    o_ref[...] = _bdot(_silu(c_ref[...]), w_ref[0]) + b_ref[0]


def _modulation(c, ada_w, ada_b):
    depth, d, d3 = ada_w.shape
    bsz = c.shape[0]
    nb = d3 // d
    return pl.pallas_call(
        _mod_kernel,
        out_shape=jax.ShapeDtypeStruct((depth * bsz, d3), F32),
        grid=(depth, nb),
        in_specs=[
            pl.BlockSpec((bsz, d), lambda l, j: (0, 0)),
            pl.BlockSpec((1, d, d), lambda l, j: (l, 0, j)),
            pl.BlockSpec((1, 1, d), lambda l, j: (l, 0, j)),
        ],
        out_specs=pl.BlockSpec((bsz, d), lambda l, j: (l, j)),
        compiler_params=_params("parallel", "parallel"),
        name="adaln_mod",
    )(c, ada_w.astype(BF16), ada_b.reshape(depth, 1, d3))


def _cmul(xr, xi, yr, yi):
    return xr * yr - xi * yi, xr * yi + xi * yr


def _s5_disc_kernel(lre_ref, lim_ref, ldt_ref, are_ref, aim_ref, a2re_ref, a2im_ref, qre_ref, qim_ref):
    lre = lre_ref[...]
    lim = lim_ref[...]
    dt = jnp.exp(ldt_ref[...])
    mag = jnp.exp(lre * dt)
    are = mag * jnp.cos(lim * dt)
    aim = mag * jnp.sin(lim * dt)
    den = lre * lre + lim * lim
    nr = are - 1.0
    ni = aim
    are_ref[...] = are
    aim_ref[...] = aim
    a2re_ref[...], a2im_ref[...] = _cmul(are, aim, are, aim)
    qre_ref[...] = (nr * lre + ni * lim) / den
    qim_ref[...] = (ni * lre - nr * lim) / den


def _s5_bbar_kernel(qre_ref, qim_ref, are_ref, aim_ref, bre_ref, bim_ref, ore_ref, oim_ref, pre_ref, pim_ref):
    bbr, bbi = _cmul(qre_ref[...], qim_ref[...], bre_ref[...], bim_ref[...])
    ore_ref[...] = bbr
    oim_ref[...] = bbi
    pre_ref[...], pim_ref[...] = _cmul(are_ref[...], aim_ref[...], bbr, bbi)


def _s5_cpow_kernel(cre_ref, cim_ref, are_ref, aim_ref, c1re_ref, c1im_ref, c2re_ref, c2im_ref):
    ar, ai = are_ref[...], aim_ref[...]
    c1r, c1i = _cmul(cre_ref[...], cim_ref[...], ar, ai)
    c1re_ref[...] = c1r
    c1im_ref[...] = c1i
    c2re_ref[...], c2im_ref[...] = _cmul(c1r, c1i, ar, ai)


def _s5_kmat_kernel(wb_ref, wc_ref, wi_ref):
    n = wc_ref.shape[2]
    k0 = jnp.dot(wb_ref[0, n:, :], wc_ref[0], preferred_element_type=F32)
    k1 = jnp.dot(wb_ref[0, :n, :], wc_ref[0], preferred_element_type=F32)
    top = jnp.concatenate([k0, k1], axis=1)
    bot = jnp.concatenate([jnp.zeros_like(k0), k0], axis=1)
    wi_ref[0] = jnp.concatenate([top, bot], axis=0).astype(BF16)


def _s5_weights(lam_re, lam_im, log_dt, b_re, b_im, c_re, c_im):
    g, p = lam_re.shape
    m = b_re.shape[-1]
    a_re, a_im, a2_re, a2_im, q_re, q_im = pl.pallas_call(
        _s5_disc_kernel,
        out_shape=[jax.ShapeDtypeStruct((g, p), F32)] * 6,
        name="s5_disc",
    )(lam_re, lam_im, log_dt.reshape(g, 1))
    rows = g * p
    blk = 1024
    col = lambda a: a.reshape(rows, 1)
    bb_re, bb_im, ab_re, ab_im = pl.pallas_call(
        _s5_bbar_kernel,
        out_shape=[jax.ShapeDtypeStruct((rows, m), F32)] * 4,
        grid=(rows // blk,),
        in_specs=[pl.BlockSpec((blk, 1), lambda i: (i, 0))] * 4 + [pl.BlockSpec((blk, m), lambda i: (i, 0))] * 2,
        out_specs=[pl.BlockSpec((blk, m), lambda i: (i, 0))] * 4,
        compiler_params=_params("parallel"),
        name="s5_bbar",
    )(col(q_re), col(q_im), col(a_re), col(a_im), b_re.reshape(rows, m), b_im.reshape(rows, m))
    rep = lambda a: jnp.repeat(a, m, axis=0)
    c1_re, c1_im, c2_re, c2_im = pl.pallas_call(
        _s5_cpow_kernel,
        out_shape=[jax.ShapeDtypeStruct((g * m, p), F32)] * 4,
        name="s5_cpow",
    )(c_re.reshape(g * m, p), c_im.reshape(g * m, p), rep(a_re), rep(a_im))

    tg = S5_BLOCK_GROUPS
    nb = g // tg
    eye = jnp.eye(tg, dtype=F32)

    def in_blocks(bb):
        t = bb.reshape(nb, tg, p, m).transpose(0, 1, 3, 2)
        return (t[:, :, :, None, :] * eye[None, :, None, :, None]).reshape(nb, tg * m, tg * p)

    def out_blocks(cc):
        t = cc.reshape(nb, tg, m, p).transpose(0, 1, 3, 2)
        return (t[:, :, :, None, :] * eye[None, :, None, :, None]).reshape(nb, tg * p, tg * m)

    w_b = jnp.concatenate([
        jnp.concatenate([in_blocks(ab_re), in_blocks(ab_im)], axis=2),
        jnp.concatenate([in_blocks(bb_re), in_blocks(bb_im)], axis=2)], axis=1).astype(BF16)
    w_c = jnp.concatenate([
        jnp.concatenate([out_blocks(c1_re), out_blocks(c2_re)], axis=2),
        jnp.concatenate([-out_blocks(c1_im), -out_blocks(c2_im)], axis=2)], axis=1).astype(BF16)
    w_c0 = jnp.concatenate([out_blocks(c_re.reshape(g * m, p)),
                            -out_blocks(c_im.reshape(g * m, p))], axis=1).astype(BF16)
    w_i = pl.pallas_call(
        _s5_kmat_kernel,
        out_shape=jax.ShapeDtypeStruct((nb, 2 * tg * m, 2 * tg * m), BF16),
        grid=(nb,),
        in_specs=[pl.BlockSpec((1, 2 * tg * m, 2 * tg * p), lambda i: (i, 0, 0)),
                  pl.BlockSpec((1, 2 * tg * p, tg * m), lambda i: (i, 0, 0))],
        out_specs=pl.BlockSpec((1, 2 * tg * m, 2 * tg * m), lambda i: (i, 0, 0)),
        compiler_params=_params("parallel"),
        name="s5_kmat",
    )(w_b, w_c0)
    a2 = jnp.concatenate([a2_re.reshape(nb, 1, tg * p), a2_im.reshape(nb, 1, tg * p)], axis=2)
    a2 = jnp.broadcast_to(a2, (nb, SUBLANES, 2 * tg * p))
    per_tile = lambda w: w.reshape((nb // S5_TILE_BLOCKS, S5_TILE_BLOCKS) + w.shape[1:])
    return per_tile(w_b), per_tile(w_i), per_tile(w_c), per_tile(a2)


def _modulated_norm_rows(x, nw, scale, shift):
    r, d = x.shape
    y = (_rms(x) * nw).reshape(r // SUBLANES, SUBLANES, d)
    return (y * (1.0 + scale)[None] + shift[None]).reshape(r, d)


def _sequence_rows(slab_ref, b, tmt, lo, hi):
    return jnp.concatenate([slab_ref[j, pl.ds(b, tmt, stride=SUBLANES), :]
                            for j in range(lo // LANES, hi // LANES)], axis=1)


def _s5_in_kernel(x_ref, nw_ref, sc_ref, sh_ref, w_ref, xt_ref, u_ref, z_ref, slab_ref):
    bsz, tmt, d = x_ref.shape
    for j in range(d // LANES):
        for b in range(bsz):
            slab_ref[j, pl.ds(b, tmt, stride=bsz), :] = x_ref[b, :, j * LANES:(j + 1) * LANES]
    x_tm = jnp.concatenate([slab_ref[j] for j in range(d // LANES)], axis=1)
    xt_ref[...] = x_tm
    h = _modulated_norm_rows(x_tm, nw_ref[...], sc_ref[...], sh_ref[...]).astype(BF16)
    e = z_ref.shape[1]
    nt, _, cw = u_ref.shape
    u = jnp.dot(h, w_ref[:, :e], preferred_element_type=F32).astype(BF16)
    for k in range(nt):
        u_ref[k] = u[:, k * cw:(k + 1) * cw]
    z_ref[...] = jnp.dot(h, w_ref[:, e:], preferred_element_type=F32).astype(BF16)


def _s5_in(x, norm_w, mod, layer, w_in):
    bsz, seqlen, d = x.shape
    e = w_in.shape[1] // 2
    tmt = min(ROW_TILE // bsz, seqlen)
    rows = tmt * bsz
    t = seqlen * bsz
    cw = S5_TILE_GROUPS * S5_GROUP
    nt = e // cw
    return pl.pallas_call(
        _s5_in_kernel,
        out_shape=[jax.ShapeDtypeStruct((t, d), F32), jax.ShapeDtypeStruct((nt, t, cw), BF16),
                   jax.ShapeDtypeStruct((t, e), BF16)],
        grid=(seqlen // tmt,),
        in_specs=[
            pl.BlockSpec((bsz, tmt, d), lambda i: (0, i, 0)),
            _resident((1, d), lambda i: (0, 0)),
            _resident((bsz, d), lambda i: (layer, MOD_SCALE)),
            _resident((bsz, d), lambda i: (layer, MOD_SHIFT)),
            _resident((d, 2 * e), lambda i: (0, 0)),
        ],
        out_specs=[pl.BlockSpec((rows, d), lambda i: (i, 0)), pl.BlockSpec((nt, rows, cw), lambda i: (0, i, 0)),
                   pl.BlockSpec((rows, e), lambda i: (i, 0))],
        scratch_shapes=[pltpu.VMEM((d // LANES, rows, LANES), F32)],
        compiler_params=_params("parallel"),
        name="s5_in",
    )(x, norm_w.reshape(1, d), mod, mod, w_in.astype(BF16))


def _s5_scan_kernel(u_ref, wb_ref, wi_ref, wc_ref, a_ref, d_ref, y_ref, sin_ref, xs_ref, st_ref, *, pairs, chunks):
    @pl.when(pl.program_id(1) == 0)
    def _():
        st_ref[...] = jnp.zeros_like(st_ref)

    nblk = wb_ref.shape[1]
    bw = wi_ref.shape[2] // 2
    half = wb_ref.shape[3] // 2
    nl = half // LANES
    rows_c = pairs * SUBLANES
    a_re = [[a_ref[0, k, :, j * LANES:(j + 1) * LANES] for j in range(nl)] for k in range(nblk)]
    a_im = [[a_ref[0, k, :, half + j * LANES:half + (j + 1) * LANES] for j in range(nl)] for k in range(nblk)]
    x_re = [[st_ref[k, 0, :, j * LANES:(j + 1) * LANES] for j in range(nl)] for k in range(nblk)]
    x_im = [[st_ref[k, 1, :, j * LANES:(j + 1) * LANES] for j in range(nl)] for k in range(nblk)]
    intra = {}

    def project_in(ci):
        u = u_ref[0, ci * 2 * rows_c:(ci + 1) * 2 * rows_c, :].astype(F32).reshape(pairs, 2, SUBLANES, nblk * bw)
        u0 = u[:, 0].reshape(rows_c, nblk * bw)
        u1 = u[:, 1].reshape(rows_c, nblk * bw)
        slot = ci % 2
        for k in range(nblk):
            uu = jnp.concatenate([u0[:, k * bw:(k + 1) * bw], u1[:, k * bw:(k + 1) * bw]], axis=1).astype(BF16)
            sin_ref[slot, k] = jnp.dot(uu, wb_ref[0, k], preferred_element_type=F32)
            intra[ci, k] = jnp.dot(uu, wi_ref[0, k], preferred_element_type=F32)

    def recur(ci):
        slot = ci % 2
        for pp in range(pairs // 2):
            r0 = 2 * pp * SUBLANES
            for k in range(nblk):
                for j in range(nl):
                    lo, hi = j * LANES, (j + 1) * LANES
                    before_re, before_im = [], []
                    for s in range(2):
                        rr = r0 + s * SUBLANES
                        before_re.append(x_re[k][j])
                        before_im.append(x_im[k][j])
                        b_re = sin_ref[slot, k, rr:rr + SUBLANES, lo:hi]
                        b_im = sin_ref[slot, k, rr:rr + SUBLANES, half + lo:half + hi]
                        v_re = a_re[k][j] * x_re[k][j] - a_im[k][j] * x_im[k][j] + b_re
                        v_im = a_re[k][j] * x_im[k][j] + a_im[k][j] * x_re[k][j] + b_im
                        x_re[k][j], x_im[k][j] = v_re, v_im
                    xs_ref[slot, k, r0:r0 + 2 * SUBLANES, lo:hi] = jnp.concatenate(before_re, axis=0).astype(BF16)
                    xs_ref[slot, k, r0:r0 + 2 * SUBLANES, half + lo:half + hi] = (
                        jnp.concatenate(before_im, axis=0).astype(BF16))

    def project_out(ci):
        slot = ci % 2
        cols = []
        for k in range(nblk):
            yk = intra.pop((ci, k)) + jnp.dot(xs_ref[slot, k], wc_ref[0, k], preferred_element_type=F32)
            steps = [yk[:, s * bw:(s + 1) * bw].reshape(pairs, 1, SUBLANES, bw) for s in range(2)]
            cols.append(jnp.concatenate(steps, axis=1).reshape(2 * rows_c, bw))
        y = jnp.concatenate(cols, axis=1)
        u = u_ref[0, ci * 2 * rows_c:(ci + 1) * 2 * rows_c, :].astype(F32)
        y_ref[0, ci * 2 * rows_c:(ci + 1) * 2 * rows_c, :] = _gelu_tanh(y + d_ref[0] * u).astype(BF16)

    project_in(0)
    for ci in range(chunks):
        if ci + 1 < chunks:
            project_in(ci + 1)
        recur(ci)
        project_out(ci)

    for k in range(nblk):
        for j in range(nl):
            st_ref[k, 0, :, j * LANES:(j + 1) * LANES] = x_re[k][j]
            st_ref[k, 1, :, j * LANES:(j + 1) * LANES] = x_im[k][j]


def _s5_scan(u_tiles, w_b, w_i, w_c, a2, d_skip, pairs, chunks):
    nt, nblk, kw, sw = w_b.shape
    cw = u_tiles.shape[2]
    t = u_tiles.shape[1]
    rows_c = pairs * SUBLANES
    rows = 2 * rows_c * chunks
    return pl.pallas_call(
        functools.partial(_s5_scan_kernel, pairs=pairs, chunks=chunks),
        out_shape=jax.ShapeDtypeStruct((nt, t, cw), BF16),
        grid=(nt, t // rows),
        in_specs=[
            pl.BlockSpec((1, rows, cw), lambda k, c: (k, c, 0)),
            _resident((1, nblk, kw, sw), lambda k, c: (k, 0, 0, 0)),
            _resident((1, nblk, kw, kw), lambda k, c: (k, 0, 0, 0)),
            _resident((1, nblk, sw, kw), lambda k, c: (k, 0, 0, 0)),
            _resident((1, nblk, SUBLANES, sw), lambda k, c: (k, 0, 0, 0)),
            _resident((1, 1, cw), lambda k, c: (k, 0, 0)),
        ],
        out_specs=pl.BlockSpec((1, rows, cw), lambda k, c: (k, c, 0)),
        scratch_shapes=[
            pltpu.VMEM((2, nblk, rows_c, sw), F32),
            pltpu.VMEM((2, nblk, rows_c, sw), BF16),
            pltpu.VMEM((nblk, 2, SUBLANES, sw // 2), F32),
        ],
        compiler_params=_params("parallel", "arbitrary"),
        name="s5_scan",
    )(u_tiles, w_b, w_i, w_c, a2, d_skip.reshape(nt, 1, cw))


def _s5_out_kernel(y_ref, z_ref, x_ref, gate_ref, wg_ref, wo_ref, o_ref):
    y = jnp.concatenate([y_ref[k] for k in range(y_ref.shape[0])], axis=1)
    yf = y.astype(F32)
    y2 = yf * _sigmoid(jnp.dot(y, wg_ref[...], preferred_element_type=F32)) * _silu(z_ref[...].astype(F32))
    out = jnp.dot(y2.astype(BF16), wo_ref[...], preferred_element_type=F32)
    r, d = out.shape
    x = x_ref[...].reshape(r // SUBLANES, SUBLANES, d)
    o_ref[...] = (x + gate_ref[...][None] * out.reshape(r // SUBLANES, SUBLANES, d)).reshape(r, d)


def _s5_out(y, z, x_tm, mod, layer, w_glu, w_out):
    t, d = x_tm.shape
    e = w_glu.shape[0]
    nt, _, cw = y.shape
    rows = min(ROW_TILE, t)
    row_spec = lambda c: pl.BlockSpec((rows, c), lambda i: (i, 0))
    return pl.pallas_call(
        _s5_out_kernel,
        out_shape=jax.ShapeDtypeStruct((t, d), F32),
        grid=(t // rows,),
        in_specs=[
            pl.BlockSpec((nt, rows, cw), lambda i: (0, i, 0)), row_spec(e), row_spec(d),
            _resident((SUBLANES, d), lambda i: (layer, MOD_GATE)),
            _resident((e, e), lambda i: (0, 0)),
            _resident((e, d), lambda i: (0, 0)),
        ],
        out_specs=row_spec(d),
        compiler_params=_params("parallel"),
        name="s5_out",
    )(y, z, x_tm, mod, w_glu.astype(BF16), w_out.astype(BF16))


def _gdn_in_kernel(x_ref, nw_ref, sc_ref, sh_ref, w_ref, wg_ref, cw_ref, gp_ref,
                   q_ref, k_ref, v_ref, z_ref, g_ref, ext_ref, st_ref, *, heads):
    halo = (GDN_CONV - 1) * SUBLANES
    r, d = x_ref.shape
    cc = ext_ref.shape[1]
    bsz = SUBLANES
    tmt = r // bsz
    qk = q_ref.shape[1] // bsz
    e = v_ref.shape[1] // bsz
    dk = qk // heads

    @pl.when(pl.program_id(0) == 0)
    def _():
        ext_ref[0:halo, :] = jnp.zeros((halo, cc), F32)

    h = _modulated_norm_rows(x_ref[...], nw_ref[...], sc_ref[...], sh_ref[...]).astype(BF16)

    ext_ref[halo:halo + r, :] = jnp.dot(h, w_ref[:, :cc], preferred_element_type=F32)
    conv = cw_ref[0:1, :] * ext_ref[0:r, :]
    for j in range(1, GDN_CONV):
        conv = conv + cw_ref[j:j + 1, :] * ext_ref[j * SUBLANES:j * SUBLANES + r, :]
    ext_ref[0:halo, :] = ext_ref[r:r + halo, :]
    act = _silu(conv)

    def stage(lo, val):
        for j in range(val.shape[1] // LANES):
            st_ref[lo // LANES + j] = val[:, j * LANES:(j + 1) * LANES]

    for hh in range(heads):
        qh = act[:, hh * dk:(hh + 1) * dk]
        kh = act[:, qk + hh * dk:qk + (hh + 1) * dk]
        stage(hh * dk, qh * lax.rsqrt(jnp.sum(qh * qh, axis=-1, keepdims=True) + NORM_EPS) * (dk ** -0.5))
        stage(qk + hh * dk, kh * lax.rsqrt(jnp.sum(kh * kh, axis=-1, keepdims=True) + NORM_EPS))
    stage(2 * qk, act[:, 2 * qk:])
    stage(cc, jnp.dot(h, w_ref[:, cc:], preferred_element_type=F32))

    logit = jnp.dot(h, wg_ref[...], preferred_element_type=F32)
    beta = _sigmoid(logit)
    g = -jnp.exp(gp_ref[0:1, :]) * _softplus(logit + gp_ref[1:2, :])
    lane = lax.broadcasted_iota(jnp.int32, logit.shape, 1)
    stage(cc + e, jnp.where(lane < heads, beta, g))

    for b in range(bsz):
        seq = lambda lo, hi: _sequence_rows(st_ref, b, tmt, lo, hi)
        q_ref[:, b * qk:(b + 1) * qk] = seq(0, qk).astype(BF16)
        k_ref[:, b * qk:(b + 1) * qk] = seq(qk, 2 * qk).astype(BF16)
        v_ref[:, b * e:(b + 1) * e] = seq(2 * qk, cc).astype(BF16)
        z_ref[:, b * e:(b + 1) * e] = seq(cc, cc + e).astype(BF16)
        g_ref[:, b * LANES:(b + 1) * LANES] = seq(cc + e, cc + e + LANES)


def _gdn_in(x_tm, norm_w, mod, layer, w_in, conv_w, a_log, dt_bias):
    t, d = x_tm.shape
    bsz = SUBLANES
    seqlen = t // bsz
    heads = a_log.shape[0]
    cc = conv_w.shape[1]
    e = (w_in.shape[1] - cc - 2 * heads)
    qk = (cc - e) // 2
    rows = min(GDN_IN_ROWS, t)
    tmt = rows // bsz
    w_main = w_in[:, :cc + e].astype(BF16)
    w_gate = jnp.zeros((d, LANES), F32).at[:, :2 * heads].set(w_in[:, cc + e:]).astype(BF16)
    gate_par = jnp.zeros((2, LANES), F32)
    gate_par = gate_par.at[0, heads:2 * heads].set(a_log).at[1, heads:2 * heads].set(dt_bias)
    seq_spec = lambda c: pl.BlockSpec((tmt, bsz * c), lambda i: (i, 0))
    return pl.pallas_call(
        functools.partial(_gdn_in_kernel, heads=heads),
        out_shape=[
            jax.ShapeDtypeStruct((seqlen, bsz * qk), BF16),
            jax.ShapeDtypeStruct((seqlen, bsz * qk), BF16),
            jax.ShapeDtypeStruct((seqlen, bsz * e), BF16),
            jax.ShapeDtypeStruct((seqlen, bsz * e), BF16),
            jax.ShapeDtypeStruct((seqlen, bsz * LANES), F32),
        ],
        grid=(t // rows,),
        in_specs=[
            pl.BlockSpec((rows, d), lambda i: (i, 0)),
            _resident((1, d), lambda i: (0, 0)),
            _resident((SUBLANES, d), lambda i: (layer, MOD_SCALE)),
            _resident((SUBLANES, d), lambda i: (layer, MOD_SHIFT)),
            _resident((d, cc + e), lambda i: (0, 0)),
            _resident((d, LANES), lambda i: (0, 0)),
            _resident((GDN_CONV, cc), lambda i: (0, 0)),
            _resident((2, LANES), lambda i: (0, 0)),
        ],
        out_specs=[seq_spec(qk), seq_spec(qk), seq_spec(e), seq_spec(e), seq_spec(LANES)],
        scratch_shapes=[
            pltpu.VMEM((rows + (GDN_CONV - 1) * SUBLANES, cc), F32),
            pltpu.VMEM(((cc + e + LANES) // LANES, rows, LANES), F32),
        ],
        compiler_params=_params("arbitrary"),
        name="gdn_in",
    )(x_tm, norm_w.reshape(1, d), mod, mod, w_main, w_gate, conv_w, gate_par)


def _pair_block_diag(x):
    c = x.shape[0]
    lane = lax.broadcasted_iota(jnp.int32, x.shape, 1)
    zero = jnp.zeros_like(x)
    return jnp.concatenate([jnp.where(lane < c, x, zero), jnp.where(lane >= c, x, zero)], axis=0)


def _pair_mm(x, y):
    return jnp.dot(x.astype(BF16), _pair_block_diag(y.astype(BF16)), preferred_element_type=F32)


def _gdn_core_kernel(q_ref, k_ref, v_ref, z_ref, g_ref, nw_ref, y_ref, s_ref, *, heads, nb):
    @pl.when(pl.program_id(1) == 0)
    def _():
        s_ref[...] = jnp.zeros_like(s_ref)

    c = q_ref.shape[0]
    qk_w = q_ref.shape[1] // nb
    e_w = v_ref.shape[1] // nb
    dk = qk_w // heads
    dv = e_w // heads
    units = [(b, p) for b in range(nb) for p in range(heads // 2)]

    row = lax.broadcasted_iota(jnp.int32, (c, 2 * c), 0)
    col = lax.broadcasted_iota(jnp.int32, (c, 2 * c), 1) & (c - 1)
    causal = row >= col
    strict = row > col
    eye = (row == col).astype(F32)
    row1 = lax.broadcasted_iota(jnp.int32, (c, c), 0)
    col1 = lax.broadcasted_iota(jnp.int32, (c, c), 1)
    tri = (row1 >= col1).astype(BF16)
    nh2 = 2 * heads
    eye_h = (lax.broadcasted_iota(jnp.int32, (nh2, nh2), 0)
             == lax.broadcasted_iota(jnp.int32, (nh2, nh2), 1)).astype(BF16)

    def same_block(shift):
        return (row >> shift) == (col >> shift)

    gb, gc, gc_t = [], [], []
    for b in range(nb):
        g = g_ref[:, b * LANES:b * LANES + nh2]
        cs = sum(jnp.dot(tri, part, preferred_element_type=F32) for part in _split3(g))
        gb.append(g)
        gc.append(cs)
        gc_t.append(sum(lax.dot_general(eye_h, part, (((1,), (1,)), ((), ())), preferred_element_type=F32)
                        for part in _split3(cs)))

    def lanes2(ref, b, p, width, per_batch):
        return ref[:, b * per_batch + 2 * p * width:b * per_batch + 2 * (p + 1) * width]

    def pair_cols(b, p, fn, width):
        return jnp.concatenate([jnp.broadcast_to(fn(b, 2 * p + i), (c, width)) for i in range(2)], axis=1)

    beta_c = lambda b, h: gb[b][:, h:h + 1]
    gcol = lambda b, h: gc[b][:, heads + h:heads + h + 1]
    grow = lambda b, h: gc_t[b][heads + h:heads + h + 1, :]

    a_mat, qk_mat = {}, {}
    for (b, p) in units:
        qp = lanes2(q_ref, b, p, dk, qk_w)
        kp = lanes2(k_ref, b, p, dk, qk_w)
        res = lax.dot_general(jnp.concatenate([qp, kp], axis=0), _pair_block_diag(kp),
                              (((1,), (1,)), ((), ())), preferred_element_type=F32)
        decay = jnp.where(causal, jnp.exp(jnp.concatenate(
            [gcol(b, 2 * p + i) - grow(b, 2 * p + i) for i in range(2)], axis=1)), 0.0)
        qk_mat[b, p] = res[:c] * decay
        a_mat[b, p] = jnp.where(strict, pair_cols(b, p, beta_c, c) * res[c:] * decay, 0.0)

    shift0 = GDN_BASE_BLOCK.bit_length() - 1
    dblk = {u: jnp.where(same_block(shift0), a_mat[u], 0.0) for u in units}
    d2 = {u: _pair_mm(dblk[u], dblk[u]) for u in units}
    d4 = {u: _pair_mm(d2[u], d2[u]) for u in units}
    t_inv = {u: eye - dblk[u] for u in units}
    for pw in (d2, d4):
        upd = {u: _pair_mm(t_inv[u], pw[u]) for u in units}
        t_inv = {u: t_inv[u] + upd[u] for u in units}
    shift = shift0
    while (1 << shift) < c:
        level = same_block(shift + 1) & jnp.logical_not(same_block(shift))
        xt = {u: _pair_mm(jnp.where(level, a_mat[u], 0.0), t_inv[u]) for u in units}
        upd = {u: _pair_mm(t_inv[u], xt[u]) for u in units}
        t_inv = {u: t_inv[u] - upd[u] for u in units}
        shift += 1

    w_pk, u_pk, gam_pk = {}, {}, {}
    for (b, p) in units:
        gam_pk[b, p] = jnp.exp(pair_cols(b, p, gcol, dk))
        kp = lanes2(k_ref, b, p, dk, qk_w).astype(F32)
        vp = lanes2(v_ref, b, p, dv, e_w).astype(F32)
        rw = (pair_cols(b, p, beta_c, dk) * gam_pk[b, p] * kp).astype(BF16)
        ru = (pair_cols(b, p, beta_c, dv) * vp).astype(BF16)
        zk = jnp.zeros((c, dk), BF16)
        zv = jnp.zeros((c, dv), BF16)
        rhs = jnp.concatenate([
            jnp.concatenate([rw[:, :dk], zk, ru[:, :dv], zv], axis=1),
            jnp.concatenate([zk, rw[:, dk:], zv, ru[:, dv:]], axis=1)], axis=0)
        wu = jnp.dot(t_inv[b, p].astype(BF16), rhs, preferred_element_type=F32)
        w_pk[b, p] = wu[:, :2 * dk]
        u_pk[b, p] = wu[:, 2 * dk:]

    hunits = [(b, h) for b in range(nb) for h in range(heads)]
    ws, states = {}, {}
    for (b, h) in hunits:
        p, i = divmod(h, 2)
        qd = lanes2(q_ref, b, p, dk, qk_w).astype(F32) * gam_pk[b, p]
        lhs = jnp.concatenate([w_pk[b, p][:, i * dk:(i + 1) * dk], qd[:, i * dk:(i + 1) * dk]], axis=0)
        states[b, h] = s_ref[b * heads + h]
        ws[b, h] = _bdot(lhs, states[b, h])
    v_new = {(b, h): u_pk[b, h // 2][:, (h % 2) * dv:(h % 2 + 1) * dv] - ws[b, h][:c] for (b, h) in hunits}
    o_intra = {(b, h): _bdot(qk_mat[b, h // 2][:, (h % 2) * c:(h % 2 + 1) * c], v_new[b, h]) for (b, h) in hunits}
    for (b, h) in hunits:
        g_last = gcol(b, h)[c - 1:c, :]
        kh = k_ref[:, b * qk_w + h * dk:b * qk_w + (h + 1) * dk].astype(F32)
        k_dec = kh * jnp.exp(g_last - gcol(b, h))
        s_ref[b * heads + h] = jnp.exp(g_last) * states[b, h] + _bdot_tn(k_dec, v_new[b, h])
    for (b, h) in hunits:
        o = ws[b, h][c:] + o_intra[b, h]
        on = _rms(o) * nw_ref[...]
        zh = z_ref[:, b * e_w + h * dv:b * e_w + (h + 1) * dv].astype(F32)
        y_ref[:, b * e_w + h * dv:b * e_w + (h + 1) * dv] = (on * _silu(zh)).astype(BF16)


def _gdn_core(q, k, v, z, gates, norm_w, heads):
    seqlen = q.shape[0]
    nb = GDN_BATCH_PER_STEP
    bsz = gates.shape[1] // LANES
    qk = q.shape[1] // bsz
    e = v.shape[1] // bsz
    c = min(GDN_CHUNK, seqlen)
    col_spec = lambda w: pl.BlockSpec((c, nb * w), lambda b, n: (n, b))
    return pl.pallas_call(
        functools.partial(_gdn_core_kernel, heads=heads, nb=nb),
        out_shape=jax.ShapeDtypeStruct((seqlen, bsz * e), BF16),
        grid=(bsz // nb, seqlen // c),
        in_specs=[col_spec(qk), col_spec(qk), col_spec(e), col_spec(e), col_spec(LANES),
                  _resident((1, e // heads), lambda b, n: (0, 0))],
        out_specs=col_spec(e),
        scratch_shapes=[pltpu.VMEM((nb * heads, qk // heads, e // heads), F32)],
        compiler_params=_params("parallel", "arbitrary"),
        name="gdn_core",
    )(q, k, v, z, gates, norm_w.reshape(1, e // heads))


def _gdn_out_kernel(y_ref, x_ref, gate_ref, w_ref, fw_ref, o_ref, slab_ref):
    bsz, tmt, d = o_ref.shape
    e = y_ref.shape[1] // bsz
    for j in range(d // LANES):
        slab_ref[j] = x_ref[:, j * LANES:(j + 1) * LANES]
    y = jnp.concatenate([y_ref[:, b * e:(b + 1) * e] for b in range(bsz)], axis=0)
    out = jnp.dot(y, w_ref[...], preferred_element_type=F32).reshape(bsz, tmt, d)
    for b in range(bsz):
        xn = _sequence_rows(slab_ref, b, tmt, 0, d) + gate_ref[b:b + 1, :] * out[b]
        o_ref[b] = _rms(xn) * fw_ref[...]


def _gdn_out(y, x_tm, mod, layer, w_out, final_w):
    e, d = w_out.shape
    seqlen = y.shape[0]
    bsz = y.shape[1] // e
    tmt = min(ROW_TILE // bsz, seqlen)
    return pl.pallas_call(
        _gdn_out_kernel,
        out_shape=jax.ShapeDtypeStruct((bsz, seqlen, d), F32),
        grid=(seqlen // tmt,),
        in_specs=[
            pl.BlockSpec((tmt, bsz * e), lambda i: (i, 0)),
            pl.BlockSpec((tmt * bsz, d), lambda i: (i, 0)),
            _resident((bsz, d), lambda i: (layer, MOD_GATE)),
            _resident((e, d), lambda i: (0, 0)),
            _resident((1, d), lambda i: (0, 0)),
        ],
        out_specs=pl.BlockSpec((bsz, tmt, d), lambda i: (0, i, 0)),
        scratch_shapes=[pltpu.VMEM((d // LANES, tmt * bsz, LANES), F32)],
        compiler_params=_params("parallel"),
        name="gdn_out",
    )(y, x_tm, mod, w_out.astype(BF16), final_w.reshape(1, d))


def kernel(x, c, ada_w, ada_b, norm_w, s5_w_in, s5_lambda_re, s5_lambda_im, s5_log_dt, s5_b_re, s5_b_im,
           s5_c_re, s5_c_im, s5_d, s5_w_glu, s5_w_out, gdn_w_in, gdn_conv_w, gdn_a_log, gdn_dt_bias,
           gdn_norm_w, gdn_w_out, final_norm_w):
    bsz, seqlen, d = x.shape
    assert bsz == SUBLANES, "the time-major layout keeps one batch row per vector sublane"
    assert ada_w.shape[0] == 2 and s5_w_in.shape[0] == 1 and gdn_w_in.shape[0] == 1

    mod = _modulation(c, ada_w, ada_b)

    w_b, w_i, w_c, a2 = _s5_weights(s5_lambda_re[0], s5_lambda_im[0], s5_log_dt[0], s5_b_re[0], s5_b_im[0],
                                    s5_c_re[0], s5_c_im[0])
    x_tm, u, z = _s5_in(x, norm_w[0], mod, 0, s5_w_in[0])
    pairs = min(S5_PAIRS, seqlen // 2)
    chunks = min(S5_CHUNKS, seqlen // (2 * pairs))
    y = _s5_scan(u, w_b, w_i, w_c, a2, s5_d[0], pairs, chunks)
    x1_tm = _s5_out(y, z, x_tm, mod, 0, s5_w_glu[0], s5_w_out[0])

    heads = gdn_a_log.shape[1]
    q, k, v, zg, gates = _gdn_in(x1_tm, norm_w[1], mod, 1, gdn_w_in[0], gdn_conv_w[0],
                                 gdn_a_log[0], gdn_dt_bias[0])
    yg = _gdn_core(q, k, v, zg, gates, gdn_norm_w[0], heads)
    return _gdn_out(yg, x1_tm, mod, 1, gdn_w_out[0], final_norm_w)
```

```python
import functools
import math

import jax
import jax.numpy as jnp
from jax import lax
from jax.experimental import pallas as pl
from jax.experimental.pallas import tpu as pltpu

F32 = jnp.float32
BF16 = jnp.bfloat16

NORM_EPS = 1e-6
S5_GROUP = 16
S5_STATE = 64
S5_TILE_GROUPS = 16
S5_BLOCK_GROUPS = 8
S5_TILE_BLOCKS = S5_TILE_GROUPS // S5_BLOCK_GROUPS
S5_PAIRS = 64
S5_CHUNKS = 4
GDN_HEADS = 8
GDN_CONV = 4
GDN_CHUNK = 128
GDN_BASE_BLOCK = 8
GDN_BATCH_PER_STEP = 4
ROW_TILE = 512
GDN_IN_ROWS = 256
SUBLANES = 8
LANES = 128
VMEM_LIMIT_BYTES = 56 * 1024 * 1024


def _sigmoid(x):
    return 1.0 / (1.0 + jnp.exp(-x))


def _silu(x):
    return x * _sigmoid(x)


def _gelu_tanh(x):
    c = math.sqrt(2.0 / math.pi)
    return 0.5 * x * (1.0 + jnp.tanh(c * (x + 0.044715 * (x * x * x))))


def _softplus(x):
    return jnp.maximum(x, 0.0) + jnp.log(1.0 + jnp.exp(-jnp.abs(x)))


def _rms(x):
    return x * lax.rsqrt(jnp.mean(x * x, axis=-1, keepdims=True) + NORM_EPS)


def _bdot(a, b):
    return jnp.dot(a.astype(BF16), b.astype(BF16), preferred_element_type=F32)


def _bdot_nt(a, b):
    return lax.dot_general(a.astype(BF16), b.astype(BF16), (((1,), (1,)), ((), ())),
                           preferred_element_type=F32)


def _bdot_tn(a, b):
    return lax.dot_general(a.astype(BF16), b.astype(BF16), (((0,), (0,)), ((), ())),
                           preferred_element_type=F32)


def _split3(x):
    hi = x.astype(BF16)
    r = x - hi.astype(F32)
    mid = r.astype(BF16)
    lo = (r - mid.astype(F32)).astype(BF16)
    return hi, mid, lo


def _params(*sem):
    return pltpu.CompilerParams(dimension_semantics=sem, vmem_limit_bytes=VMEM_LIMIT_BYTES)


def _resident(shape, index_map):
    return pl.BlockSpec(shape, index_map, pipeline_mode=pl.Buffered(1))


MOD_SHIFT, MOD_SCALE, MOD_GATE = 0, 1, 2


def _mod_kernel(c_ref, w_ref, b_ref, o_ref):
    o_ref[...] = _bdot(_silu(c_ref[...]), w_ref[0]) + b_ref[0]


def _modulation(c, ada_w, ada_b):
    depth, d, d3 = ada_w.shape
    bsz = c.shape[0]
    nb = d3 // d
    return pl.pallas_call(
        _mod_kernel,
        out_shape=jax.ShapeDtypeStruct((depth * bsz, d3), F32),
        grid=(depth, nb),
        in_specs=[
            pl.BlockSpec((bsz, d), lambda l, j: (0, 0)),
            pl.BlockSpec((1, d, d), lambda l, j: (l, 0, j)),
            pl.BlockSpec((1, 1, d), lambda l, j: (l, 0, j)),
        ],
        out_specs=pl.BlockSpec((bsz, d), lambda l, j: (l, j)),
        compiler_params=_params("parallel", "parallel"),
        name="adaln_mod",
    )(c, ada_w.astype(BF16), ada_b.reshape(depth, 1, d3))


def _cmul(xr, xi, yr, yi):
    return xr * yr - xi * yi, xr * yi + xi * yr


def _place_nt(sel, vals):
    return sum(lax.dot_general(sel, part, (((1,), (1,)), ((), ())), preferred_element_type=F32)
               for part in _split3(vals))


def _s5_prep_kernel(lre_ref, lim_ref, ldt_ref, bre_ref, bim_ref, cre_ref, cim_ref, wb_ref, wi_ref, wc_ref, a2_ref):
    tg, p, m = bre_ref.shape
    ns, nc = tg * p, tg * m
    lre, lim = lre_ref[0], lim_ref[0]
    dt = jnp.exp(ldt_ref[0])
    mag = jnp.exp(lre * dt)
    a_re = mag * jnp.cos(lim * dt)
    a_im = mag * jnp.sin(lim * dt)
    den = lre * lre + lim * lim
    nr = a_re - 1.0
    q_re = (nr * lre + a_im * lim) / den
    q_im = (a_im * lre - nr * lim) / den
    a2_re, a2_im = _cmul(a_re, a_im, a_re, a_im)
    a2_ref[0] = jnp.broadcast_to(jnp.concatenate([a2_re, a2_im], axis=1), a2_ref.shape[1:])

    def iota(shape, axis):
        return lax.broadcasted_iota(jnp.int32, shape, axis)

    sel_n = (iota((nc, m), 0) % m == iota((nc, m), 1)).astype(BF16)
    zb_re = _place_nt(sel_n, bre_ref[...].reshape(ns, m))
    zb_im = _place_nt(sel_n, bim_ref[...].reshape(ns, m))
    same_in = iota((nc, ns), 0) // m == iota((nc, ns), 1) // p
    bb_re, bb_im = _cmul(q_re, q_im, zb_re, zb_im)
    bb_re = jnp.where(same_in, bb_re, 0.0)
    bb_im = jnp.where(same_in, bb_im, 0.0)
    ab_re, ab_im = _cmul(a_re, a_im, bb_re, bb_im)
    wb = jnp.concatenate([jnp.concatenate([ab_re, ab_im], axis=1),
                          jnp.concatenate([bb_re, bb_im], axis=1)], axis=0).astype(BF16)
    wb_ref[0] = wb

    sel_p = (iota((ns, p), 0) % p == iota((ns, p), 1)).astype(BF16)
    same_out = iota((ns, nc), 0) // p == iota((ns, nc), 1) // m
    c_re = jnp.where(same_out, _place_nt(sel_p, cre_ref[...].reshape(nc, p)), 0.0)
    c_im = jnp.where(same_out, _place_nt(sel_p, cim_ref[...].reshape(nc, p)), 0.0)
    eye_s = (iota((ns, ns), 0) == iota((ns, ns), 1)).astype(BF16)
    col_re = _place_nt(eye_s, jnp.broadcast_to(a_re, (nc, ns)))
    col_im = _place_nt(eye_s, jnp.broadcast_to(a_im, (nc, ns)))
    c1_re, c1_im = _cmul(c_re, c_im, col_re, col_im)
    c2_re, c2_im = _cmul(c1_re, c1_im, col_re, col_im)
    wc_ref[0] = jnp.concatenate([jnp.concatenate([c1_re, c2_re], axis=1),
                                 jnp.concatenate([-c1_im, -c2_im], axis=1)], axis=0).astype(BF16)

    wc0 = jnp.concatenate([c_re, -c_im], axis=0).astype(BF16)
    k0 = jnp.dot(wb[nc:], wc0, preferred_element_type=F32)
    k1 = jnp.dot(wb[:nc], wc0, preferred_element_type=F32)
    wi_ref[0] = jnp.concatenate([jnp.concatenate([k0, k1], axis=1),
                                 jnp.concatenate([jnp.zeros_like(k0), k0], axis=1)], axis=0).astype(BF16)


def _s5_weights(lam_re, lam_im, log_dt, b_re, b_im, c_re, c_im):
    g, p = lam_re.shape
    m = b_re.shape[-1]
    tg = S5_BLOCK_GROUPS
    nb = g // tg
    ns, nc = tg * p, tg * m
    row = lambda a: a.reshape(nb, 1, ns)
    row_spec = pl.BlockSpec((1, 1, ns), lambda i: (i, 0, 0))
    w_b, w_i, w_c, a2 = pl.pallas_call(
        _s5_prep_kernel,
        out_shape=[jax.ShapeDtypeStruct((nb, 2 * nc, 2 * ns), BF16),
                   jax.ShapeDtypeStruct((nb, 2 * nc, 2 * nc), BF16),
                   jax.ShapeDtypeStruct((nb, 2 * ns, 2 * nc), BF16),
                   jax.ShapeDtypeStruct((nb, SUBLANES, 2 * ns), F32)],
        grid=(nb,),
        in_specs=[row_spec, row_spec, row_spec,
                  pl.BlockSpec((tg, p, m), lambda i: (i, 0, 0)), pl.BlockSpec((tg, p, m), lambda i: (i, 0, 0)),
                  pl.BlockSpec((tg, m, p), lambda i: (i, 0, 0)), pl.BlockSpec((tg, m, p), lambda i: (i, 0, 0))],
        out_specs=[pl.BlockSpec((1, 2 * nc, 2 * ns), lambda i: (i, 0, 0)),
                   pl.BlockSpec((1, 2 * nc, 2 * nc), lambda i: (i, 0, 0)),
                   pl.BlockSpec((1, 2 * ns, 2 * nc), lambda i: (i, 0, 0)),
                   pl.BlockSpec((1, SUBLANES, 2 * ns), lambda i: (i, 0, 0))],
        compiler_params=_params("parallel"),
        name="s5_prep",
    )(row(lam_re), row(lam_im), row(jnp.broadcast_to(log_dt[:, None], (g, p))), b_re, b_im, c_re, c_im)
    per_tile = lambda w: w.reshape((nb // S5_TILE_BLOCKS, S5_TILE_BLOCKS) + w.shape[1:])
    return per_tile(w_b), per_tile(w_i), per_tile(w_c), per_tile(a2)


def _modulated_norm_rows(x, nw, scale, shift):
    r, d = x.shape
    y = (_rms(x) * nw).reshape(r // SUBLANES, SUBLANES, d)
    return (y * (1.0 + scale)[None] + shift[None]).reshape(r, d)


def _sequence_rows(slab_ref, b, tmt, lo, hi):
    return jnp.concatenate([slab_ref[j, pl.ds(b, tmt, stride=SUBLANES), :]
                            for j in range(lo // LANES, hi // LANES)], axis=1)


def _s5_in_kernel(x_ref, nw_ref, sc_ref, sh_ref, w_ref, xt_ref, u_ref, z_ref, slab_ref):
    bsz, tmt, d = x_ref.shape
    for j in range(d // LANES):
        for b in range(bsz):
            slab_ref[j, pl.ds(b, tmt, stride=bsz), :] = x_ref[b, :, j * LANES:(j + 1) * LANES]
    x_tm = jnp.concatenate([slab_ref[j] for j in range(d // LANES)], axis=1)
    xt_ref[...] = x_tm
    h = _modulated_norm_rows(x_tm, nw_ref[...], sc_ref[...], sh_ref[...]).astype(BF16)
    e = z_ref.shape[1]
    nt, _, cw = u_ref.shape
    u = jnp.dot(h, w_ref[:, :e], preferred_element_type=F32).astype(BF16)
    for k in range(nt):
        u_ref[k] = u[:, k * cw:(k + 1) * cw]
    z_ref[...] = jnp.dot(h, w_ref[:, e:], preferred_element_type=F32).astype(BF16)


def _s5_in(x, norm_w, mod, layer, w_in):
    bsz, seqlen, d = x.shape
    e = w_in.shape[1] // 2
    tmt = min(ROW_TILE // bsz, seqlen)
    rows = tmt * bsz
    t = seqlen * bsz
    cw = S5_TILE_GROUPS * S5_GROUP
    nt = e // cw
    return pl.pallas_call(
        _s5_in_kernel,
        out_shape=[jax.ShapeDtypeStruct((t, d), F32), jax.ShapeDtypeStruct((nt, t, cw), BF16),
                   jax.ShapeDtypeStruct((t, e), BF16)],
        grid=(seqlen // tmt,),
        in_specs=[
            pl.BlockSpec((bsz, tmt, d), lambda i: (0, i, 0)),
            _resident((1, d), lambda i: (0, 0)),
            _resident((bsz, d), lambda i: (layer, MOD_SCALE)),
            _resident((bsz, d), lambda i: (layer, MOD_SHIFT)),
            _resident((d, 2 * e), lambda i: (0, 0)),
        ],
        out_specs=[pl.BlockSpec((rows, d), lambda i: (i, 0)), pl.BlockSpec((nt, rows, cw), lambda i: (0, i, 0)),
                   pl.BlockSpec((rows, e), lambda i: (i, 0))],
        scratch_shapes=[pltpu.VMEM((d // LANES, rows, LANES), F32)],
        compiler_params=_params("parallel"),
        name="s5_in",
    )(x, norm_w.reshape(1, d), mod, mod, w_in.astype(BF16))


def _s5_scan_kernel(u_ref, wb_ref, wi_ref, wc_ref, a_ref, d_ref, y_ref, sin_ref, xs_ref, st_ref, *, pairs, chunks):
    @pl.when(pl.program_id(1) == 0)
    def _():
        st_ref[...] = jnp.zeros_like(st_ref)

    nblk = wb_ref.shape[1]
    bw = wi_ref.shape[2] // 2
    half = wb_ref.shape[3] // 2
    nl = half // LANES
    rows_c = pairs * SUBLANES
    a_re = [[a_ref[0, k, :, j * LANES:(j + 1) * LANES] for j in range(nl)] for k in range(nblk)]
    a_im = [[a_ref[0, k, :, half + j * LANES:half + (j + 1) * LANES] for j in range(nl)] for k in range(nblk)]
    x_re = [[st_ref[k, 0, :, j * LANES:(j + 1) * LANES] for j in range(nl)] for k in range(nblk)]
    x_im = [[st_ref[k, 1, :, j * LANES:(j + 1) * LANES] for j in range(nl)] for k in range(nblk)]
    intra = {}

    def project_in(ci):
        u = u_ref[0, ci * 2 * rows_c:(ci + 1) * 2 * rows_c, :].astype(F32).reshape(pairs, 2, SUBLANES, nblk * bw)
        u0 = u[:, 0].reshape(rows_c, nblk * bw)
        u1 = u[:, 1].reshape(rows_c, nblk * bw)
        slot = ci % 2
        for k in range(nblk):
            uu = jnp.concatenate([u0[:, k * bw:(k + 1) * bw], u1[:, k * bw:(k + 1) * bw]], axis=1).astype(BF16)
            sin_ref[slot, k] = jnp.dot(uu, wb_ref[0, k], preferred_element_type=F32)
            intra[ci, k] = jnp.dot(uu, wi_ref[0, k], preferred_element_type=F32)

    def recur(ci):
        slot = ci % 2
        for pp in range(pairs // 2):
            r0 = 2 * pp * SUBLANES
            for k in range(nblk):
                for j in range(nl):
                    lo, hi = j * LANES, (j + 1) * LANES
                    before_re, before_im = [], []
                    for s in range(2):
                        rr = r0 + s * SUBLANES
                        before_re.append(x_re[k][j])
                        before_im.append(x_im[k][j])
                        b_re = sin_ref[slot, k, rr:rr + SUBLANES, lo:hi]
                        b_im = sin_ref[slot, k, rr:rr + SUBLANES, half + lo:half + hi]
                        v_re = a_re[k][j] * x_re[k][j] - a_im[k][j] * x_im[k][j] + b_re
                        v_im = a_re[k][j] * x_im[k][j] + a_im[k][j] * x_re[k][j] + b_im
                        x_re[k][j], x_im[k][j] = v_re, v_im
                    xs_ref[slot, k, r0:r0 + 2 * SUBLANES, lo:hi] = jnp.concatenate(before_re, axis=0).astype(BF16)
                    xs_ref[slot, k, r0:r0 + 2 * SUBLANES, half + lo:half + hi] = (
                        jnp.concatenate(before_im, axis=0).astype(BF16))

    def project_out(ci):
        slot = ci % 2
        cols = []
        for k in range(nblk):
            yk = intra.pop((ci, k)) + jnp.dot(xs_ref[slot, k], wc_ref[0, k], preferred_element_type=F32)
            steps = [yk[:, s * bw:(s + 1) * bw].reshape(pairs, 1, SUBLANES, bw) for s in range(2)]
            cols.append(jnp.concatenate(steps, axis=1).reshape(2 * rows_c, bw))
        y = jnp.concatenate(cols, axis=1)
        u = u_ref[0, ci * 2 * rows_c:(ci + 1) * 2 * rows_c, :].astype(F32)
        y_ref[0, ci * 2 * rows_c:(ci + 1) * 2 * rows_c, :] = _gelu_tanh(y + d_ref[0] * u).astype(BF16)

    project_in(0)
    for ci in range(chunks):
        if ci + 1 < chunks:
            project_in(ci + 1)
        recur(ci)
        project_out(ci)

    for k in range(nblk):
        for j in range(nl):
            st_ref[k, 0, :, j * LANES:(j + 1) * LANES] = x_re[k][j]
            st_ref[k, 1, :, j * LANES:(j + 1) * LANES] = x_im[k][j]


def _s5_scan(u_tiles, w_b, w_i, w_c, a2, d_skip, pairs, chunks):
    nt, nblk, kw, sw = w_b.shape
    cw = u_tiles.shape[2]
    t = u_tiles.shape[1]
    rows_c = pairs * SUBLANES
    rows = 2 * rows_c * chunks
    return pl.pallas_call(
        functools.partial(_s5_scan_kernel, pairs=pairs, chunks=chunks),
        out_shape=jax.ShapeDtypeStruct((nt, t, cw), BF16),
        grid=(nt, t // rows),
        in_specs=[
            pl.BlockSpec((1, rows, cw), lambda k, c: (k, c, 0)),
            _resident((1, nblk, kw, sw), lambda k, c: (k, 0, 0, 0)),
            _resident((1, nblk, kw, kw), lambda k, c: (k, 0, 0, 0)),
            _resident((1, nblk, sw, kw), lambda k, c: (k, 0, 0, 0)),
            _resident((1, nblk, SUBLANES, sw), lambda k, c: (k, 0, 0, 0)),
            _resident((1, 1, cw), lambda k, c: (k, 0, 0)),
        ],
        out_specs=pl.BlockSpec((1, rows, cw), lambda k, c: (k, c, 0)),
        scratch_shapes=[
            pltpu.VMEM((2, nblk, rows_c, sw), F32),
            pltpu.VMEM((2, nblk, rows_c, sw), BF16),
            pltpu.VMEM((nblk, 2, SUBLANES, sw // 2), F32),
        ],
        compiler_params=_params("parallel", "arbitrary"),
        name="s5_scan",
    )(u_tiles, w_b, w_i, w_c, a2, d_skip.reshape(nt, 1, cw))


def _s5_out_kernel(y_ref, z_ref, x_ref, gate_ref, wg_ref, wo_ref, o_ref):
    y = jnp.concatenate([y_ref[k] for k in range(y_ref.shape[0])], axis=1)
    yf = y.astype(F32)
    y2 = yf * _sigmoid(jnp.dot(y, wg_ref[...], preferred_element_type=F32)) * _silu(z_ref[...].astype(F32))
    out = jnp.dot(y2.astype(BF16), wo_ref[...], preferred_element_type=F32)
    r, d = out.shape
    x = x_ref[...].reshape(r // SUBLANES, SUBLANES, d)
    o_ref[...] = (x + gate_ref[...][None] * out.reshape(r // SUBLANES, SUBLANES, d)).reshape(r, d)


def _s5_out(y, z, x_tm, mod, layer, w_glu, w_out):
    t, d = x_tm.shape
    e = w_glu.shape[0]
    nt, _, cw = y.shape
    rows = min(ROW_TILE, t)
    row_spec = lambda c: pl.BlockSpec((rows, c), lambda i: (i, 0))
    return pl.pallas_call(
        _s5_out_kernel,
        out_shape=jax.ShapeDtypeStruct((t, d), F32),
        grid=(t // rows,),
        in_specs=[
            pl.BlockSpec((nt, rows, cw), lambda i: (0, i, 0)), row_spec(e), row_spec(d),
            _resident((SUBLANES, d), lambda i: (layer, MOD_GATE)),
            _resident((e, e), lambda i: (0, 0)),
            _resident((e, d), lambda i: (0, 0)),
        ],
        out_specs=row_spec(d),
        compiler_params=_params("parallel"),
        name="s5_out",
    )(y, z, x_tm, mod, w_glu.astype(BF16), w_out.astype(BF16))


def _gdn_in_kernel(x_ref, nw_ref, sc_ref, sh_ref, w_ref, wg_ref, cw_ref, gp_ref,
                   q_ref, k_ref, v_ref, z_ref, g_ref, ext_ref, st_ref, *, heads):
    halo = (GDN_CONV - 1) * SUBLANES
    r, d = x_ref.shape
    cc = ext_ref.shape[1]
    bsz = SUBLANES
    tmt = r // bsz
    qk = q_ref.shape[1] // bsz
    e = v_ref.shape[1] // bsz
    dk = qk // heads

    @pl.when(pl.program_id(0) == 0)
    def _():
        ext_ref[0:halo, :] = jnp.zeros((halo, cc), F32)

    h = _modulated_norm_rows(x_ref[...], nw_ref[...], sc_ref[...], sh_ref[...]).astype(BF16)

    ext_ref[halo:halo + r, :] = jnp.dot(h, w_ref[:, :cc], preferred_element_type=F32)
    conv = cw_ref[0:1, :] * ext_ref[0:r, :]
    for j in range(1, GDN_CONV):
        conv = conv + cw_ref[j:j + 1, :] * ext_ref[j * SUBLANES:j * SUBLANES + r, :]
    ext_ref[0:halo, :] = ext_ref[r:r + halo, :]
    act = _silu(conv)

    def stage(lo, val):
        for j in range(val.shape[1] // LANES):
            st_ref[lo // LANES + j] = val[:, j * LANES:(j + 1) * LANES]

    for hh in range(heads):
        qh = act[:, hh * dk:(hh + 1) * dk]
        kh = act[:, qk + hh * dk:qk + (hh + 1) * dk]
        stage(hh * dk, qh * lax.rsqrt(jnp.sum(qh * qh, axis=-1, keepdims=True) + NORM_EPS) * (dk ** -0.5))
        stage(qk + hh * dk, kh * lax.rsqrt(jnp.sum(kh * kh, axis=-1, keepdims=True) + NORM_EPS))
    stage(2 * qk, act[:, 2 * qk:])
    stage(cc, jnp.dot(h, w_ref[:, cc:], preferred_element_type=F32))

    logit = jnp.dot(h, wg_ref[...], preferred_element_type=F32)
    beta = _sigmoid(logit)
    g = -jnp.exp(gp_ref[0:1, :]) * _softplus(logit + gp_ref[1:2, :])
    lane = lax.broadcasted_iota(jnp.int32, logit.shape, 1)
    stage(cc + e, jnp.where(lane < heads, beta, g))

    for b in range(bsz):
        seq = lambda lo, hi: _sequence_rows(st_ref, b, tmt, lo, hi)
        q_ref[:, b * qk:(b + 1) * qk] = seq(0, qk).astype(BF16)
        k_ref[:, b * qk:(b + 1) * qk] = seq(qk, 2 * qk).astype(BF16)
        v_ref[:, b * e:(b + 1) * e] = seq(2 * qk, cc).astype(BF16)
        z_ref[:, b * e:(b + 1) * e] = seq(cc, cc + e).astype(BF16)
        g_ref[:, b * LANES:(b + 1) * LANES] = seq(cc + e, cc + e + LANES)


def _gdn_in(x_tm, norm_w, mod, layer, w_in, conv_w, a_log, dt_bias):
    t, d = x_tm.shape
    bsz = SUBLANES
    seqlen = t // bsz
    heads = a_log.shape[0]
    cc = conv_w.shape[1]
    e = (w_in.shape[1] - cc - 2 * heads)
    qk = (cc - e) // 2
    rows = min(GDN_IN_ROWS, t)
    tmt = rows // bsz
    w_main = w_in[:, :cc + e].astype(BF16)
    w_gate = jnp.pad(w_in[:, cc + e:], ((0, 0), (0, LANES - 2 * heads))).astype(BF16)
    gate_par = jnp.pad(jnp.stack([a_log, dt_bias]), ((0, 0), (heads, LANES - 2 * heads)))
    seq_spec = lambda c: pl.BlockSpec((tmt, bsz * c), lambda i: (i, 0))
    return pl.pallas_call(
        functools.partial(_gdn_in_kernel, heads=heads),
        out_shape=[
            jax.ShapeDtypeStruct((seqlen, bsz * qk), BF16),
            jax.ShapeDtypeStruct((seqlen, bsz * qk), BF16),
            jax.ShapeDtypeStruct((seqlen, bsz * e), BF16),
            jax.ShapeDtypeStruct((seqlen, bsz * e), BF16),
            jax.ShapeDtypeStruct((seqlen, bsz * LANES), F32),
        ],
        grid=(t // rows,),
        in_specs=[
            pl.BlockSpec((rows, d), lambda i: (i, 0)),
            _resident((1, d), lambda i: (0, 0)),
            _resident((SUBLANES, d), lambda i: (layer, MOD_SCALE)),
            _resident((SUBLANES, d), lambda i: (layer, MOD_SHIFT)),
            _resident((d, cc + e), lambda i: (0, 0)),
            _resident((d, LANES), lambda i: (0, 0)),
            _resident((GDN_CONV, cc), lambda i: (0, 0)),
            _resident((2, LANES), lambda i: (0, 0)),
        ],
        out_specs=[seq_spec(qk), seq_spec(qk), seq_spec(e), seq_spec(e), seq_spec(LANES)],
        scratch_shapes=[
            pltpu.VMEM((rows + (GDN_CONV - 1) * SUBLANES, cc), F32),
            pltpu.VMEM(((cc + e + LANES) // LANES, rows, LANES), F32),
        ],
        compiler_params=_params("arbitrary"),
        name="gdn_in",
    )(x_tm, norm_w.reshape(1, d), mod, mod, w_main, w_gate, conv_w, gate_par)


def _pair_block_diag(x):
    c = x.shape[0]
    lane = lax.broadcasted_iota(jnp.int32, x.shape, 1)
    zero = jnp.zeros_like(x)
    return jnp.concatenate([jnp.where(lane < c, x, zero), jnp.where(lane >= c, x, zero)], axis=0)


def _pair_mm(x, y):
    return jnp.dot(x.astype(BF16), _pair_block_diag(y.astype(BF16)), preferred_element_type=F32)


def _gdn_core_kernel(q_ref, k_ref, v_ref, z_ref, g_ref, nw_ref, y_ref, s_ref, *, heads, nb):
    @pl.when(pl.program_id(1) == 0)
    def _():
        s_ref[...] = jnp.zeros_like(s_ref)

    c = q_ref.shape[0]
    qk_w = q_ref.shape[1] // nb
    e_w = v_ref.shape[1] // nb
    dk = qk_w // heads
    dv = e_w // heads
    units = [(b, p) for b in range(nb) for p in range(heads // 2)]

    row = lax.broadcasted_iota(jnp.int32, (c, 2 * c), 0)
    col = lax.broadcasted_iota(jnp.int32, (c, 2 * c), 1) & (c - 1)
    causal = row >= col
    strict = row > col
    eye = (row == col).astype(F32)
    row1 = lax.broadcasted_iota(jnp.int32, (c, c), 0)
    col1 = lax.broadcasted_iota(jnp.int32, (c, c), 1)
    tri = (row1 >= col1).astype(BF16)
    nh2 = 2 * heads
    eye_h = (lax.broadcasted_iota(jnp.int32, (nh2, nh2), 0)
             == lax.broadcasted_iota(jnp.int32, (nh2, nh2), 1)).astype(BF16)

    def same_block(shift):
        return (row >> shift) == (col >> shift)

    gb, gc, gc_t = [], [], []
    for b in range(nb):
        g = g_ref[:, b * LANES:b * LANES + nh2]
        cs = sum(jnp.dot(tri, part, preferred_element_type=F32) for part in _split3(g))
        gb.append(g)
        gc.append(cs)
        gc_t.append(sum(lax.dot_general(eye_h, part, (((1,), (1,)), ((), ())), preferred_element_type=F32)
                        for part in _split3(cs)))

    def lanes2(ref, b, p, width, per_batch):
        return ref[:, b * per_batch + 2 * p * width:b * per_batch + 2 * (p + 1) * width]

    def pair_cols(b, p, fn, width):
        return jnp.concatenate([jnp.broadcast_to(fn(b, 2 * p + i), (c, width)) for i in range(2)], axis=1)

    beta_c = lambda b, h: gb[b][:, h:h + 1]
    gcol = lambda b, h: gc[b][:, heads + h:heads + h + 1]
    grow = lambda b, h: gc_t[b][heads + h:heads + h + 1, :]

    a_mat, qk_mat = {}, {}
    for (b, p) in units:
        qp = lanes2(q_ref, b, p, dk, qk_w)
        kp = lanes2(k_ref, b, p, dk, qk_w)
        res = lax.dot_general(jnp.concatenate([qp, kp], axis=0), _pair_block_diag(kp),
                              (((1,), (1,)), ((), ())), preferred_element_type=F32)
        decay = jnp.where(causal, jnp.exp(jnp.concatenate(
            [gcol(b, 2 * p + i) - grow(b, 2 * p + i) for i in range(2)], axis=1)), 0.0)
        qk_mat[b, p] = res[:c] * decay
        a_mat[b, p] = jnp.where(strict, pair_cols(b, p, beta_c, c) * res[c:] * decay, 0.0)

    shift0 = GDN_BASE_BLOCK.bit_length() - 1
    dblk = {u: jnp.where(same_block(shift0), a_mat[u], 0.0) for u in units}
    d2 = {u: _pair_mm(dblk[u], dblk[u]) for u in units}
    d4 = {u: _pair_mm(d2[u], d2[u]) for u in units}
    t_inv = {u: eye - dblk[u] for u in units}
    for pw in (d2, d4):
        upd = {u: _pair_mm(t_inv[u], pw[u]) for u in units}
        t_inv = {u: t_inv[u] + upd[u] for u in units}
    shift = shift0
    while (1 << shift) < c:
        level = same_block(shift + 1) & jnp.logical_not(same_block(shift))
        xt = {u: _pair_mm(jnp.where(level, a_mat[u], 0.0), t_inv[u]) for u in units}
        upd = {u: _pair_mm(t_inv[u], xt[u]) for u in units}
        t_inv = {u: t_inv[u] - upd[u] for u in units}
        shift += 1

    w_pk, u_pk, gam_pk = {}, {}, {}
    for (b, p) in units:
        gam_pk[b, p] = jnp.exp(pair_cols(b, p, gcol, dk))
        kp = lanes2(k_ref, b, p, dk, qk_w).astype(F32)
        vp = lanes2(v_ref, b, p, dv, e_w).astype(F32)
        rw = (pair_cols(b, p, beta_c, dk) * gam_pk[b, p] * kp).astype(BF16)
        ru = (pair_cols(b, p, beta_c, dv) * vp).astype(BF16)
        zk = jnp.zeros((c, dk), BF16)
        zv = jnp.zeros((c, dv), BF16)
        rhs = jnp.concatenate([
            jnp.concatenate([rw[:, :dk], zk, ru[:, :dv], zv], axis=1),
            jnp.concatenate([zk, rw[:, dk:], zv, ru[:, dv:]], axis=1)], axis=0)
        wu = jnp.dot(t_inv[b, p].astype(BF16), rhs, preferred_element_type=F32)
        w_pk[b, p] = wu[:, :2 * dk]
        u_pk[b, p] = wu[:, 2 * dk:]

    hunits = [(b, h) for b in range(nb) for h in range(heads)]
    ws, states = {}, {}
    for (b, h) in hunits:
        p, i = divmod(h, 2)
        qd = lanes2(q_ref, b, p, dk, qk_w).astype(F32) * gam_pk[b, p]
        lhs = jnp.concatenate([w_pk[b, p][:, i * dk:(i + 1) * dk], qd[:, i * dk:(i + 1) * dk]], axis=0)
        states[b, h] = s_ref[b * heads + h]
        ws[b, h] = _bdot(lhs, states[b, h])
    v_new = {(b, h): u_pk[b, h // 2][:, (h % 2) * dv:(h % 2 + 1) * dv] - ws[b, h][:c] for (b, h) in hunits}
    o_intra = {(b, h): _bdot(qk_mat[b, h // 2][:, (h % 2) * c:(h % 2 + 1) * c], v_new[b, h]) for (b, h) in hunits}
    for (b, h) in hunits:
        g_last = gcol(b, h)[c - 1:c, :]
        kh = k_ref[:, b * qk_w + h * dk:b * qk_w + (h + 1) * dk].astype(F32)
        k_dec = kh * jnp.exp(g_last - gcol(b, h))
        s_ref[b * heads + h] = jnp.exp(g_last) * states[b, h] + _bdot_tn(k_dec, v_new[b, h])
    for (b, h) in hunits:
        o = ws[b, h][c:] + o_intra[b, h]
        on = _rms(o) * nw_ref[...]
        zh = z_ref[:, b * e_w + h * dv:b * e_w + (h + 1) * dv].astype(F32)
        y_ref[:, b * e_w + h * dv:b * e_w + (h + 1) * dv] = (on * _silu(zh)).astype(BF16)


def _gdn_core(q, k, v, z, gates, norm_w, heads):
    seqlen = q.shape[0]
    nb = GDN_BATCH_PER_STEP
    bsz = gates.shape[1] // LANES
    qk = q.shape[1] // bsz
    e = v.shape[1] // bsz
    c = min(GDN_CHUNK, seqlen)
    col_spec = lambda w: pl.BlockSpec((c, nb * w), lambda b, n: (n, b))
    return pl.pallas_call(
        functools.partial(_gdn_core_kernel, heads=heads, nb=nb),
        out_shape=jax.ShapeDtypeStruct((seqlen, bsz * e), BF16),
        grid=(bsz // nb, seqlen // c),
        in_specs=[col_spec(qk), col_spec(qk), col_spec(e), col_spec(e), col_spec(LANES),
                  _resident((1, e // heads), lambda b, n: (0, 0))],
        out_specs=col_spec(e),
        scratch_shapes=[pltpu.VMEM((nb * heads, qk // heads, e // heads), F32)],
        compiler_params=_params("parallel", "arbitrary"),
        name="gdn_core",
    )(q, k, v, z, gates, norm_w.reshape(1, e // heads))


def _gdn_out_kernel(y_ref, x_ref, gate_ref, w_ref, fw_ref, o_ref, slab_ref):
    bsz, tmt, d = o_ref.shape
    e = y_ref.shape[1] // bsz
    for j in range(d // LANES):
        slab_ref[j] = x_ref[:, j * LANES:(j + 1) * LANES]
    y = jnp.concatenate([y_ref[:, b * e:(b + 1) * e] for b in range(bsz)], axis=0)
    out = jnp.dot(y, w_ref[...], preferred_element_type=F32).reshape(bsz, tmt, d)
    for b in range(bsz):
        xn = _sequence_rows(slab_ref, b, tmt, 0, d) + gate_ref[b:b + 1, :] * out[b]
        o_ref[b] = _rms(xn) * fw_ref[...]


def _gdn_out(y, x_tm, mod, layer, w_out, final_w):
    e, d = w_out.shape
    seqlen = y.shape[0]
    bsz = y.shape[1] // e
    tmt = min(ROW_TILE // bsz, seqlen)
    return pl.pallas_call(
        _gdn_out_kernel,
        out_shape=jax.ShapeDtypeStruct((bsz, seqlen, d), F32),
        grid=(seqlen // tmt,),
        in_specs=[
            pl.BlockSpec((tmt, bsz * e), lambda i: (i, 0)),
            pl.BlockSpec((tmt * bsz, d), lambda i: (i, 0)),
            _resident((bsz, d), lambda i: (layer, MOD_GATE)),
            _resident((e, d), lambda i: (0, 0)),
            _resident((1, d), lambda i: (0, 0)),
        ],
        out_specs=pl.BlockSpec((bsz, tmt, d), lambda i: (0, i, 0)),
        scratch_shapes=[pltpu.VMEM((d // LANES, tmt * bsz, LANES), F32)],
        compiler_params=_params("parallel"),
        name="gdn_out",
    )(y, x_tm, mod, w_out.astype(BF16), final_w.reshape(1, d))


def kernel(x, c, ada_w, ada_b, norm_w, s5_w_in, s5_lambda_re, s5_lambda_im, s5_log_dt, s5_b_re, s5_b_im,
           s5_c_re, s5_c_im, s5_d, s5_w_glu, s5_w_out, gdn_w_in, gdn_conv_w, gdn_a_log, gdn_dt_bias,
           gdn_norm_w, gdn_w_out, final_norm_w):
    bsz, seqlen, d = x.shape
    assert bsz == SUBLANES, "the time-major layout keeps one batch row per vector sublane"
    assert ada_w.shape[0] == 2 and s5_w_in.shape[0] == 1 and gdn_w_in.shape[0] == 1

    mod = _modulation(c, ada_w, ada_b)

    w_b, w_i, w_c, a2 = _s5_weights(s5_lambda_re[0], s5_lambda_im[0], s5_log_dt[0], s5_b_re[0], s5_b_im[0],
                                    s5_c_re[0], s5_c_im[0])
    x_tm, u, z = _s5_in(x, norm_w[0], mod, 0, s5_w_in[0])
    pairs = min(S5_PAIRS, seqlen // 2)
    chunks = min(S5_CHUNKS, seqlen // (2 * pairs))
    y = _s5_scan(u, w_b, w_i, w_c, a2, s5_d[0], pairs, chunks)
    x1_tm = _s5_out(y, z, x_tm, mod, 0, s5_w_glu[0], s5_w_out[0])

    heads = gdn_a_log.shape[1]
    q, k, v, zg, gates = _gdn_in(x1_tm, norm_w[1], mod, 1, gdn_w_in[0], gdn_conv_w[0],
                                 gdn_a_log[0], gdn_dt_bias[0])
    yg = _gdn_core(q, k, v, zg, gates, gdn_norm_w[0], heads)
    return _gdn_out(yg, x1_tm, mod, 1, gdn_w_out[0], final_norm_w)
```

```python
import functools
import math

import jax
import jax.numpy as jnp
from jax import lax
from jax.experimental import pallas as pl
from jax.experimental.pallas import tpu as pltpu

F32 = jnp.float32
BF16 = jnp.bfloat16

NORM_EPS = 1e-6
S5_GROUP = 16
S5_STATE = 64
S5_TILE_GROUPS = 16
S5_RUN = 4
S5_BLOCK_GROUPS = S5_TILE_GROUPS // S5_RUN
S5_TILE_BLOCKS = S5_TILE_GROUPS // S5_BLOCK_GROUPS
S5_RUNS = 64
S5_CHUNKS = 4
GDN_HEADS = 8
GDN_CONV = 4
GDN_CHUNK = 128
GDN_BASE_BLOCK = 8
GDN_BATCH_PER_STEP = 4
ROW_TILE = 512
GDN_IN_ROWS = 256
SUBLANES = 8
LANES = 128
VMEM_LIMIT_BYTES = 56 * 1024 * 1024


def _sigmoid(x):
    return 1.0 / (1.0 + jnp.exp(-x))


def _silu(x):
    return x * _sigmoid(x)


def _gelu_tanh(x):
    c = math.sqrt(2.0 / math.pi)
    return 0.5 * x * (1.0 + jnp.tanh(c * (x + 0.044715 * (x * x * x))))


def _softplus(x):
    return jnp.maximum(x, 0.0) + jnp.log(1.0 + jnp.exp(-jnp.abs(x)))


def _rms(x):
    return x * lax.rsqrt(jnp.mean(x * x, axis=-1, keepdims=True) + NORM_EPS)


def _bdot(a, b):
    return jnp.dot(a.astype(BF16), b.astype(BF16), preferred_element_type=F32)


def _bdot_nt(a, b):
    return lax.dot_general(a.astype(BF16), b.astype(BF16), (((1,), (1,)), ((), ())),
                           preferred_element_type=F32)


def _bdot_tn(a, b):
    return lax.dot_general(a.astype(BF16), b.astype(BF16), (((0,), (0,)), ((), ())),
                           preferred_element_type=F32)


def _split3(x):
    hi = x.astype(BF16)
    r = x - hi.astype(F32)
    mid = r.astype(BF16)
    lo = (r - mid.astype(F32)).astype(BF16)
    return hi, mid, lo


def _params(*sem):
    return pltpu.CompilerParams(dimension_semantics=sem, vmem_limit_bytes=VMEM_LIMIT_BYTES)


def _resident(shape, index_map):
    return pl.BlockSpec(shape, index_map, pipeline_mode=pl.Buffered(1))


MOD_SHIFT, MOD_SCALE, MOD_GATE = 0, 1, 2


def _mod_kernel(c_ref, w_ref, b_ref, o_ref):
    o_ref[...] = _bdot(_silu(c_ref[...]), w_ref[0]) + b_ref[0]


def _modulation(c, ada_w, ada_b):
    depth, d, d3 = ada_w.shape
    bsz = c.shape[0]
    nb = d3 // d
    return pl.pallas_call(
        _mod_kernel,
        out_shape=jax.ShapeDtypeStruct((depth * bsz, d3), F32),
        grid=(depth, nb),
        in_specs=[
            pl.BlockSpec((bsz, d), lambda l, j: (0, 0)),
            pl.BlockSpec((1, d, d), lambda l, j: (l, 0, j)),
            pl.BlockSpec((1, 1, d), lambda l, j: (l, 0, j)),
        ],
        out_specs=pl.BlockSpec((bsz, d), lambda l, j: (l, j)),
        compiler_params=_params("parallel", "parallel"),
        name="adaln_mod",
    )(c, ada_w.astype(BF16), ada_b.reshape(depth, 1, d3))


def _cmul(xr, xi, yr, yi):
    return xr * yr - xi * yi, xr * yi + xi * yr


def _place_nt(sel, vals):
    return sum(lax.dot_general(sel, part, (((1,), (1,)), ((), ())), preferred_element_type=F32)
               for part in _split3(vals))


def _s5_prep_kernel(lre_ref, lim_ref, ldt_ref, bre_ref, bim_ref, cre_ref, cim_ref, wb_ref, wi_ref, wc_ref, aq_ref,
                    *, q):
    tg, p, m = bre_ref.shape
    ns, nc = tg * p, tg * m
    lre, lim = lre_ref[0], lim_ref[0]
    dt = jnp.exp(ldt_ref[0])
    mag = jnp.exp(lre * dt)
    a_re = mag * jnp.cos(lim * dt)
    a_im = mag * jnp.sin(lim * dt)
    den = lre * lre + lim * lim
    nr = a_re - 1.0
    q_re = (nr * lre + a_im * lim) / den
    q_im = (a_im * lre - nr * lim) / den

    def iota(shape, axis):
        return lax.broadcasted_iota(jnp.int32, shape, axis)

    sel_n = (iota((nc, m), 0) % m == iota((nc, m), 1)).astype(BF16)
    zb_re = _place_nt(sel_n, bre_ref[...].reshape(ns, m))
    zb_im = _place_nt(sel_n, bim_ref[...].reshape(ns, m))
    same_in = iota((nc, ns), 0) // m == iota((nc, ns), 1) // p
    bb_re, bb_im = _cmul(q_re, q_im, zb_re, zb_im)
    bb = [(jnp.where(same_in, bb_re, 0.0), jnp.where(same_in, bb_im, 0.0))]
    pw_re, pw_im = a_re, a_im
    for _ in range(1, q):
        bb.append(_cmul(pw_re, pw_im, *bb[0]))
        pw_re, pw_im = _cmul(pw_re, pw_im, a_re, a_im)
    aq_ref[0] = jnp.broadcast_to(jnp.concatenate([pw_re, pw_im], axis=1), aq_ref.shape[1:])
    wb = jnp.concatenate([jnp.concatenate(bb[q - 1 - s], axis=1) for s in range(q)], axis=0).astype(BF16)
    wb_ref[0] = wb

    sel_p = (iota((ns, p), 0) % p == iota((ns, p), 1)).astype(BF16)
    same_out = iota((ns, nc), 0) // p == iota((ns, nc), 1) // m
    c_re = jnp.where(same_out, _place_nt(sel_p, cre_ref[...].reshape(nc, p)), 0.0)
    c_im = jnp.where(same_out, _place_nt(sel_p, cim_ref[...].reshape(nc, p)), 0.0)
    eye_s = (iota((ns, ns), 0) == iota((ns, ns), 1)).astype(BF16)
    col_re = _place_nt(eye_s, jnp.broadcast_to(a_re, (nc, ns)))
    col_im = _place_nt(eye_s, jnp.broadcast_to(a_im, (nc, ns)))
    ca = [(c_re, c_im)]
    for _ in range(q):
        ca.append(_cmul(*ca[-1], col_re, col_im))
    wc_ref[0] = jnp.concatenate([jnp.concatenate([ca[k][0] for k in range(1, q + 1)], axis=1),
                                 jnp.concatenate([-ca[k][1] for k in range(1, q + 1)], axis=1)], axis=0).astype(BF16)

    wc0 = jnp.concatenate([c_re, -c_im], axis=0).astype(BF16)
    kd = [jnp.dot(wb[(q - 1 - d) * nc:(q - d) * nc], wc0, preferred_element_type=F32) for d in range(q)]
    zero = jnp.zeros_like(kd[0])
    wi_ref[0] = jnp.concatenate(
        [jnp.concatenate([kd[so - si] if so >= si else zero for so in range(q)], axis=1) for si in range(q)],
        axis=0).astype(BF16)


def _s5_weights(lam_re, lam_im, log_dt, b_re, b_im, c_re, c_im):
    g, p = lam_re.shape
    m = b_re.shape[-1]
    q = S5_RUN
    tg = S5_BLOCK_GROUPS
    nb = g // tg
    ns, nc = tg * p, tg * m
    row = lambda a: a.reshape(nb, 1, ns)
    row_spec = pl.BlockSpec((1, 1, ns), lambda i: (i, 0, 0))
    w_b, w_i, w_c, aq = pl.pallas_call(
        functools.partial(_s5_prep_kernel, q=q),
        out_shape=[jax.ShapeDtypeStruct((nb, q * nc, 2 * ns), BF16),
                   jax.ShapeDtypeStruct((nb, q * nc, q * nc), BF16),
                   jax.ShapeDtypeStruct((nb, 2 * ns, q * nc), BF16),
                   jax.ShapeDtypeStruct((nb, SUBLANES, 2 * ns), F32)],
        grid=(nb,),
        in_specs=[row_spec, row_spec, row_spec,
                  pl.BlockSpec((tg, p, m), lambda i: (i, 0, 0)), pl.BlockSpec((tg, p, m), lambda i: (i, 0, 0)),
                  pl.BlockSpec((tg, m, p), lambda i: (i, 0, 0)), pl.BlockSpec((tg, m, p), lambda i: (i, 0, 0))],
        out_specs=[pl.BlockSpec((1, q * nc, 2 * ns), lambda i: (i, 0, 0)),
                   pl.BlockSpec((1, q * nc, q * nc), lambda i: (i, 0, 0)),
                   pl.BlockSpec((1, 2 * ns, q * nc), lambda i: (i, 0, 0)),
                   pl.BlockSpec((1, SUBLANES, 2 * ns), lambda i: (i, 0, 0))],
        compiler_params=_params("parallel"),
        name="s5_prep",
    )(row(lam_re), row(lam_im), row(jnp.broadcast_to(log_dt[:, None], (g, p))), b_re, b_im, c_re, c_im)
    per_tile = lambda w: w.reshape((nb // S5_TILE_BLOCKS, S5_TILE_BLOCKS) + w.shape[1:])
    return per_tile(w_b), per_tile(w_i), per_tile(w_c), per_tile(aq)


def _modulated_norm_rows(x, nw, scale, shift):
    r, d = x.shape
    y = (_rms(x) * nw).reshape(r // SUBLANES, SUBLANES, d)
    return (y * (1.0 + scale)[None] + shift[None]).reshape(r, d)


def _sequence_rows(slab_ref, b, tmt, lo, hi):
    return jnp.concatenate([slab_ref[j, pl.ds(b, tmt, stride=SUBLANES), :]
                            for j in range(lo // LANES, hi // LANES)], axis=1)


def _s5_in_kernel(x_ref, nw_ref, sc_ref, sh_ref, w_ref, xt_ref, u_ref, z_ref, slab_ref):
    bsz, tmt, d = x_ref.shape
    for j in range(d // LANES):
        for b in range(bsz):
            slab_ref[j, pl.ds(b, tmt, stride=bsz), :] = x_ref[b, :, j * LANES:(j + 1) * LANES]
    x_tm = jnp.concatenate([slab_ref[j] for j in range(d // LANES)], axis=1)
    xt_ref[...] = x_tm
    h = _modulated_norm_rows(x_tm, nw_ref[...], sc_ref[...], sh_ref[...]).astype(BF16)
    e = z_ref.shape[1]
    nt, _, cw = u_ref.shape
    u = jnp.dot(h, w_ref[:, :e], preferred_element_type=F32).astype(BF16)
    for k in range(nt):
        u_ref[k] = u[:, k * cw:(k + 1) * cw]
    z_ref[...] = jnp.dot(h, w_ref[:, e:], preferred_element_type=F32).astype(BF16)


def _s5_in(x, norm_w, mod, layer, w_in):
    bsz, seqlen, d = x.shape
    e = w_in.shape[1] // 2
    tmt = min(ROW_TILE // bsz, seqlen)
    rows = tmt * bsz
    t = seqlen * bsz
    cw = S5_TILE_GROUPS * S5_GROUP
    nt = e // cw
    return pl.pallas_call(
        _s5_in_kernel,
        out_shape=[jax.ShapeDtypeStruct((t, d), F32), jax.ShapeDtypeStruct((nt, t, cw), BF16),
                   jax.ShapeDtypeStruct((t, e), BF16)],
        grid=(seqlen // tmt,),
        in_specs=[
            pl.BlockSpec((bsz, tmt, d), lambda i: (0, i, 0)),
            _resident((1, d), lambda i: (0, 0)),
            _resident((bsz, d), lambda i: (layer, MOD_SCALE)),
            _resident((bsz, d), lambda i: (layer, MOD_SHIFT)),
            _resident((d, 2 * e), lambda i: (0, 0)),
        ],
        out_specs=[pl.BlockSpec((rows, d), lambda i: (i, 0)), pl.BlockSpec((nt, rows, cw), lambda i: (0, i, 0)),
                   pl.BlockSpec((rows, e), lambda i: (i, 0))],
        scratch_shapes=[pltpu.VMEM((d // LANES, rows, LANES), F32)],
        compiler_params=_params("parallel"),
        name="s5_in",
    )(x, norm_w.reshape(1, d), mod, mod, w_in.astype(BF16))


def _s5_scan_kernel(u_ref, wb_ref, wi_ref, wc_ref, a_ref, d_ref, y_ref, sin_ref, xs_ref, st_ref, *, runs, chunks, q):
    @pl.when(pl.program_id(1) == 0)
    def _():
        st_ref[...] = jnp.zeros_like(st_ref)

    nblk = wb_ref.shape[1]
    bw = wi_ref.shape[2] // q
    half = wb_ref.shape[3] // 2
    nl = half // LANES
    rows_c = runs * SUBLANES
    a_re = [[a_ref[0, k, :, j * LANES:(j + 1) * LANES] for j in range(nl)] for k in range(nblk)]
    a_im = [[a_ref[0, k, :, half + j * LANES:half + (j + 1) * LANES] for j in range(nl)] for k in range(nblk)]
    x_re = [[st_ref[k, 0, :, j * LANES:(j + 1) * LANES] for j in range(nl)] for k in range(nblk)]
    x_im = [[st_ref[k, 1, :, j * LANES:(j + 1) * LANES] for j in range(nl)] for k in range(nblk)]
    intra = {}

    def project_in(ci):
        u = u_ref[0, ci * q * rows_c:(ci + 1) * q * rows_c, :].astype(F32).reshape(runs, q, SUBLANES, nblk * bw)
        us = [u[:, s].reshape(rows_c, nblk * bw) for s in range(q)]
        slot = ci % 2
        for k in range(nblk):
            uu = jnp.concatenate([us[s][:, k * bw:(k + 1) * bw] for s in range(q)], axis=1).astype(BF16)
            sin_ref[slot, k] = jnp.dot(uu, wb_ref[0, k], preferred_element_type=F32)
            intra[ci, k] = jnp.dot(uu, wi_ref[0, k], preferred_element_type=F32)

    def recur(ci):
        slot = ci % 2
        for pp in range(runs // 2):
            r0 = 2 * pp * SUBLANES
            for k in range(nblk):
                for j in range(nl):
                    lo, hi = j * LANES, (j + 1) * LANES
                    before_re, before_im = [], []
                    for s in range(2):
                        rr = r0 + s * SUBLANES
                        before_re.append(x_re[k][j])
                        before_im.append(x_im[k][j])
                        b_re = sin_ref[slot, k, rr:rr + SUBLANES, lo:hi]
                        b_im = sin_ref[slot, k, rr:rr + SUBLANES, half + lo:half + hi]
                        v_re = a_re[k][j] * x_re[k][j] - a_im[k][j] * x_im[k][j] + b_re
                        v_im = a_re[k][j] * x_im[k][j] + a_im[k][j] * x_re[k][j] + b_im
                        x_re[k][j], x_im[k][j] = v_re, v_im
                    xs_ref[slot, k, r0:r0 + 2 * SUBLANES, lo:hi] = jnp.concatenate(before_re, axis=0).astype(BF16)
                    xs_ref[slot, k, r0:r0 + 2 * SUBLANES, half + lo:half + hi] = (
                        jnp.concatenate(before_im, axis=0).astype(BF16))

    def project_out(ci):
        slot = ci % 2
        yk = [intra.pop((ci, k)) + jnp.dot(xs_ref[slot, k], wc_ref[0, k], preferred_element_type=F32)
              for k in range(nblk)]
        steps = [jnp.concatenate([yk[k][:, s * bw:(s + 1) * bw] for k in range(nblk)], axis=1)
                 .reshape(runs, 1, SUBLANES, nblk * bw) for s in range(q)]
        y = jnp.concatenate(steps, axis=1).reshape(q * rows_c, nblk * bw)
        u = u_ref[0, ci * q * rows_c:(ci + 1) * q * rows_c, :].astype(F32)
        y_ref[0, ci * q * rows_c:(ci + 1) * q * rows_c, :] = _gelu_tanh(y + d_ref[0] * u).astype(BF16)

    project_in(0)
    for ci in range(chunks):
        if ci + 1 < chunks:
            project_in(ci + 1)
        recur(ci)
        project_out(ci)

    for k in range(nblk):
        for j in range(nl):
            st_ref[k, 0, :, j * LANES:(j + 1) * LANES] = x_re[k][j]
            st_ref[k, 1, :, j * LANES:(j + 1) * LANES] = x_im[k][j]


def _s5_scan(u_tiles, w_b, w_i, w_c, aq, d_skip, runs, chunks):
    nt, nblk, kw, sw = w_b.shape
    q = S5_RUN
    cw = u_tiles.shape[2]
    t = u_tiles.shape[1]
    rows_c = runs * SUBLANES
    rows = q * rows_c * chunks
    return pl.pallas_call(
        functools.partial(_s5_scan_kernel, runs=runs, chunks=chunks, q=q),
        out_shape=jax.ShapeDtypeStruct((nt, t, cw), BF16),
        grid=(nt, t // rows),
        in_specs=[
            pl.BlockSpec((1, rows, cw), lambda k, c: (k, c, 0)),
            _resident((1, nblk, kw, sw), lambda k, c: (k, 0, 0, 0)),
            _resident((1, nblk, kw, kw), lambda k, c: (k, 0, 0, 0)),
            _resident((1, nblk, sw, kw), lambda k, c: (k, 0, 0, 0)),
            _resident((1, nblk, SUBLANES, sw), lambda k, c: (k, 0, 0, 0)),
            _resident((1, 1, cw), lambda k, c: (k, 0, 0)),
        ],
        out_specs=pl.BlockSpec((1, rows, cw), lambda k, c: (k, c, 0)),
        scratch_shapes=[
            pltpu.VMEM((2, nblk, rows_c, sw), F32),
            pltpu.VMEM((2, nblk, rows_c, sw), BF16),
            pltpu.VMEM((nblk, 2, SUBLANES, sw // 2), F32),
        ],
        compiler_params=_params("parallel", "arbitrary"),
        name="s5_scan",
    )(u_tiles, w_b, w_i, w_c, aq, d_skip.reshape(nt, 1, cw))


def _s5_out_kernel(y_ref, z_ref, x_ref, gate_ref, wg_ref, wo_ref, o_ref):
    y = jnp.concatenate([y_ref[k] for k in range(y_ref.shape[0])], axis=1)
    yf = y.astype(F32)
    y2 = yf * _sigmoid(jnp.dot(y, wg_ref[...], preferred_element_type=F32)) * _silu(z_ref[...].astype(F32))
    out = jnp.dot(y2.astype(BF16), wo_ref[...], preferred_element_type=F32)
    r, d = out.shape
    x = x_ref[...].reshape(r // SUBLANES, SUBLANES, d)
    o_ref[...] = (x + gate_ref[...][None] * out.reshape(r // SUBLANES, SUBLANES, d)).reshape(r, d)


def _s5_out(y, z, x_tm, mod, layer, w_glu, w_out):
    t, d = x_tm.shape
    e = w_glu.shape[0]
    nt, _, cw = y.shape
    rows = min(ROW_TILE, t)
    row_spec = lambda c: pl.BlockSpec((rows, c), lambda i: (i, 0))
    return pl.pallas_call(
        _s5_out_kernel,
        out_shape=jax.ShapeDtypeStruct((t, d), F32),
        grid=(t // rows,),
        in_specs=[
            pl.BlockSpec((nt, rows, cw), lambda i: (0, i, 0)), row_spec(e), row_spec(d),
            _resident((SUBLANES, d), lambda i: (layer, MOD_GATE)),
            _resident((e, e), lambda i: (0, 0)),
            _resident((e, d), lambda i: (0, 0)),
        ],
        out_specs=row_spec(d),
        compiler_params=_params("parallel"),
        name="s5_out",
    )(y, z, x_tm, mod, w_glu.astype(BF16), w_out.astype(BF16))


def _gdn_in_kernel(x_ref, nw_ref, sc_ref, sh_ref, w_ref, wg_ref, cw_ref, gp_ref,
                   q_ref, k_ref, v_ref, z_ref, g_ref, ext_ref, st_ref, *, heads):
    halo = (GDN_CONV - 1) * SUBLANES
    r, d = x_ref.shape
    cc = ext_ref.shape[1]
    bsz = SUBLANES
    tmt = r // bsz
    qk = q_ref.shape[1] // bsz
    e = v_ref.shape[1] // bsz
    dk = qk // heads

    @pl.when(pl.program_id(0) == 0)
    def _():
        ext_ref[0:halo, :] = jnp.zeros((halo, cc), F32)

    h = _modulated_norm_rows(x_ref[...], nw_ref[...], sc_ref[...], sh_ref[...]).astype(BF16)

    ext_ref[halo:halo + r, :] = jnp.dot(h, w_ref[:, :cc], preferred_element_type=F32)
    conv = cw_ref[0:1, :] * ext_ref[0:r, :]
    for j in range(1, GDN_CONV):
        conv = conv + cw_ref[j:j + 1, :] * ext_ref[j * SUBLANES:j * SUBLANES + r, :]
    ext_ref[0:halo, :] = ext_ref[r:r + halo, :]
    act = _silu(conv)

    def stage(lo, val):
        for j in range(val.shape[1] // LANES):
            st_ref[lo // LANES + j] = val[:, j * LANES:(j + 1) * LANES]

    for hh in range(heads):
        qh = act[:, hh * dk:(hh + 1) * dk]
        kh = act[:, qk + hh * dk:qk + (hh + 1) * dk]
        stage(hh * dk, qh * lax.rsqrt(jnp.sum(qh * qh, axis=-1, keepdims=True) + NORM_EPS) * (dk ** -0.5))
        stage(qk + hh * dk, kh * lax.rsqrt(jnp.sum(kh * kh, axis=-1, keepdims=True) + NORM_EPS))
    stage(2 * qk, act[:, 2 * qk:])
    stage(cc, jnp.dot(h, w_ref[:, cc:], preferred_element_type=F32))

    logit = jnp.dot(h, wg_ref[...], preferred_element_type=F32)
    beta = _sigmoid(logit)
    g = -jnp.exp(gp_ref[0:1, :]) * _softplus(logit + gp_ref[1:2, :])
    lane = lax.broadcasted_iota(jnp.int32, logit.shape, 1)
    stage(cc + e, jnp.where(lane < heads, beta, g))

    for b in range(bsz):
        seq = lambda lo, hi: _sequence_rows(st_ref, b, tmt, lo, hi)
        q_ref[:, b * qk:(b + 1) * qk] = seq(0, qk).astype(BF16)
        k_ref[:, b * qk:(b + 1) * qk] = seq(qk, 2 * qk).astype(BF16)
        v_ref[:, b * e:(b + 1) * e] = seq(2 * qk, cc).astype(BF16)
        z_ref[:, b * e:(b + 1) * e] = seq(cc, cc + e).astype(BF16)
        g_ref[:, b * LANES:(b + 1) * LANES] = seq(cc + e, cc + e + LANES)


def _gdn_in(x_tm, norm_w, mod, layer, w_in, conv_w, a_log, dt_bias):
    t, d = x_tm.shape
    bsz = SUBLANES
    seqlen = t // bsz
    heads = a_log.shape[0]
    cc = conv_w.shape[1]
    e = (w_in.shape[1] - cc - 2 * heads)
    qk = (cc - e) // 2
    rows = min(GDN_IN_ROWS, t)
    tmt = rows // bsz
    w_main = w_in[:, :cc + e].astype(BF16)
    w_gate = jnp.pad(w_in[:, cc + e:], ((0, 0), (0, LANES - 2 * heads))).astype(BF16)
    gate_par = jnp.pad(jnp.stack([a_log, dt_bias]), ((0, 0), (heads, LANES - 2 * heads)))
    seq_spec = lambda c: pl.BlockSpec((tmt, bsz * c), lambda i: (i, 0))
    return pl.pallas_call(
        functools.partial(_gdn_in_kernel, heads=heads),
        out_shape=[
            jax.ShapeDtypeStruct((seqlen, bsz * qk), BF16),
            jax.ShapeDtypeStruct((seqlen, bsz * qk), BF16),
            jax.ShapeDtypeStruct((seqlen, bsz * e), BF16),
            jax.ShapeDtypeStruct((seqlen, bsz * e), BF16),
            jax.ShapeDtypeStruct((seqlen, bsz * LANES), F32),
        ],
        grid=(t // rows,),
        in_specs=[
            pl.BlockSpec((rows, d), lambda i: (i, 0)),
            _resident((1, d), lambda i: (0, 0)),
            _resident((SUBLANES, d), lambda i: (layer, MOD_SCALE)),
            _resident((SUBLANES, d), lambda i: (layer, MOD_SHIFT)),
            _resident((d, cc + e), lambda i: (0, 0)),
            _resident((d, LANES), lambda i: (0, 0)),
            _resident((GDN_CONV, cc), lambda i: (0, 0)),
            _resident((2, LANES), lambda i: (0, 0)),
        ],
        out_specs=[seq_spec(qk), seq_spec(qk), seq_spec(e), seq_spec(e), seq_spec(LANES)],
        scratch_shapes=[
            pltpu.VMEM((rows + (GDN_CONV - 1) * SUBLANES, cc), F32),
            pltpu.VMEM(((cc + e + LANES) // LANES, rows, LANES), F32),
        ],
        compiler_params=_params("arbitrary"),
        name="gdn_in",
    )(x_tm, norm_w.reshape(1, d), mod, mod, w_main, w_gate, conv_w, gate_par)


def _pair_block_diag(x):
    c = x.shape[0]
    lane = lax.broadcasted_iota(jnp.int32, x.shape, 1)
    zero = jnp.zeros_like(x)
    return jnp.concatenate([jnp.where(lane < c, x, zero), jnp.where(lane >= c, x, zero)], axis=0)


def _pair_mm(x, y):
    return jnp.dot(x.astype(BF16), _pair_block_diag(y.astype(BF16)), preferred_element_type=F32)


def _gdn_core_kernel(q_ref, k_ref, v_ref, z_ref, g_ref, nw_ref, y_ref, s_ref, *, heads, nb):
    @pl.when(pl.program_id(1) == 0)
    def _():
        s_ref[...] = jnp.zeros_like(s_ref)

    c = q_ref.shape[0]
    qk_w = q_ref.shape[1] // nb
    e_w = v_ref.shape[1] // nb
    dk = qk_w // heads
    dv = e_w // heads
    units = [(b, p) for b in range(nb) for p in range(heads // 2)]

    row = lax.broadcasted_iota(jnp.int32, (c, 2 * c), 0)
    col = lax.broadcasted_iota(jnp.int32, (c, 2 * c), 1) & (c - 1)
    causal = row >= col
    strict = row > col
    eye = (row == col).astype(F32)
    row1 = lax.broadcasted_iota(jnp.int32, (c, c), 0)
    col1 = lax.broadcasted_iota(jnp.int32, (c, c), 1)
    tri = (row1 >= col1).astype(BF16)
    nh2 = 2 * heads
    eye_h = (lax.broadcasted_iota(jnp.int32, (nh2, nh2), 0)
             == lax.broadcasted_iota(jnp.int32, (nh2, nh2), 1)).astype(BF16)

    def same_block(shift):
        return (row >> shift) == (col >> shift)

    gb, gc, gc_t = [], [], []
    for b in range(nb):
        g = g_ref[:, b * LANES:b * LANES + nh2]
        cs = sum(jnp.dot(tri, part, preferred_element_type=F32) for part in _split3(g))
        gb.append(g)
        gc.append(cs)
        gc_t.append(sum(lax.dot_general(eye_h, part, (((1,), (1,)), ((), ())), preferred_element_type=F32)
                        for part in _split3(cs)))

    def lanes2(ref, b, p, width, per_batch):
        return ref[:, b * per_batch + 2 * p * width:b * per_batch + 2 * (p + 1) * width]

    def pair_cols(b, p, fn, width):
        return jnp.concatenate([jnp.broadcast_to(fn(b, 2 * p + i), (c, width)) for i in range(2)], axis=1)

    beta_c = lambda b, h: gb[b][:, h:h + 1]
    gcol = lambda b, h: gc[b][:, heads + h:heads + h + 1]
    grow = lambda b, h: gc_t[b][heads + h:heads + h + 1, :]

    a_mat, qk_mat = {}, {}
    for (b, p) in units:
        qp = lanes2(q_ref, b, p, dk, qk_w)
        kp = lanes2(k_ref, b, p, dk, qk_w)
        res = lax.dot_general(jnp.concatenate([qp, kp], axis=0), _pair_block_diag(kp),
                              (((1,), (1,)), ((), ())), preferred_element_type=F32)
        decay = jnp.where(causal, jnp.exp(jnp.concatenate(
            [gcol(b, 2 * p + i) - grow(b, 2 * p + i) for i in range(2)], axis=1)), 0.0)
        qk_mat[b, p] = res[:c] * decay
        a_mat[b, p] = jnp.where(strict, pair_cols(b, p, beta_c, c) * res[c:] * decay, 0.0)

    shift0 = GDN_BASE_BLOCK.bit_length() - 1
    dblk = {u: jnp.where(same_block(shift0), a_mat[u], 0.0) for u in units}
    d2 = {u: _pair_mm(dblk[u], dblk[u]) for u in units}
    d4 = {u: _pair_mm(d2[u], d2[u]) for u in units}
    t_inv = {u: eye - dblk[u] for u in units}
    for pw in (d2, d4):
        upd = {u: _pair_mm(t_inv[u], pw[u]) for u in units}
        t_inv = {u: t_inv[u] + upd[u] for u in units}
    shift = shift0
    while (1 << shift) < c:
        level = same_block(shift + 1) & jnp.logical_not(same_block(shift))
        xt = {u: _pair_mm(jnp.where(level, a_mat[u], 0.0), t_inv[u]) for u in units}
        upd = {u: _pair_mm(t_inv[u], xt[u]) for u in units}
        t_inv = {u: t_inv[u] - upd[u] for u in units}
        shift += 1

    w_pk, u_pk, gam_pk = {}, {}, {}
    for (b, p) in units:
        gam_pk[b, p] = jnp.exp(pair_cols(b, p, gcol, dk))
        kp = lanes2(k_ref, b, p, dk, qk_w).astype(F32)
        vp = lanes2(v_ref, b, p, dv, e_w).astype(F32)
        rw = (pair_cols(b, p, beta_c, dk) * gam_pk[b, p] * kp).astype(BF16)
        ru = (pair_cols(b, p, beta_c, dv) * vp).astype(BF16)
        zk = jnp.zeros((c, dk), BF16)
        zv = jnp.zeros((c, dv), BF16)
        rhs = jnp.concatenate([
            jnp.concatenate([rw[:, :dk], zk, ru[:, :dv], zv], axis=1),
            jnp.concatenate([zk, rw[:, dk:], zv, ru[:, dv:]], axis=1)], axis=0)
        wu = jnp.dot(t_inv[b, p].astype(BF16), rhs, preferred_element_type=F32)
        w_pk[b, p] = wu[:, :2 * dk]
        u_pk[b, p] = wu[:, 2 * dk:]

    hunits = [(b, h) for b in range(nb) for h in range(heads)]
    ws, states = {}, {}
    for (b, h) in hunits:
        p, i = divmod(h, 2)
        qd = lanes2(q_ref, b, p, dk, qk_w).astype(F32) * gam_pk[b, p]
        lhs = jnp.concatenate([w_pk[b, p][:, i * dk:(i + 1) * dk], qd[:, i * dk:(i + 1) * dk]], axis=0)
        states[b, h] = s_ref[b * heads + h]
        ws[b, h] = _bdot(lhs, states[b, h])
    v_new = {(b, h): u_pk[b, h // 2][:, (h % 2) * dv:(h % 2 + 1) * dv] - ws[b, h][:c] for (b, h) in hunits}
    o_intra = {(b, h): _bdot(qk_mat[b, h // 2][:, (h % 2) * c:(h % 2 + 1) * c], v_new[b, h]) for (b, h) in hunits}
    for (b, h) in hunits:
        g_last = gcol(b, h)[c - 1:c, :]
        kh = k_ref[:, b * qk_w + h * dk:b * qk_w + (h + 1) * dk].astype(F32)
        k_dec = kh * jnp.exp(g_last - gcol(b, h))
        s_ref[b * heads + h] = jnp.exp(g_last) * states[b, h] + _bdot_tn(k_dec, v_new[b, h])
    for (b, h) in hunits:
        o = ws[b, h][c:] + o_intra[b, h]
        on = _rms(o) * nw_ref[...]
        zh = z_ref[:, b * e_w + h * dv:b * e_w + (h + 1) * dv].astype(F32)
        y_ref[:, b * e_w + h * dv:b * e_w + (h + 1) * dv] = (on * _silu(zh)).astype(BF16)


def _gdn_core(q, k, v, z, gates, norm_w, heads):
    seqlen = q.shape[0]
    nb = GDN_BATCH_PER_STEP
    bsz = gates.shape[1] // LANES
    qk = q.shape[1] // bsz
    e = v.shape[1] // bsz
    c = min(GDN_CHUNK, seqlen)
    col_spec = lambda w: pl.BlockSpec((c, nb * w), lambda b, n: (n, b))
    return pl.pallas_call(
        functools.partial(_gdn_core_kernel, heads=heads, nb=nb),
        out_shape=jax.ShapeDtypeStruct((seqlen, bsz * e), BF16),
        grid=(bsz // nb, seqlen // c),
        in_specs=[col_spec(qk), col_spec(qk), col_spec(e), col_spec(e), col_spec(LANES),
                  _resident((1, e // heads), lambda b, n: (0, 0))],
        out_specs=col_spec(e),
        scratch_shapes=[pltpu.VMEM((nb * heads, qk // heads, e // heads), F32)],
        compiler_params=_params("parallel", "arbitrary"),
        name="gdn_core",
    )(q, k, v, z, gates, norm_w.reshape(1, e // heads))


def _gdn_out_kernel(y_ref, x_ref, gate_ref, w_ref, fw_ref, o_ref, slab_ref):
    bsz, tmt, d = o_ref.shape
    e = y_ref.shape[1] // bsz
    for j in range(d // LANES):
        slab_ref[j] = x_ref[:, j * LANES:(j + 1) * LANES]
    y = jnp.concatenate([y_ref[:, b * e:(b + 1) * e] for b in range(bsz)], axis=0)
    out = jnp.dot(y, w_ref[...], preferred_element_type=F32).reshape(bsz, tmt, d)
    for b in range(bsz):
        xn = _sequence_rows(slab_ref, b, tmt, 0, d) + gate_ref[b:b + 1, :] * out[b]
        o_ref[b] = _rms(xn) * fw_ref[...]


def _gdn_out(y, x_tm, mod, layer, w_out, final_w):
    e, d = w_out.shape
    seqlen = y.shape[0]
    bsz = y.shape[1] // e
    tmt = min(ROW_TILE // bsz, seqlen)
    return pl.pallas_call(
        _gdn_out_kernel,
        out_shape=jax.ShapeDtypeStruct((bsz, seqlen, d), F32),
        grid=(seqlen // tmt,),
        in_specs=[
            pl.BlockSpec((tmt, bsz * e), lambda i: (i, 0)),
            pl.BlockSpec((tmt * bsz, d), lambda i: (i, 0)),
            _resident((bsz, d), lambda i: (layer, MOD_GATE)),
            _resident((e, d), lambda i: (0, 0)),
            _resident((1, d), lambda i: (0, 0)),
        ],
        out_specs=pl.BlockSpec((bsz, tmt, d), lambda i: (0, i, 0)),
        scratch_shapes=[pltpu.VMEM((d // LANES, tmt * bsz, LANES), F32)],
        compiler_params=_params("parallel"),
        name="gdn_out",
    )(y, x_tm, mod, w_out.astype(BF16), final_w.reshape(1, d))


def kernel(x, c, ada_w, ada_b, norm_w, s5_w_in, s5_lambda_re, s5_lambda_im, s5_log_dt, s5_b_re, s5_b_im,
           s5_c_re, s5_c_im, s5_d, s5_w_glu, s5_w_out, gdn_w_in, gdn_conv_w, gdn_a_log, gdn_dt_bias,
           gdn_norm_w, gdn_w_out, final_norm_w):
    bsz, seqlen, d = x.shape
    assert bsz == SUBLANES, "the time-major layout keeps one batch row per vector sublane"
    assert ada_w.shape[0] == 2 and s5_w_in.shape[0] == 1 and gdn_w_in.shape[0] == 1

    mod = _modulation(c, ada_w, ada_b)

    w_b, w_i, w_c, aq = _s5_weights(s5_lambda_re[0], s5_lambda_im[0], s5_log_dt[0], s5_b_re[0], s5_b_im[0],
                                    s5_c_re[0], s5_c_im[0])
    x_tm, u, z = _s5_in(x, norm_w[0], mod, 0, s5_w_in[0])
    runs = min(S5_RUNS, seqlen // S5_RUN)
    chunks = min(S5_CHUNKS, seqlen // (S5_RUN * runs))
    y = _s5_scan(u, w_b, w_i, w_c, aq, s5_d[0], runs, chunks)
    x1_tm = _s5_out(y, z, x_tm, mod, 0, s5_w_glu[0], s5_w_out[0])

    heads = gdn_a_log.shape[1]
    q, k, v, zg, gates = _gdn_in(x1_tm, norm_w[1], mod, 1, gdn_w_in[0], gdn_conv_w[0],
                                 gdn_a_log[0], gdn_dt_bias[0])
    yg = _gdn_core(q, k, v, zg, gates, gdn_norm_w[0], heads)
    return _gdn_out(yg, x1_tm, mod, 1, gdn_w_out[0], final_norm_w)
```

```python
import functools
import math

import jax
import jax.numpy as jnp
from jax import lax
from jax.experimental import pallas as pl
from jax.experimental.pallas import tpu as pltpu

F32 = jnp.float32
BF16 = jnp.bfloat16

NORM_EPS = 1e-6
S5_GROUP = 16
S5_STATE = 64
S5_TILE_GROUPS = 16
S5_RUN = 4
S5_BLOCK_GROUPS = S5_TILE_GROUPS // S5_RUN
S5_TILE_BLOCKS = S5_TILE_GROUPS // S5_BLOCK_GROUPS
S5_RUNS = 64
S5_CHUNKS = 4
GDN_HEADS = 8
GDN_CONV = 4
GDN_CHUNK = 128
GDN_BASE_BLOCK = 8
GDN_BATCH_PER_STEP = 4
ROW_TILE = 512
GDN_IN_ROWS = 256
GDN_OUT_ROWS = 1024
SUBLANES = 8
LANES = 128
VMEM_LIMIT_BYTES = 56 * 1024 * 1024


def _sigmoid(x):
    return 1.0 / (1.0 + jnp.exp(-x))


def _silu(x):
    return x * _sigmoid(x)


def _gelu_tanh(x):
    c = math.sqrt(2.0 / math.pi)
    return 0.5 * x * (1.0 + jnp.tanh(c * (x + 0.044715 * (x * x * x))))


def _softplus(x):
    return jnp.maximum(x, 0.0) + jnp.log(1.0 + jnp.exp(-jnp.abs(x)))


def _rms(x):
    return x * lax.rsqrt(jnp.mean(x * x, axis=-1, keepdims=True) + NORM_EPS)


def _bdot(a, b):
    return jnp.dot(a.astype(BF16), b.astype(BF16), preferred_element_type=F32)


def _bdot_nt(a, b):
    return lax.dot_general(a.astype(BF16), b.astype(BF16), (((1,), (1,)), ((), ())),
                           preferred_element_type=F32)


def _bdot_tn(a, b):
    return lax.dot_general(a.astype(BF16), b.astype(BF16), (((0,), (0,)), ((), ())),
                           preferred_element_type=F32)


def _split3(x):
    hi = x.astype(BF16)
    r = x - hi.astype(F32)
    mid = r.astype(BF16)
    lo = (r - mid.astype(F32)).astype(BF16)
    return hi, mid, lo


def _params(*sem):
    return pltpu.CompilerParams(dimension_semantics=sem, vmem_limit_bytes=VMEM_LIMIT_BYTES)


def _resident(shape, index_map):
    return pl.BlockSpec(shape, index_map, pipeline_mode=pl.Buffered(1))


MOD_SHIFT, MOD_SCALE, MOD_GATE = 0, 1, 2


def _mod_kernel(c_ref, w_ref, b_ref, o_ref):
    o_ref[...] = _bdot(_silu(c_ref[...]), w_ref[0]) + b_ref[0]


def _modulation(c, ada_w, ada_b):
    depth, d, d3 = ada_w.shape
    bsz = c.shape[0]
    nb = d3 // d
    return pl.pallas_call(
        _mod_kernel,
        out_shape=jax.ShapeDtypeStruct((depth * bsz, d3), F32),
        grid=(depth, nb),
        in_specs=[
            pl.BlockSpec((bsz, d), lambda l, j: (0, 0)),
            pl.BlockSpec((1, d, d), lambda l, j: (l, 0, j)),
            pl.BlockSpec((1, 1, d), lambda l, j: (l, 0, j)),
        ],
        out_specs=pl.BlockSpec((bsz, d), lambda l, j: (l, j)),
        compiler_params=_params("parallel", "parallel"),
        name="adaln_mod",
    )(c, ada_w.astype(BF16), ada_b.reshape(depth, 1, d3))


def _cmul(xr, xi, yr, yi):
    return xr * yr - xi * yi, xr * yi + xi * yr


def _place_nt(sel, vals):
    return sum(lax.dot_general(sel, part, (((1,), (1,)), ((), ())), preferred_element_type=F32)
               for part in _split3(vals))


def _s5_prep_kernel(lre_ref, lim_ref, ldt_ref, bre_ref, bim_ref, cre_ref, cim_ref, wb_ref, wi_ref, wc_ref, aq_ref,
                    *, q):
    tg, p, m = bre_ref.shape
    ns, nc = tg * p, tg * m
    lre, lim = lre_ref[0], lim_ref[0]
    dt = jnp.exp(ldt_ref[0])
    mag = jnp.exp(lre * dt)
    a_re = mag * jnp.cos(lim * dt)
    a_im = mag * jnp.sin(lim * dt)
    den = lre * lre + lim * lim
    nr = a_re - 1.0
    q_re = (nr * lre + a_im * lim) / den
    q_im = (a_im * lre - nr * lim) / den

    def iota(shape, axis):
        return lax.broadcasted_iota(jnp.int32, shape, axis)

    sel_n = (iota((nc, m), 0) % m == iota((nc, m), 1)).astype(BF16)
    zb_re = _place_nt(sel_n, bre_ref[...].reshape(ns, m))
    zb_im = _place_nt(sel_n, bim_ref[...].reshape(ns, m))
    same_in = iota((nc, ns), 0) // m == iota((nc, ns), 1) // p
    bb_re, bb_im = _cmul(q_re, q_im, zb_re, zb_im)
    bb = [(jnp.where(same_in, bb_re, 0.0), jnp.where(same_in, bb_im, 0.0))]
    pw_re, pw_im = a_re, a_im
    for _ in range(1, q):
        bb.append(_cmul(pw_re, pw_im, *bb[0]))
        pw_re, pw_im = _cmul(pw_re, pw_im, a_re, a_im)
    aq_ref[0] = jnp.broadcast_to(jnp.concatenate([pw_re, pw_im], axis=1), aq_ref.shape[1:])
    wb = jnp.concatenate([jnp.concatenate(bb[q - 1 - s], axis=1) for s in range(q)], axis=0).astype(BF16)
    wb_ref[0] = wb

    sel_p = (iota((ns, p), 0) % p == iota((ns, p), 1)).astype(BF16)
    same_out = iota((ns, nc), 0) // p == iota((ns, nc), 1) // m
    c_re = jnp.where(same_out, _place_nt(sel_p, cre_ref[...].reshape(nc, p)), 0.0)
    c_im = jnp.where(same_out, _place_nt(sel_p, cim_ref[...].reshape(nc, p)), 0.0)
    eye_s = (iota((ns, ns), 0) == iota((ns, ns), 1)).astype(BF16)
    col_re = _place_nt(eye_s, jnp.broadcast_to(a_re, (nc, ns)))
    col_im = _place_nt(eye_s, jnp.broadcast_to(a_im, (nc, ns)))
    ca = [(c_re, c_im)]
    for _ in range(q):
        ca.append(_cmul(*ca[-1], col_re, col_im))
    wc_ref[0] = jnp.concatenate([jnp.concatenate([ca[k][0] for k in range(1, q + 1)], axis=1),
                                 jnp.concatenate([-ca[k][1] for k in range(1, q + 1)], axis=1)], axis=0).astype(BF16)

    wc0 = jnp.concatenate([c_re, -c_im], axis=0).astype(BF16)
    kd = [jnp.dot(wb[(q - 1 - d) * nc:(q - d) * nc], wc0, preferred_element_type=F32) for d in range(q)]
    zero = jnp.zeros_like(kd[0])
    wi_ref[0] = jnp.concatenate(
        [jnp.concatenate([kd[so - si] if so >= si else zero for so in range(q)], axis=1) for si in range(q)],
        axis=0).astype(BF16)


def _s5_weights(lam_re, lam_im, log_dt, b_re, b_im, c_re, c_im):
    g, p = lam_re.shape
    m = b_re.shape[-1]
    q = S5_RUN
    tg = S5_BLOCK_GROUPS
    nb = g // tg
    ns, nc = tg * p, tg * m
    row = lambda a: a.reshape(nb, 1, ns)
    row_spec = pl.BlockSpec((1, 1, ns), lambda i: (i, 0, 0))
    w_b, w_i, w_c, aq = pl.pallas_call(
        functools.partial(_s5_prep_kernel, q=q),
        out_shape=[jax.ShapeDtypeStruct((nb, q * nc, 2 * ns), BF16),
                   jax.ShapeDtypeStruct((nb, q * nc, q * nc), BF16),
                   jax.ShapeDtypeStruct((nb, 2 * ns, q * nc), BF16),
                   jax.ShapeDtypeStruct((nb, SUBLANES, 2 * ns), F32)],
        grid=(nb,),
        in_specs=[row_spec, row_spec, row_spec,
                  pl.BlockSpec((tg, p, m), lambda i: (i, 0, 0)), pl.BlockSpec((tg, p, m), lambda i: (i, 0, 0)),
                  pl.BlockSpec((tg, m, p), lambda i: (i, 0, 0)), pl.BlockSpec((tg, m, p), lambda i: (i, 0, 0))],
        out_specs=[pl.BlockSpec((1, q * nc, 2 * ns), lambda i: (i, 0, 0)),
                   pl.BlockSpec((1, q * nc, q * nc), lambda i: (i, 0, 0)),
                   pl.BlockSpec((1, 2 * ns, q * nc), lambda i: (i, 0, 0)),
                   pl.BlockSpec((1, SUBLANES, 2 * ns), lambda i: (i, 0, 0))],
        compiler_params=_params("parallel"),
        name="s5_prep",
    )(row(lam_re), row(lam_im), row(jnp.broadcast_to(log_dt[:, None], (g, p))), b_re, b_im, c_re, c_im)
    per_tile = lambda w: w.reshape((nb // S5_TILE_BLOCKS, S5_TILE_BLOCKS) + w.shape[1:])
    return per_tile(w_b), per_tile(w_i), per_tile(w_c), per_tile(aq)


def _modulated_norm_rows(x, nw, scale, shift):
    r, d = x.shape
    y = (_rms(x) * nw).reshape(r // SUBLANES, SUBLANES, d)
    return (y * (1.0 + scale)[None] + shift[None]).reshape(r, d)


def _sequence_rows(slab_ref, b, tmt, lo, hi):
    return jnp.concatenate([slab_ref[j, pl.ds(b, tmt, stride=SUBLANES), :]
                            for j in range(lo // LANES, hi // LANES)], axis=1)


def _s5_in_kernel(x_ref, nw_ref, sc_ref, sh_ref, w_ref, xt_ref, u_ref, z_ref, slab_ref):
    bsz, tmt, d = x_ref.shape
    for j in range(d // LANES):
        for b in range(bsz):
            slab_ref[j, pl.ds(b, tmt, stride=bsz), :] = x_ref[b, :, j * LANES:(j + 1) * LANES]
    x_tm = jnp.concatenate([slab_ref[j] for j in range(d // LANES)], axis=1)
    xt_ref[...] = x_tm
    h = _modulated_norm_rows(x_tm, nw_ref[...], sc_ref[...], sh_ref[...]).astype(BF16)
    e = z_ref.shape[1]
    nt, _, cw = u_ref.shape
    u = jnp.dot(h, w_ref[:, :e], preferred_element_type=F32).astype(BF16)
    for k in range(nt):
        u_ref[k] = u[:, k * cw:(k + 1) * cw]
    z_ref[...] = jnp.dot(h, w_ref[:, e:], preferred_element_type=F32).astype(BF16)


def _s5_in(x, norm_w, mod, layer, w_in):
    bsz, seqlen, d = x.shape
    e = w_in.shape[1] // 2
    tmt = min(ROW_TILE // bsz, seqlen)
    rows = tmt * bsz
    t = seqlen * bsz
    cw = S5_TILE_GROUPS * S5_GROUP
    nt = e // cw
    return pl.pallas_call(
        _s5_in_kernel,
        out_shape=[jax.ShapeDtypeStruct((t, d), F32), jax.ShapeDtypeStruct((nt, t, cw), BF16),
                   jax.ShapeDtypeStruct((t, e), BF16)],
        grid=(seqlen // tmt,),
        in_specs=[
            pl.BlockSpec((bsz, tmt, d), lambda i: (0, i, 0)),
            _resident((1, d), lambda i: (0, 0)),
            _resident((bsz, d), lambda i: (layer, MOD_SCALE)),
            _resident((bsz, d), lambda i: (layer, MOD_SHIFT)),
            _resident((d, 2 * e), lambda i: (0, 0)),
        ],
        out_specs=[pl.BlockSpec((rows, d), lambda i: (i, 0)), pl.BlockSpec((nt, rows, cw), lambda i: (0, i, 0)),
                   pl.BlockSpec((rows, e), lambda i: (i, 0))],
        scratch_shapes=[pltpu.VMEM((d // LANES, rows, LANES), F32)],
        compiler_params=_params("parallel"),
        name="s5_in",
    )(x, norm_w.reshape(1, d), mod, mod, w_in.astype(BF16))


def _s5_scan_kernel(u_ref, wb_ref, wi_ref, wc_ref, a_ref, d_ref, y_ref, sin_ref, xs_ref, st_ref, *, runs, chunks, q):
    @pl.when(pl.program_id(1) == 0)
    def _():
        st_ref[...] = jnp.zeros_like(st_ref)

    nblk = wb_ref.shape[1]
    bw = wi_ref.shape[2] // q
    half = wb_ref.shape[3] // 2
    nl = half // LANES
    rows_c = runs * SUBLANES
    a_re = [[a_ref[0, k, :, j * LANES:(j + 1) * LANES] for j in range(nl)] for k in range(nblk)]
    a_im = [[a_ref[0, k, :, half + j * LANES:half + (j + 1) * LANES] for j in range(nl)] for k in range(nblk)]
    x_re = [[st_ref[k, 0, :, j * LANES:(j + 1) * LANES] for j in range(nl)] for k in range(nblk)]
    x_im = [[st_ref[k, 1, :, j * LANES:(j + 1) * LANES] for j in range(nl)] for k in range(nblk)]
    intra = {}

    def project_in(ci):
        u = u_ref[0, ci * q * rows_c:(ci + 1) * q * rows_c, :].astype(F32).reshape(runs, q, SUBLANES, nblk * bw)
        us = [u[:, s].reshape(rows_c, nblk * bw) for s in range(q)]
        slot = ci % 2
        for k in range(nblk):
            uu = jnp.concatenate([us[s][:, k * bw:(k + 1) * bw] for s in range(q)], axis=1).astype(BF16)
            sin_ref[slot, k] = jnp.dot(uu, wb_ref[0, k], preferred_element_type=F32)
            intra[ci, k] = jnp.dot(uu, wi_ref[0, k], preferred_element_type=F32)

    def recur(ci):
        slot = ci % 2
        for pp in range(runs // 2):
            r0 = 2 * pp * SUBLANES
            for k in range(nblk):
                for j in range(nl):
                    lo, hi = j * LANES, (j + 1) * LANES
                    before_re, before_im = [], []
                    for s in range(2):
                        rr = r0 + s * SUBLANES
                        before_re.append(x_re[k][j])
                        before_im.append(x_im[k][j])
                        b_re = sin_ref[slot, k, rr:rr + SUBLANES, lo:hi]
                        b_im = sin_ref[slot, k, rr:rr + SUBLANES, half + lo:half + hi]
                        v_re = a_re[k][j] * x_re[k][j] - a_im[k][j] * x_im[k][j] + b_re
                        v_im = a_re[k][j] * x_im[k][j] + a_im[k][j] * x_re[k][j] + b_im
                        x_re[k][j], x_im[k][j] = v_re, v_im
                    xs_ref[slot, k, r0:r0 + 2 * SUBLANES, lo:hi] = jnp.concatenate(before_re, axis=0).astype(BF16)
                    xs_ref[slot, k, r0:r0 + 2 * SUBLANES, half + lo:half + hi] = (
                        jnp.concatenate(before_im, axis=0).astype(BF16))

    def project_out(ci):
        slot = ci % 2
        yk = [intra.pop((ci, k)) + jnp.dot(xs_ref[slot, k], wc_ref[0, k], preferred_element_type=F32)
              for k in range(nblk)]
        steps = [jnp.concatenate([yk[k][:, s * bw:(s + 1) * bw] for k in range(nblk)], axis=1)
                 .reshape(runs, 1, SUBLANES, nblk * bw) for s in range(q)]
        y = jnp.concatenate(steps, axis=1).reshape(q * rows_c, nblk * bw)
        u = u_ref[0, ci * q * rows_c:(ci + 1) * q * rows_c, :].astype(F32)
        y_ref[0, ci * q * rows_c:(ci + 1) * q * rows_c, :] = _gelu_tanh(y + d_ref[0] * u).astype(BF16)

    project_in(0)
    for ci in range(chunks):
        if ci + 1 < chunks:
            project_in(ci + 1)
        recur(ci)
        project_out(ci)

    for k in range(nblk):
        for j in range(nl):
            st_ref[k, 0, :, j * LANES:(j + 1) * LANES] = x_re[k][j]
            st_ref[k, 1, :, j * LANES:(j + 1) * LANES] = x_im[k][j]


def _s5_scan(u_tiles, w_b, w_i, w_c, aq, d_skip, runs, chunks):
    nt, nblk, kw, sw = w_b.shape
    q = S5_RUN
    cw = u_tiles.shape[2]
    t = u_tiles.shape[1]
    rows_c = runs * SUBLANES
    rows = q * rows_c * chunks
    return pl.pallas_call(
        functools.partial(_s5_scan_kernel, runs=runs, chunks=chunks, q=q),
        out_shape=jax.ShapeDtypeStruct((nt, t, cw), BF16),
        grid=(nt, t // rows),
        in_specs=[
            pl.BlockSpec((1, rows, cw), lambda k, c: (k, c, 0)),
            _resident((1, nblk, kw, sw), lambda k, c: (k, 0, 0, 0)),
            _resident((1, nblk, kw, kw), lambda k, c: (k, 0, 0, 0)),
            _resident((1, nblk, sw, kw), lambda k, c: (k, 0, 0, 0)),
            _resident((1, nblk, SUBLANES, sw), lambda k, c: (k, 0, 0, 0)),
            _resident((1, 1, cw), lambda k, c: (k, 0, 0)),
        ],
        out_specs=pl.BlockSpec((1, rows, cw), lambda k, c: (k, c, 0)),
        scratch_shapes=[
            pltpu.VMEM((2, nblk, rows_c, sw), F32),
            pltpu.VMEM((2, nblk, rows_c, sw), BF16),
            pltpu.VMEM((nblk, 2, SUBLANES, sw // 2), F32),
        ],
        compiler_params=_params("parallel", "arbitrary"),
        name="s5_scan",
    )(u_tiles, w_b, w_i, w_c, aq, d_skip.reshape(nt, 1, cw))


def _s5_out_kernel(y_ref, z_ref, x_ref, gate_ref, wg_ref, wo_ref, o_ref):
    y = jnp.concatenate([y_ref[k] for k in range(y_ref.shape[0])], axis=1)
    yf = y.astype(F32)
    y2 = yf * _sigmoid(jnp.dot(y, wg_ref[...], preferred_element_type=F32)) * _silu(z_ref[...].astype(F32))
    out = jnp.dot(y2.astype(BF16), wo_ref[...], preferred_element_type=F32)
    r, d = out.shape
    x = x_ref[...].reshape(r // SUBLANES, SUBLANES, d)
    o_ref[...] = (x + gate_ref[...][None] * out.reshape(r // SUBLANES, SUBLANES, d)).reshape(r, d)


def _s5_out(y, z, x_tm, mod, layer, w_glu, w_out):
    t, d = x_tm.shape
    e = w_glu.shape[0]
    nt, _, cw = y.shape
    rows = min(ROW_TILE, t)
    row_spec = lambda c: pl.BlockSpec((rows, c), lambda i: (i, 0))
    return pl.pallas_call(
        _s5_out_kernel,
        out_shape=jax.ShapeDtypeStruct((t, d), F32),
        grid=(t // rows,),
        in_specs=[
            pl.BlockSpec((nt, rows, cw), lambda i: (0, i, 0)), row_spec(e), row_spec(d),
            _resident((SUBLANES, d), lambda i: (layer, MOD_GATE)),
            _resident((e, e), lambda i: (0, 0)),
            _resident((e, d), lambda i: (0, 0)),
        ],
        out_specs=row_spec(d),
        compiler_params=_params("parallel"),
        name="s5_out",
    )(y, z, x_tm, mod, w_glu.astype(BF16), w_out.astype(BF16))


def _gdn_in_kernel(x_ref, nw_ref, sc_ref, sh_ref, w_ref, wg_ref, cw_ref, gp_ref,
                   q_ref, k_ref, v_ref, z_ref, g_ref, ext_ref, st_ref, *, heads):
    halo = (GDN_CONV - 1) * SUBLANES
    r, d = x_ref.shape
    cc = ext_ref.shape[1]
    bsz = SUBLANES
    tmt = r // bsz
    qk = q_ref.shape[1] // bsz
    e = v_ref.shape[1] // bsz
    dk = qk // heads

    @pl.when(pl.program_id(0) == 0)
    def _():
        ext_ref[0:halo, :] = jnp.zeros((halo, cc), F32)

    h = _modulated_norm_rows(x_ref[...], nw_ref[...], sc_ref[...], sh_ref[...]).astype(BF16)

    ext_ref[halo:halo + r, :] = jnp.dot(h, w_ref[:, :cc], preferred_element_type=F32)
    conv = cw_ref[0:1, :] * ext_ref[0:r, :]
    for j in range(1, GDN_CONV):
        conv = conv + cw_ref[j:j + 1, :] * ext_ref[j * SUBLANES:j * SUBLANES + r, :]
    ext_ref[0:halo, :] = ext_ref[r:r + halo, :]
    act = _silu(conv)

    def stage(lo, val):
        for j in range(val.shape[1] // LANES):
            st_ref[lo // LANES + j] = val[:, j * LANES:(j + 1) * LANES]

    for hh in range(heads):
        qh = act[:, hh * dk:(hh + 1) * dk]
        kh = act[:, qk + hh * dk:qk + (hh + 1) * dk]
        stage(hh * dk, qh * lax.rsqrt(jnp.sum(qh * qh, axis=-1, keepdims=True) + NORM_EPS) * (dk ** -0.5))
        stage(qk + hh * dk, kh * lax.rsqrt(jnp.sum(kh * kh, axis=-1, keepdims=True) + NORM_EPS))
    stage(2 * qk, act[:, 2 * qk:])
    stage(cc, jnp.dot(h, w_ref[:, cc:], preferred_element_type=F32))

    logit = jnp.dot(h, wg_ref[...], preferred_element_type=F32)
    beta = _sigmoid(logit)
    g = -jnp.exp(gp_ref[0:1, :]) * _softplus(logit + gp_ref[1:2, :])
    lane = lax.broadcasted_iota(jnp.int32, logit.shape, 1)
    stage(cc + e, jnp.where(lane < heads, beta, g))

    for b in range(bsz):
        seq = lambda lo, hi: _sequence_rows(st_ref, b, tmt, lo, hi)
        q_ref[:, b * qk:(b + 1) * qk] = seq(0, qk).astype(BF16)
        k_ref[:, b * qk:(b + 1) * qk] = seq(qk, 2 * qk).astype(BF16)
        v_ref[:, b * e:(b + 1) * e] = seq(2 * qk, cc).astype(BF16)
        z_ref[:, b * e:(b + 1) * e] = seq(cc, cc + e).astype(BF16)
        g_ref[:, b * LANES:(b + 1) * LANES] = seq(cc + e, cc + e + LANES)


def _gdn_cast_kernel(w_ref, main_ref, gate_ref, *, n_main, tail):
    j = pl.program_id(0)

    @pl.when(j < n_main)
    def _():
        main_ref[...] = w_ref[...].astype(BF16)

    @pl.when(j == n_main)
    def _():
        w = w_ref[:, :LANES]
        lane = lax.broadcasted_iota(jnp.int32, w.shape, 1)
        gate_ref[...] = jnp.where(lane < tail, w, 0.0).astype(BF16)


def _gdn_weight_cast(w_in, n_cols):
    d, total = w_in.shape
    blk = 4 * LANES
    n_main = n_cols // blk
    assert n_cols % blk == 0 and 0 < total - n_cols <= LANES
    return pl.pallas_call(
        functools.partial(_gdn_cast_kernel, n_main=n_main, tail=total - n_cols),
        out_shape=[jax.ShapeDtypeStruct((d, n_cols), BF16), jax.ShapeDtypeStruct((d, LANES), BF16)],
        grid=(n_main + 1,),
        in_specs=[pl.BlockSpec((d, blk), lambda j: (0, j))],
        out_specs=[pl.BlockSpec((d, blk), lambda j: (0, jnp.minimum(j, n_main - 1))),
                   pl.BlockSpec((d, LANES), lambda j: (0, 0))],
        compiler_params=_params("arbitrary"),
        name="gdn_wcast",
    )(w_in)


def _gdn_in(x_tm, norm_w, mod, layer, w_in, conv_w, a_log, dt_bias):
    t, d = x_tm.shape
    bsz = SUBLANES
    seqlen = t // bsz
    heads = a_log.shape[0]
    cc = conv_w.shape[1]
    e = (w_in.shape[1] - cc - 2 * heads)
    qk = (cc - e) // 2
    rows = min(GDN_IN_ROWS, t)
    tmt = rows // bsz
    w_main, w_gate = _gdn_weight_cast(w_in, cc + e)
    gate_par = jnp.pad(jnp.stack([a_log, dt_bias]), ((0, 0), (heads, LANES - 2 * heads)))
    seq_spec = lambda c: pl.BlockSpec((tmt, bsz * c), lambda i: (i, 0))
    return pl.pallas_call(
        functools.partial(_gdn_in_kernel, heads=heads),
        out_shape=[
            jax.ShapeDtypeStruct((seqlen, bsz * qk), BF16),
            jax.ShapeDtypeStruct((seqlen, bsz * qk), BF16),
            jax.ShapeDtypeStruct((seqlen, bsz * e), BF16),
            jax.ShapeDtypeStruct((seqlen, bsz * e), BF16),
            jax.ShapeDtypeStruct((seqlen, bsz * LANES), F32),
        ],
        grid=(t // rows,),
        in_specs=[
            pl.BlockSpec((rows, d), lambda i: (i, 0)),
            _resident((1, d), lambda i: (0, 0)),
            _resident((SUBLANES, d), lambda i: (layer, MOD_SCALE)),
            _resident((SUBLANES, d), lambda i: (layer, MOD_SHIFT)),
            _resident((d, cc + e), lambda i: (0, 0)),
            _resident((d, LANES), lambda i: (0, 0)),
            _resident((GDN_CONV, cc), lambda i: (0, 0)),
            _resident((2, LANES), lambda i: (0, 0)),
        ],
        out_specs=[seq_spec(qk), seq_spec(qk), seq_spec(e), seq_spec(e), seq_spec(LANES)],
        scratch_shapes=[
            pltpu.VMEM((rows + (GDN_CONV - 1) * SUBLANES, cc), F32),
            pltpu.VMEM(((cc + e + LANES) // LANES, rows, LANES), F32),
        ],
        compiler_params=_params("arbitrary"),
        name="gdn_in",
    )(x_tm, norm_w.reshape(1, d), mod, mod, w_main, w_gate, conv_w, gate_par)


def _pair_block_diag(x):
    c = x.shape[0]
    lane = lax.broadcasted_iota(jnp.int32, x.shape, 1)
    zero = jnp.zeros_like(x)
    return jnp.concatenate([jnp.where(lane < c, x, zero), jnp.where(lane >= c, x, zero)], axis=0)


def _pair_mm(x, y):
    return jnp.dot(x.astype(BF16), _pair_block_diag(y.astype(BF16)), preferred_element_type=F32)


def _gdn_core_kernel(q_ref, k_ref, v_ref, z_ref, g_ref, nw_ref, y_ref, s_ref, *, heads, nb):
    @pl.when(pl.program_id(1) == 0)
    def _():
        s_ref[...] = jnp.zeros_like(s_ref)

    c = q_ref.shape[0]
    qk_w = q_ref.shape[1] // nb
    e_w = v_ref.shape[1] // nb
    dk = qk_w // heads
    dv = e_w // heads
    units = [(b, p) for b in range(nb) for p in range(heads // 2)]

    row = lax.broadcasted_iota(jnp.int32, (c, 2 * c), 0)
    col = lax.broadcasted_iota(jnp.int32, (c, 2 * c), 1) & (c - 1)
    causal = row >= col
    strict = row > col
    eye = (row == col).astype(F32)
    row1 = lax.broadcasted_iota(jnp.int32, (c, c), 0)
    col1 = lax.broadcasted_iota(jnp.int32, (c, c), 1)
    tri = (row1 >= col1).astype(BF16)
    nh2 = 2 * heads
    eye_h = (lax.broadcasted_iota(jnp.int32, (nh2, nh2), 0)
             == lax.broadcasted_iota(jnp.int32, (nh2, nh2), 1)).astype(BF16)

    def same_block(shift):
        return (row >> shift) == (col >> shift)

    gb, gc, gc_t = [], [], []
    for b in range(nb):
        g = g_ref[:, b * LANES:b * LANES + nh2]
        cs = sum(jnp.dot(tri, part, preferred_element_type=F32) for part in _split3(g))
        gb.append(g)
        gc.append(cs)
        gc_t.append(sum(lax.dot_general(eye_h, part, (((1,), (1,)), ((), ())), preferred_element_type=F32)
                        for part in _split3(cs)))

    def lanes2(ref, b, p, width, per_batch):
        return ref[:, b * per_batch + 2 * p * width:b * per_batch + 2 * (p + 1) * width]

    def pair_cols(b, p, fn, width):
        return jnp.concatenate([jnp.broadcast_to(fn(b, 2 * p + i), (c, width)) for i in range(2)], axis=1)

    beta_c = lambda b, h: gb[b][:, h:h + 1]
    gcol = lambda b, h: gc[b][:, heads + h:heads + h + 1]
    grow = lambda b, h: gc_t[b][heads + h:heads + h + 1, :]

    a_mat, qk_mat = {}, {}
    for (b, p) in units:
        qp = lanes2(q_ref, b, p, dk, qk_w)
        kp = lanes2(k_ref, b, p, dk, qk_w)
        res = lax.dot_general(jnp.concatenate([qp, kp], axis=0), _pair_block_diag(kp),
                              (((1,), (1,)), ((), ())), preferred_element_type=F32)
        decay = jnp.where(causal, jnp.exp(jnp.concatenate(
            [gcol(b, 2 * p + i) - grow(b, 2 * p + i) for i in range(2)], axis=1)), 0.0)
        qk_mat[b, p] = res[:c] * decay
        a_mat[b, p] = jnp.where(strict, pair_cols(b, p, beta_c, c) * res[c:] * decay, 0.0)

    shift0 = GDN_BASE_BLOCK.bit_length() - 1
    dblk = {u: jnp.where(same_block(shift0), a_mat[u], 0.0) for u in units}
    d2 = {u: _pair_mm(dblk[u], dblk[u]) for u in units}
    d4 = {u: _pair_mm(d2[u], d2[u]) for u in units}
    t_inv = {u: eye - dblk[u] for u in units}
    for pw in (d2, d4):
        upd = {u: _pair_mm(t_inv[u], pw[u]) for u in units}
        t_inv = {u: t_inv[u] + upd[u] for u in units}
    shift = shift0
    while (1 << shift) < c:
        level = same_block(shift + 1) & jnp.logical_not(same_block(shift))
        xt = {u: _pair_mm(jnp.where(level, a_mat[u], 0.0), t_inv[u]) for u in units}
        upd = {u: _pair_mm(t_inv[u], xt[u]) for u in units}
        t_inv = {u: t_inv[u] - upd[u] for u in units}
        shift += 1

    w_pk, u_pk, gam_pk = {}, {}, {}
    for (b, p) in units:
        gam_pk[b, p] = jnp.exp(pair_cols(b, p, gcol, dk))
        kp = lanes2(k_ref, b, p, dk, qk_w).astype(F32)
        vp = lanes2(v_ref, b, p, dv, e_w).astype(F32)
        rw = (pair_cols(b, p, beta_c, dk) * gam_pk[b, p] * kp).astype(BF16)
        ru = (pair_cols(b, p, beta_c, dv) * vp).astype(BF16)
        zk = jnp.zeros((c, dk), BF16)
        zv = jnp.zeros((c, dv), BF16)
        rhs = jnp.concatenate([
            jnp.concatenate([rw[:, :dk], zk, ru[:, :dv], zv], axis=1),
            jnp.concatenate([zk, rw[:, dk:], zv, ru[:, dv:]], axis=1)], axis=0)
        wu = jnp.dot(t_inv[b, p].astype(BF16), rhs, preferred_element_type=F32)
        w_pk[b, p] = wu[:, :2 * dk]
        u_pk[b, p] = wu[:, 2 * dk:]

    hunits = [(b, h) for b in range(nb) for h in range(heads)]
    ws, states = {}, {}
    for (b, h) in hunits:
        p, i = divmod(h, 2)
        qd = lanes2(q_ref, b, p, dk, qk_w).astype(F32) * gam_pk[b, p]
        lhs = jnp.concatenate([w_pk[b, p][:, i * dk:(i + 1) * dk], qd[:, i * dk:(i + 1) * dk]], axis=0)
        states[b, h] = s_ref[b * heads + h]
        ws[b, h] = _bdot(lhs, states[b, h])
    v_new = {(b, h): u_pk[b, h // 2][:, (h % 2) * dv:(h % 2 + 1) * dv] - ws[b, h][:c] for (b, h) in hunits}
    o_intra = {(b, h): _bdot(qk_mat[b, h // 2][:, (h % 2) * c:(h % 2 + 1) * c], v_new[b, h]) for (b, h) in hunits}
    for (b, h) in hunits:
        g_last = gcol(b, h)[c - 1:c, :]
        kh = k_ref[:, b * qk_w + h * dk:b * qk_w + (h + 1) * dk].astype(F32)
        k_dec = kh * jnp.exp(g_last - gcol(b, h))
        s_ref[b * heads + h] = jnp.exp(g_last) * states[b, h] + _bdot_tn(k_dec, v_new[b, h])
    for (b, h) in hunits:
        o = ws[b, h][c:] + o_intra[b, h]
        on = _rms(o) * nw_ref[...]
        zh = z_ref[:, b * e_w + h * dv:b * e_w + (h + 1) * dv].astype(F32)
        y_ref[:, b * e_w + h * dv:b * e_w + (h + 1) * dv] = (on * _silu(zh)).astype(BF16)


def _gdn_core(q, k, v, z, gates, norm_w, heads):
    seqlen = q.shape[0]
    nb = GDN_BATCH_PER_STEP
    bsz = gates.shape[1] // LANES
    qk = q.shape[1] // bsz
    e = v.shape[1] // bsz
    c = min(GDN_CHUNK, seqlen)
    col_spec = lambda w: pl.BlockSpec((c, nb * w), lambda b, n: (n, b))
    return pl.pallas_call(
        functools.partial(_gdn_core_kernel, heads=heads, nb=nb),
        out_shape=jax.ShapeDtypeStruct((seqlen, bsz * e), BF16),
        grid=(bsz // nb, seqlen // c),
        in_specs=[col_spec(qk), col_spec(qk), col_spec(e), col_spec(e), col_spec(LANES),
                  _resident((1, e // heads), lambda b, n: (0, 0))],
        out_specs=col_spec(e),
        scratch_shapes=[pltpu.VMEM((nb * heads, qk // heads, e // heads), F32)],
        compiler_params=_params("parallel", "arbitrary"),
        name="gdn_core",
    )(q, k, v, z, gates, norm_w.reshape(1, e // heads))


def _gdn_out_kernel(y_ref, x_ref, gate_ref, w_ref, fw_ref, o_ref, slab_ref):
    bsz, tmt, d = o_ref.shape
    e = y_ref.shape[1] // bsz
    for j in range(d // LANES):
        slab_ref[j] = x_ref[:, j * LANES:(j + 1) * LANES]
    y = jnp.concatenate([y_ref[:, b * e:(b + 1) * e] for b in range(bsz)], axis=0)
    out = jnp.dot(y, w_ref[...], preferred_element_type=F32).reshape(bsz, tmt, d)
    for b in range(bsz):
        xn = _sequence_rows(slab_ref, b, tmt, 0, d) + gate_ref[b:b + 1, :] * out[b]
        o_ref[b] = _rms(xn) * fw_ref[...]


def _gdn_out(y, x_tm, mod, layer, w_out, final_w):
    e, d = w_out.shape
    seqlen = y.shape[0]
    bsz = y.shape[1] // e
    tmt = min(GDN_OUT_ROWS // bsz, seqlen)
    return pl.pallas_call(
        _gdn_out_kernel,
        out_shape=jax.ShapeDtypeStruct((bsz, seqlen, d), F32),
        grid=(seqlen // tmt,),
        in_specs=[
            pl.BlockSpec((tmt, bsz * e), lambda i: (i, 0)),
            pl.BlockSpec((tmt * bsz, d), lambda i: (i, 0)),
            _resident((bsz, d), lambda i: (layer, MOD_GATE)),
            _resident((e, d), lambda i: (0, 0)),
            _resident((1, d), lambda i: (0, 0)),
        ],
        out_specs=pl.BlockSpec((bsz, tmt, d), lambda i: (0, i, 0)),
        scratch_shapes=[pltpu.VMEM((d // LANES, tmt * bsz, LANES), F32)],
        compiler_params=_params("parallel"),
        name="gdn_out",
    )(y, x_tm, mod, w_out.astype(BF16), final_w.reshape(1, d))


def kernel(x, c, ada_w, ada_b, norm_w, s5_w_in, s5_lambda_re, s5_lambda_im, s5_log_dt, s5_b_re, s5_b_im,
           s5_c_re, s5_c_im, s5_d, s5_w_glu, s5_w_out, gdn_w_in, gdn_conv_w, gdn_a_log, gdn_dt_bias,
           gdn_norm_w, gdn_w_out, final_norm_w):
    bsz, seqlen, d = x.shape
    assert bsz == SUBLANES, "the time-major layout keeps one batch row per vector sublane"
    assert ada_w.shape[0] == 2 and s5_w_in.shape[0] == 1 and gdn_w_in.shape[0] == 1

    mod = _modulation(c, ada_w, ada_b)

    w_b, w_i, w_c, aq = _s5_weights(s5_lambda_re[0], s5_lambda_im[0], s5_log_dt[0], s5_b_re[0], s5_b_im[0],
                                    s5_c_re[0], s5_c_im[0])
    x_tm, u, z = _s5_in(x, norm_w[0], mod, 0, s5_w_in[0])
    runs = min(S5_RUNS, seqlen // S5_RUN)
    chunks = min(S5_CHUNKS, seqlen // (S5_RUN * runs))
    y = _s5_scan(u, w_b, w_i, w_c, aq, s5_d[0], runs, chunks)
    x1_tm = _s5_out(y, z, x_tm, mod, 0, s5_w_glu[0], s5_w_out[0])

    heads = gdn_a_log.shape[1]
    q, k, v, zg, gates = _gdn_in(x1_tm, norm_w[1], mod, 1, gdn_w_in[0], gdn_conv_w[0],
                                 gdn_a_log[0], gdn_dt_bias[0])
    yg = _gdn_core(q, k, v, zg, gates, gdn_norm_w[0], heads)
    return _gdn_out(yg, x1_tm, mod, 1, gdn_w_out[0], final_norm_w)
```

```python
import functools
import math

import jax
import jax.numpy as jnp
from jax import lax
from jax.experimental import pallas as pl
from jax.experimental.pallas import tpu as pltpu

F32 = jnp.float32
BF16 = jnp.bfloat16

NORM_EPS = 1e-6
S5_GROUP = 16
S5_STATE = 64
S5_TILE_GROUPS = 16
S5_RUN = 4
S5_BLOCK_GROUPS = S5_TILE_GROUPS // S5_RUN
S5_TILE_BLOCKS = S5_TILE_GROUPS // S5_BLOCK_GROUPS
S5_RUNS = 64
S5_CHUNKS = 4
GDN_HEADS = 8
GDN_CONV = 4
GDN_CHUNK = 128
GDN_BASE_BLOCK = 8
GDN_BATCH_PER_STEP = 4
ROW_TILE = 512
GDN_IN_ROWS = 256
GDN_OUT_ROWS = 1024
SUBLANES = 8
LANES = 128
VMEM_LIMIT_BYTES = 56 * 1024 * 1024


def _sigmoid(x):
    return 1.0 / (1.0 + jnp.exp(-x))


def _silu(x):
    return x * _sigmoid(x)


def _gelu_tanh(x):
    c = math.sqrt(2.0 / math.pi)
    return 0.5 * x * (1.0 + jnp.tanh(c * (x + 0.044715 * (x * x * x))))


def _softplus(x):
    return jnp.maximum(x, 0.0) + jnp.log(1.0 + jnp.exp(-jnp.abs(x)))


def _rms(x):
    return x * lax.rsqrt(jnp.mean(x * x, axis=-1, keepdims=True) + NORM_EPS)


def _bdot(a, b):
    return jnp.dot(a.astype(BF16), b.astype(BF16), preferred_element_type=F32)


def _bdot_nt(a, b):
    return lax.dot_general(a.astype(BF16), b.astype(BF16), (((1,), (1,)), ((), ())),
                           preferred_element_type=F32)


def _bdot_tn(a, b):
    return lax.dot_general(a.astype(BF16), b.astype(BF16), (((0,), (0,)), ((), ())),
                           preferred_element_type=F32)


def _split3(x):
    hi = x.astype(BF16)
    r = x - hi.astype(F32)
    mid = r.astype(BF16)
    lo = (r - mid.astype(F32)).astype(BF16)
    return hi, mid, lo


def _params(*sem):
    return pltpu.CompilerParams(dimension_semantics=sem, vmem_limit_bytes=VMEM_LIMIT_BYTES)


def _resident(shape, index_map):
    return pl.BlockSpec(shape, index_map, pipeline_mode=pl.Buffered(1))


MOD_SHIFT, MOD_SCALE, MOD_GATE = 0, 1, 2


def _mod_kernel(c_ref, w_ref, b_ref, o_ref):
    o_ref[...] = _bdot(_silu(c_ref[...]), w_ref[0]) + b_ref[0]


def _modulation(c, ada_w, ada_b):
    depth, d, d3 = ada_w.shape
    bsz = c.shape[0]
    nb = d3 // d
    return pl.pallas_call(
        _mod_kernel,
        out_shape=jax.ShapeDtypeStruct((depth * bsz, d3), F32),
        grid=(depth, nb),
        in_specs=[
            pl.BlockSpec((bsz, d), lambda l, j: (0, 0)),
            pl.BlockSpec((1, d, d), lambda l, j: (l, 0, j)),
            pl.BlockSpec((1, 1, d), lambda l, j: (l, 0, j)),
        ],
        out_specs=pl.BlockSpec((bsz, d), lambda l, j: (l, j)),
        compiler_params=_params("parallel", "parallel"),
        name="adaln_mod",
    )(c, ada_w.astype(BF16), ada_b.reshape(depth, 1, d3))


def _cmul(xr, xi, yr, yi):
    return xr * yr - xi * yi, xr * yi + xi * yr


def _place_nt(sel, vals):
    return sum(lax.dot_general(sel, part, (((1,), (1,)), ((), ())), preferred_element_type=F32)
               for part in _split3(vals))


def _s5_prep_kernel(lre_ref, lim_ref, ldt_ref, bre_ref, bim_ref, cre_ref, cim_ref, wb_ref, wi_ref, wc_ref, aq_ref,
                    *, q):
    tg, p, m = bre_ref.shape
    ns, nc = tg * p, tg * m
    lre, lim = lre_ref[0], lim_ref[0]
    dt = jnp.exp(ldt_ref[0])
    mag = jnp.exp(lre * dt)
    a_re = mag * jnp.cos(lim * dt)
    a_im = mag * jnp.sin(lim * dt)
    den = lre * lre + lim * lim
    nr = a_re - 1.0
    q_re = (nr * lre + a_im * lim) / den
    q_im = (a_im * lre - nr * lim) / den

    def iota(shape, axis):
        return lax.broadcasted_iota(jnp.int32, shape, axis)

    sel_n = (iota((nc, m), 0) % m == iota((nc, m), 1)).astype(BF16)
    zb_re = _place_nt(sel_n, bre_ref[...].reshape(ns, m))
    zb_im = _place_nt(sel_n, bim_ref[...].reshape(ns, m))
    same_in = iota((nc, ns), 0) // m == iota((nc, ns), 1) // p
    bb_re, bb_im = _cmul(q_re, q_im, zb_re, zb_im)
    bb = [(jnp.where(same_in, bb_re, 0.0), jnp.where(same_in, bb_im, 0.0))]
    pw_re, pw_im = a_re, a_im
    for _ in range(1, q):
        bb.append(_cmul(pw_re, pw_im, *bb[0]))
        pw_re, pw_im = _cmul(pw_re, pw_im, a_re, a_im)
    aq_ref[0] = jnp.broadcast_to(jnp.concatenate([pw_re, pw_im], axis=1), aq_ref.shape[1:])
    wb = jnp.concatenate([jnp.concatenate(bb[q - 1 - s], axis=1) for s in range(q)], axis=0).astype(BF16)
    wb_ref[0] = wb

    sel_p = (iota((ns, p), 0) % p == iota((ns, p), 1)).astype(BF16)
    same_out = iota((ns, nc), 0) // p == iota((ns, nc), 1) // m
    c_re = jnp.where(same_out, _place_nt(sel_p, cre_ref[...].reshape(nc, p)), 0.0)
    c_im = jnp.where(same_out, _place_nt(sel_p, cim_ref[...].reshape(nc, p)), 0.0)
    eye_s = (iota((ns, ns), 0) == iota((ns, ns), 1)).astype(BF16)
    col_re = _place_nt(eye_s, jnp.broadcast_to(a_re, (nc, ns)))
    col_im = _place_nt(eye_s, jnp.broadcast_to(a_im, (nc, ns)))
    ca = [(c_re, c_im)]
    for _ in range(q):
        ca.append(_cmul(*ca[-1], col_re, col_im))
    wc_ref[0] = jnp.concatenate([jnp.concatenate([ca[k][0] for k in range(1, q + 1)], axis=1),
                                 jnp.concatenate([-ca[k][1] for k in range(1, q + 1)], axis=1)], axis=0).astype(BF16)

    wc0 = jnp.concatenate([c_re, -c_im], axis=0).astype(BF16)
    kd = [jnp.dot(wb[(q - 1 - d) * nc:(q - d) * nc], wc0, preferred_element_type=F32) for d in range(q)]
    zero = jnp.zeros_like(kd[0])
    wi_ref[0] = jnp.concatenate(
        [jnp.concatenate([kd[so - si] if so >= si else zero for so in range(q)], axis=1) for si in range(q)],
        axis=0).astype(BF16)


def _s5_weights(lam_re, lam_im, log_dt, b_re, b_im, c_re, c_im):
    g, p = lam_re.shape
    m = b_re.shape[-1]
    q = S5_RUN
    tg = S5_BLOCK_GROUPS
    nb = g // tg
    ns, nc = tg * p, tg * m
    row = lambda a: a.reshape(nb, 1, ns)
    row_spec = pl.BlockSpec((1, 1, ns), lambda i: (i, 0, 0))
    w_b, w_i, w_c, aq = pl.pallas_call(
        functools.partial(_s5_prep_kernel, q=q),
        out_shape=[jax.ShapeDtypeStruct((nb, q * nc, 2 * ns), BF16),
                   jax.ShapeDtypeStruct((nb, q * nc, q * nc), BF16),
                   jax.ShapeDtypeStruct((nb, 2 * ns, q * nc), BF16),
                   jax.ShapeDtypeStruct((nb, SUBLANES, 2 * ns), F32)],
        grid=(nb,),
        in_specs=[row_spec, row_spec, row_spec,
                  pl.BlockSpec((tg, p, m), lambda i: (i, 0, 0)), pl.BlockSpec((tg, p, m), lambda i: (i, 0, 0)),
                  pl.BlockSpec((tg, m, p), lambda i: (i, 0, 0)), pl.BlockSpec((tg, m, p), lambda i: (i, 0, 0))],
        out_specs=[pl.BlockSpec((1, q * nc, 2 * ns), lambda i: (i, 0, 0)),
                   pl.BlockSpec((1, q * nc, q * nc), lambda i: (i, 0, 0)),
                   pl.BlockSpec((1, 2 * ns, q * nc), lambda i: (i, 0, 0)),
                   pl.BlockSpec((1, SUBLANES, 2 * ns), lambda i: (i, 0, 0))],
        compiler_params=_params("parallel"),
        name="s5_prep",
    )(row(lam_re), row(lam_im), row(jnp.broadcast_to(log_dt[:, None], (g, p))), b_re, b_im, c_re, c_im)
    per_tile = lambda w: w.reshape((nb // S5_TILE_BLOCKS, S5_TILE_BLOCKS) + w.shape[1:])
    return per_tile(w_b), per_tile(w_i), per_tile(w_c), per_tile(aq)


def _modulated_norm_rows(x, nw, scale, shift):
    r, d = x.shape
    y = (_rms(x) * nw).reshape(r // SUBLANES, SUBLANES, d)
    return (y * (1.0 + scale)[None] + shift[None]).reshape(r, d)


def _sequence_rows(slab_ref, b, tmt, lo, hi):
    return jnp.concatenate([slab_ref[j, pl.ds(b, tmt, stride=SUBLANES), :]
                            for j in range(lo // LANES, hi // LANES)], axis=1)


def _s5_in_kernel(x_ref, nw_ref, sc_ref, sh_ref, w_ref, xt_ref, u_ref, z_ref, slab_ref):
    bsz, tmt, d = x_ref.shape
    for j in range(d // LANES):
        for b in range(bsz):
            slab_ref[j, pl.ds(b, tmt, stride=bsz), :] = x_ref[b, :, j * LANES:(j + 1) * LANES]
    x_tm = jnp.concatenate([slab_ref[j] for j in range(d // LANES)], axis=1)
    xt_ref[...] = x_tm
    h = _modulated_norm_rows(x_tm, nw_ref[...], sc_ref[...], sh_ref[...]).astype(BF16)
    e = z_ref.shape[1]
    nt, _, cw = u_ref.shape
    u = jnp.dot(h, w_ref[:, :e], preferred_element_type=F32).astype(BF16)
    for k in range(nt):
        u_ref[k] = u[:, k * cw:(k + 1) * cw]
    z_ref[...] = jnp.dot(h, w_ref[:, e:], preferred_element_type=F32).astype(BF16)


def _s5_in(x, norm_w, mod, layer, w_in):
    bsz, seqlen, d = x.shape
    e = w_in.shape[1] // 2
    tmt = min(ROW_TILE // bsz, seqlen)
    rows = tmt * bsz
    t = seqlen * bsz
    cw = S5_TILE_GROUPS * S5_GROUP
    nt = e // cw
    return pl.pallas_call(
        _s5_in_kernel,
        out_shape=[jax.ShapeDtypeStruct((t, d), F32), jax.ShapeDtypeStruct((nt, t, cw), BF16),
                   jax.ShapeDtypeStruct((t, e), BF16)],
        grid=(seqlen // tmt,),
        in_specs=[
            pl.BlockSpec((bsz, tmt, d), lambda i: (0, i, 0)),
            _resident((1, d), lambda i: (0, 0)),
            _resident((bsz, d), lambda i: (layer, MOD_SCALE)),
            _resident((bsz, d), lambda i: (layer, MOD_SHIFT)),
            _resident((d, 2 * e), lambda i: (0, 0)),
        ],
        out_specs=[pl.BlockSpec((rows, d), lambda i: (i, 0)), pl.BlockSpec((nt, rows, cw), lambda i: (0, i, 0)),
                   pl.BlockSpec((rows, e), lambda i: (i, 0))],
        scratch_shapes=[pltpu.VMEM((d // LANES, rows, LANES), F32)],
        compiler_params=_params("parallel"),
        name="s5_in",
    )(x, norm_w.reshape(1, d), mod, mod, w_in.astype(BF16))


def _s5_scan_kernel(u_ref, wb_ref, wi_ref, wc_ref, a_ref, d_ref, y_ref, sin_ref, xs_ref, st_ref, *, runs, chunks, q):
    @pl.when(pl.program_id(1) == 0)
    def _():
        st_ref[...] = jnp.zeros_like(st_ref)

    nblk = wb_ref.shape[1]
    bw = wi_ref.shape[2] // q
    half = wb_ref.shape[3] // 2
    nl = half // LANES
    rows_c = runs * SUBLANES
    a_re = [[a_ref[0, k, :, j * LANES:(j + 1) * LANES] for j in range(nl)] for k in range(nblk)]
    a_im = [[a_ref[0, k, :, half + j * LANES:half + (j + 1) * LANES] for j in range(nl)] for k in range(nblk)]
    x_re = [[st_ref[k, 0, :, j * LANES:(j + 1) * LANES] for j in range(nl)] for k in range(nblk)]
    x_im = [[st_ref[k, 1, :, j * LANES:(j + 1) * LANES] for j in range(nl)] for k in range(nblk)]
    intra = {}

    def project_in(ci):
        u = u_ref[0, ci * q * rows_c:(ci + 1) * q * rows_c, :].astype(F32).reshape(runs, q, SUBLANES, nblk * bw)
        us = [u[:, s].reshape(rows_c, nblk * bw) for s in range(q)]
        slot = ci % 2
        for k in range(nblk):
            uu = jnp.concatenate([us[s][:, k * bw:(k + 1) * bw] for s in range(q)], axis=1).astype(BF16)
            sin_ref[slot, k] = jnp.dot(uu, wb_ref[0, k], preferred_element_type=F32)
            intra[ci, k] = jnp.dot(uu, wi_ref[0, k], preferred_element_type=F32)

    def recur(ci):
        slot = ci % 2
        for pp in range(runs // 2):
            r0 = 2 * pp * SUBLANES
            for k in range(nblk):
                for j in range(nl):
                    lo, hi = j * LANES, (j + 1) * LANES
                    before_re, before_im = [], []
                    for s in range(2):
                        rr = r0 + s * SUBLANES
                        before_re.append(x_re[k][j])
                        before_im.append(x_im[k][j])
                        b_re = sin_ref[slot, k, rr:rr + SUBLANES, lo:hi]
                        b_im = sin_ref[slot, k, rr:rr + SUBLANES, half + lo:half + hi]
                        v_re = a_re[k][j] * x_re[k][j] - a_im[k][j] * x_im[k][j] + b_re
                        v_im = a_re[k][j] * x_im[k][j] + a_im[k][j] * x_re[k][j] + b_im
                        x_re[k][j], x_im[k][j] = v_re, v_im
                    xs_ref[slot, k, r0:r0 + 2 * SUBLANES, lo:hi] = jnp.concatenate(before_re, axis=0).astype(BF16)
                    xs_ref[slot, k, r0:r0 + 2 * SUBLANES, half + lo:half + hi] = (
                        jnp.concatenate(before_im, axis=0).astype(BF16))

    def project_out(ci):
        slot = ci % 2
        yk = [intra.pop((ci, k)) + jnp.dot(xs_ref[slot, k], wc_ref[0, k], preferred_element_type=F32)
              for k in range(nblk)]
        steps = [jnp.concatenate([yk[k][:, s * bw:(s + 1) * bw] for k in range(nblk)], axis=1)
                 .reshape(runs, 1, SUBLANES, nblk * bw) for s in range(q)]
        y = jnp.concatenate(steps, axis=1).reshape(q * rows_c, nblk * bw)
        u = u_ref[0, ci * q * rows_c:(ci + 1) * q * rows_c, :].astype(F32)
        y_ref[0, ci * q * rows_c:(ci + 1) * q * rows_c, :] = _gelu_tanh(y + d_ref[0] * u).astype(BF16)

    project_in(0)
    for ci in range(chunks):
        if ci + 1 < chunks:
            project_in(ci + 1)
        recur(ci)
        project_out(ci)

    for k in range(nblk):
        for j in range(nl):
            st_ref[k, 0, :, j * LANES:(j + 1) * LANES] = x_re[k][j]
            st_ref[k, 1, :, j * LANES:(j + 1) * LANES] = x_im[k][j]


def _s5_scan(u_tiles, w_b, w_i, w_c, aq, d_skip, runs, chunks):
    nt, nblk, kw, sw = w_b.shape
    q = S5_RUN
    cw = u_tiles.shape[2]
    t = u_tiles.shape[1]
    rows_c = runs * SUBLANES
    rows = q * rows_c * chunks
    return pl.pallas_call(
        functools.partial(_s5_scan_kernel, runs=runs, chunks=chunks, q=q),
        out_shape=jax.ShapeDtypeStruct((nt, t, cw), BF16),
        grid=(nt, t // rows),
        in_specs=[
            pl.BlockSpec((1, rows, cw), lambda k, c: (k, c, 0)),
            _resident((1, nblk, kw, sw), lambda k, c: (k, 0, 0, 0)),
            _resident((1, nblk, kw, kw), lambda k, c: (k, 0, 0, 0)),
            _resident((1, nblk, sw, kw), lambda k, c: (k, 0, 0, 0)),
            _resident((1, nblk, SUBLANES, sw), lambda k, c: (k, 0, 0, 0)),
            _resident((1, 1, cw), lambda k, c: (k, 0, 0)),
        ],
        out_specs=pl.BlockSpec((1, rows, cw), lambda k, c: (k, c, 0)),
        scratch_shapes=[
            pltpu.VMEM((2, nblk, rows_c, sw), F32),
            pltpu.VMEM((2, nblk, rows_c, sw), BF16),
            pltpu.VMEM((nblk, 2, SUBLANES, sw // 2), F32),
        ],
        compiler_params=_params("parallel", "arbitrary"),
        name="s5_scan",
    )(u_tiles, w_b, w_i, w_c, aq, d_skip.reshape(nt, 1, cw))


def _s5_out_kernel(y_ref, z_ref, x_ref, gate_ref, wg_ref, wo_ref, o_ref):
    y = jnp.concatenate([y_ref[k] for k in range(y_ref.shape[0])], axis=1)
    yf = y.astype(F32)
    y2 = yf * _sigmoid(jnp.dot(y, wg_ref[...], preferred_element_type=F32)) * _silu(z_ref[...].astype(F32))
    out = jnp.dot(y2.astype(BF16), wo_ref[...], preferred_element_type=F32)
    r, d = out.shape
    x = x_ref[...].reshape(r // SUBLANES, SUBLANES, d)
    o_ref[...] = (x + gate_ref[...][None] * out.reshape(r // SUBLANES, SUBLANES, d)).reshape(r, d)


def _s5_out(y, z, x_tm, mod, layer, w_glu, w_out):
    t, d = x_tm.shape
    e = w_glu.shape[0]
    nt, _, cw = y.shape
    rows = min(ROW_TILE, t)
    row_spec = lambda c: pl.BlockSpec((rows, c), lambda i: (i, 0))
    return pl.pallas_call(
        _s5_out_kernel,
        out_shape=jax.ShapeDtypeStruct((t, d), F32),
        grid=(t // rows,),
        in_specs=[
            pl.BlockSpec((nt, rows, cw), lambda i: (0, i, 0)), row_spec(e), row_spec(d),
            _resident((SUBLANES, d), lambda i: (layer, MOD_GATE)),
            _resident((e, e), lambda i: (0, 0)),
            _resident((e, d), lambda i: (0, 0)),
        ],
        out_specs=row_spec(d),
        compiler_params=_params("parallel"),
        name="s5_out",
    )(y, z, x_tm, mod, w_glu.astype(BF16), w_out.astype(BF16))


def _gdn_in_kernel(x_ref, nw_ref, sc_ref, sh_ref, w_ref, wg_ref, cw_ref, gp_ref,
                   q_ref, k_ref, v_ref, z_ref, g_ref, ext_ref, st_ref, *, heads):
    halo = (GDN_CONV - 1) * SUBLANES
    r, d = x_ref.shape
    cc = ext_ref.shape[1]
    bsz = SUBLANES
    tmt = r // bsz
    qk = q_ref.shape[1] // bsz
    e = v_ref.shape[1] // bsz
    dk = qk // heads

    @pl.when(pl.program_id(0) == 0)
    def _():
        ext_ref[0:halo, :] = jnp.zeros((halo, cc), F32)

    h = _modulated_norm_rows(x_ref[...], nw_ref[...], sc_ref[...], sh_ref[...]).astype(BF16)

    ext_ref[halo:halo + r, :] = jnp.dot(h, w_ref[:, :cc], preferred_element_type=F32)
    conv = cw_ref[0:1, :] * ext_ref[0:r, :]
    for j in range(1, GDN_CONV):
        conv = conv + cw_ref[j:j + 1, :] * ext_ref[j * SUBLANES:j * SUBLANES + r, :]
    ext_ref[0:halo, :] = ext_ref[r:r + halo, :]
    act = _silu(conv)

    def stage(lo, val):
        for j in range(val.shape[1] // LANES):
            st_ref[lo // LANES + j] = val[:, j * LANES:(j + 1) * LANES]

    for hh in range(heads):
        qh = act[:, hh * dk:(hh + 1) * dk]
        kh = act[:, qk + hh * dk:qk + (hh + 1) * dk]
        stage(hh * dk, qh * lax.rsqrt(jnp.sum(qh * qh, axis=-1, keepdims=True) + NORM_EPS) * (dk ** -0.5))
        stage(qk + hh * dk, kh * lax.rsqrt(jnp.sum(kh * kh, axis=-1, keepdims=True) + NORM_EPS))
    stage(2 * qk, act[:, 2 * qk:])
    stage(cc, jnp.dot(h, w_ref[:, cc:], preferred_element_type=F32))

    logit = jnp.dot(h, wg_ref[...], preferred_element_type=F32)
    beta = _sigmoid(logit)
    g = -jnp.exp(gp_ref[0:1, :]) * _softplus(logit + gp_ref[1:2, :])
    lane = lax.broadcasted_iota(jnp.int32, logit.shape, 1)
    stage(cc + e, jnp.where(lane < heads, beta, g))

    for b in range(bsz):
        seq = lambda lo, hi: _sequence_rows(st_ref, b, tmt, lo, hi)
        q_ref[:, b * qk:(b + 1) * qk] = seq(0, qk).astype(BF16)
        k_ref[:, b * qk:(b + 1) * qk] = seq(qk, 2 * qk).astype(BF16)
        v_ref[:, b * e:(b + 1) * e] = seq(2 * qk, cc).astype(BF16)
        z_ref[:, b * e:(b + 1) * e] = seq(cc, cc + e).astype(BF16)
        g_ref[:, b * LANES:(b + 1) * LANES] = seq(cc + e, cc + e + LANES)


def _gdn_cast_kernel(w_ref, main_ref, gate_ref, *, n_main, tail):
    j = pl.program_id(0)

    @pl.when(j < n_main)
    def _():
        main_ref[...] = w_ref[0].astype(BF16)

    @pl.when(j == n_main)
    def _():
        w = w_ref[0, :, :LANES]
        lane = lax.broadcasted_iota(jnp.int32, w.shape, 1)
        gate_ref[...] = jnp.where(lane < tail, w, 0.0).astype(BF16)


def _gdn_weight_cast(w_stack, layer, n_cols):
    _, d, total = w_stack.shape
    blk = 4 * LANES
    n_main = n_cols // blk
    assert n_cols % blk == 0 and 0 < total - n_cols <= LANES
    return pl.pallas_call(
        functools.partial(_gdn_cast_kernel, n_main=n_main, tail=total - n_cols),
        out_shape=[jax.ShapeDtypeStruct((d, n_cols), BF16), jax.ShapeDtypeStruct((d, LANES), BF16)],
        grid=(n_main + 1,),
        in_specs=[pl.BlockSpec((1, d, blk), lambda j: (layer, 0, j))],
        out_specs=[pl.BlockSpec((d, blk), lambda j: (0, jnp.minimum(j, n_main - 1))),
                   pl.BlockSpec((d, LANES), lambda j: (0, 0))],
        compiler_params=_params("arbitrary"),
        name="gdn_wcast",
    )(w_stack)


def _gdn_in(x_tm, norm_w, mod, layer, w_stack, w_layer, conv_w, a_log, dt_bias):
    t, d = x_tm.shape
    bsz = SUBLANES
    seqlen = t // bsz
    heads = a_log.shape[0]
    cc = conv_w.shape[1]
    e = (w_stack.shape[2] - cc - 2 * heads)
    qk = (cc - e) // 2
    rows = min(GDN_IN_ROWS, t)
    tmt = rows // bsz
    w_main, w_gate = _gdn_weight_cast(w_stack, w_layer, cc + e)
    gate_par = jnp.pad(jnp.stack([a_log, dt_bias]), ((0, 0), (heads, LANES - 2 * heads)))
    seq_spec = lambda c: pl.BlockSpec((tmt, bsz * c), lambda i: (i, 0))
    return pl.pallas_call(
        functools.partial(_gdn_in_kernel, heads=heads),
        out_shape=[
            jax.ShapeDtypeStruct((seqlen, bsz * qk), BF16),
            jax.ShapeDtypeStruct((seqlen, bsz * qk), BF16),
            jax.ShapeDtypeStruct((seqlen, bsz * e), BF16),
            jax.ShapeDtypeStruct((seqlen, bsz * e), BF16),
            jax.ShapeDtypeStruct((seqlen, bsz * LANES), F32),
        ],
        grid=(t // rows,),
        in_specs=[
            pl.BlockSpec((rows, d), lambda i: (i, 0)),
            _resident((1, d), lambda i: (0, 0)),
            _resident((SUBLANES, d), lambda i: (layer, MOD_SCALE)),
            _resident((SUBLANES, d), lambda i: (layer, MOD_SHIFT)),
            _resident((d, cc + e), lambda i: (0, 0)),
            _resident((d, LANES), lambda i: (0, 0)),
            _resident((GDN_CONV, cc), lambda i: (0, 0)),
            _resident((2, LANES), lambda i: (0, 0)),
        ],
        out_specs=[seq_spec(qk), seq_spec(qk), seq_spec(e), seq_spec(e), seq_spec(LANES)],
        scratch_shapes=[
            pltpu.VMEM((rows + (GDN_CONV - 1) * SUBLANES, cc), F32),
            pltpu.VMEM(((cc + e + LANES) // LANES, rows, LANES), F32),
        ],
        compiler_params=_params("arbitrary"),
        name="gdn_in",
    )(x_tm, norm_w.reshape(1, d), mod, mod, w_main, w_gate, conv_w, gate_par)


def _pair_block_diag(x):
    c = x.shape[0]
    lane = lax.broadcasted_iota(jnp.int32, x.shape, 1)
    zero = jnp.zeros_like(x)
    return jnp.concatenate([jnp.where(lane < c, x, zero), jnp.where(lane >= c, x, zero)], axis=0)


def _pair_mm(x, y):
    return jnp.dot(x.astype(BF16), _pair_block_diag(y.astype(BF16)), preferred_element_type=F32)


def _gdn_core_kernel(q_ref, k_ref, v_ref, z_ref, g_ref, nw_ref, y_ref, s_ref, *, heads, nb):
    @pl.when(pl.program_id(1) == 0)
    def _():
        s_ref[...] = jnp.zeros_like(s_ref)

    c = q_ref.shape[0]
    qk_w = q_ref.shape[1] // nb
    e_w = v_ref.shape[1] // nb
    dk = qk_w // heads
    dv = e_w // heads
    units = [(b, p) for b in range(nb) for p in range(heads // 2)]

    row = lax.broadcasted_iota(jnp.int32, (c, 2 * c), 0)
    col = lax.broadcasted_iota(jnp.int32, (c, 2 * c), 1) & (c - 1)
    causal = row >= col
    strict = row > col
    eye = (row == col).astype(F32)
    row1 = lax.broadcasted_iota(jnp.int32, (c, c), 0)
    col1 = lax.broadcasted_iota(jnp.int32, (c, c), 1)
    tri = (row1 >= col1).astype(BF16)
    nh2 = 2 * heads
    eye_h = (lax.broadcasted_iota(jnp.int32, (nh2, nh2), 0)
             == lax.broadcasted_iota(jnp.int32, (nh2, nh2), 1)).astype(BF16)

    def same_block(shift):
        return (row >> shift) == (col >> shift)

    gb, gc, gc_t = [], [], []
    for b in range(nb):
        g = g_ref[:, b * LANES:b * LANES + nh2]
        cs = sum(jnp.dot(tri, part, preferred_element_type=F32) for part in _split3(g))
        gb.append(g)
        gc.append(cs)
        gc_t.append(sum(lax.dot_general(eye_h, part, (((1,), (1,)), ((), ())), preferred_element_type=F32)
                        for part in _split3(cs)))

    def lanes2(ref, b, p, width, per_batch):
        return ref[:, b * per_batch + 2 * p * width:b * per_batch + 2 * (p + 1) * width]

    def pair_cols(b, p, fn, width):
        return jnp.concatenate([jnp.broadcast_to(fn(b, 2 * p + i), (c, width)) for i in range(2)], axis=1)

    beta_c = lambda b, h: gb[b][:, h:h + 1]
    gcol = lambda b, h: gc[b][:, heads + h:heads + h + 1]
    grow = lambda b, h: gc_t[b][heads + h:heads + h + 1, :]

    a_mat, qk_mat = {}, {}
    for (b, p) in units:
        qp = lanes2(q_ref, b, p, dk, qk_w)
        kp = lanes2(k_ref, b, p, dk, qk_w)
        res = lax.dot_general(jnp.concatenate([qp, kp], axis=0), _pair_block_diag(kp),
                              (((1,), (1,)), ((), ())), preferred_element_type=F32)
        decay = jnp.where(causal, jnp.exp(jnp.concatenate(
            [gcol(b, 2 * p + i) - grow(b, 2 * p + i) for i in range(2)], axis=1)), 0.0)
        qk_mat[b, p] = res[:c] * decay
        a_mat[b, p] = jnp.where(strict, pair_cols(b, p, beta_c, c) * res[c:] * decay, 0.0)

    shift0 = GDN_BASE_BLOCK.bit_length() - 1
    dblk = {u: jnp.where(same_block(shift0), a_mat[u], 0.0) for u in units}
    d2 = {u: _pair_mm(dblk[u], dblk[u]) for u in units}
    d4 = {u: _pair_mm(d2[u], d2[u]) for u in units}
    t_inv = {u: eye - dblk[u] for u in units}
    for pw in (d2, d4):
        upd = {u: _pair_mm(t_inv[u], pw[u]) for u in units}
        t_inv = {u: t_inv[u] + upd[u] for u in units}
    shift = shift0
    while (1 << shift) < c:
        level = same_block(shift + 1) & jnp.logical_not(same_block(shift))
        xt = {u: _pair_mm(jnp.where(level, a_mat[u], 0.0), t_inv[u]) for u in units}
        upd = {u: _pair_mm(t_inv[u], xt[u]) for u in units}
        t_inv = {u: t_inv[u] - upd[u] for u in units}
        shift += 1

    w_pk, u_pk, gam_pk = {}, {}, {}
    for (b, p) in units:
        gam_pk[b, p] = jnp.exp(pair_cols(b, p, gcol, dk))
        kp = lanes2(k_ref, b, p, dk, qk_w).astype(F32)
        vp = lanes2(v_ref, b, p, dv, e_w).astype(F32)
        rw = (pair_cols(b, p, beta_c, dk) * gam_pk[b, p] * kp).astype(BF16)
        ru = (pair_cols(b, p, beta_c, dv) * vp).astype(BF16)
        zk = jnp.zeros((c, dk), BF16)
        zv = jnp.zeros((c, dv), BF16)
        rhs = jnp.concatenate([
            jnp.concatenate([rw[:, :dk], zk, ru[:, :dv], zv], axis=1),
            jnp.concatenate([zk, rw[:, dk:], zv, ru[:, dv:]], axis=1)], axis=0)
        wu = jnp.dot(t_inv[b, p].astype(BF16), rhs, preferred_element_type=F32)
        w_pk[b, p] = wu[:, :2 * dk]
        u_pk[b, p] = wu[:, 2 * dk:]

    hunits = [(b, h) for b in range(nb) for h in range(heads)]
    ws, states = {}, {}
    for (b, h) in hunits:
        p, i = divmod(h, 2)
        qd = lanes2(q_ref, b, p, dk, qk_w).astype(F32) * gam_pk[b, p]
        lhs = jnp.concatenate([w_pk[b, p][:, i * dk:(i + 1) * dk], qd[:, i * dk:(i + 1) * dk]], axis=0)
        states[b, h] = s_ref[b * heads + h]
        ws[b, h] = _bdot(lhs, states[b, h])
    v_new = {(b, h): u_pk[b, h // 2][:, (h % 2) * dv:(h % 2 + 1) * dv] - ws[b, h][:c] for (b, h) in hunits}
    o_intra = {(b, h): _bdot(qk_mat[b, h // 2][:, (h % 2) * c:(h % 2 + 1) * c], v_new[b, h]) for (b, h) in hunits}
    for (b, h) in hunits:
        g_last = gcol(b, h)[c - 1:c, :]
        kh = k_ref[:, b * qk_w + h * dk:b * qk_w + (h + 1) * dk].astype(F32)
        k_dec = kh * jnp.exp(g_last - gcol(b, h))
        s_ref[b * heads + h] = jnp.exp(g_last) * states[b, h] + _bdot_tn(k_dec, v_new[b, h])
    for (b, h) in hunits:
        o = ws[b, h][c:] + o_intra[b, h]
        on = _rms(o) * nw_ref[...]
        zh = z_ref[:, b * e_w + h * dv:b * e_w + (h + 1) * dv].astype(F32)
        y_ref[:, b * e_w + h * dv:b * e_w + (h + 1) * dv] = (on * _silu(zh)).astype(BF16)


def _gdn_core(q, k, v, z, gates, norm_w, heads):
    seqlen = q.shape[0]
    nb = GDN_BATCH_PER_STEP
    bsz = gates.shape[1] // LANES
    qk = q.shape[1] // bsz
    e = v.shape[1] // bsz
    c = min(GDN_CHUNK, seqlen)
    col_spec = lambda w: pl.BlockSpec((c, nb * w), lambda b, n: (n, b))
    return pl.pallas_call(
        functools.partial(_gdn_core_kernel, heads=heads, nb=nb),
        out_shape=jax.ShapeDtypeStruct((seqlen, bsz * e), BF16),
        grid=(bsz // nb, seqlen // c),
        in_specs=[col_spec(qk), col_spec(qk), col_spec(e), col_spec(e), col_spec(LANES),
                  _resident((1, e // heads), lambda b, n: (0, 0))],
        out_specs=col_spec(e),
        scratch_shapes=[pltpu.VMEM((nb * heads, qk // heads, e // heads), F32)],
        compiler_params=_params("parallel", "arbitrary"),
        name="gdn_core",
    )(q, k, v, z, gates, norm_w.reshape(1, e // heads))


def _gdn_out_kernel(y_ref, x_ref, gate_ref, w_ref, fw_ref, o_ref, slab_ref):
    bsz, tmt, d = o_ref.shape
    e = y_ref.shape[1] // bsz
    for j in range(d // LANES):
        slab_ref[j] = x_ref[:, j * LANES:(j + 1) * LANES]
    y = jnp.concatenate([y_ref[:, b * e:(b + 1) * e] for b in range(bsz)], axis=0)
    out = jnp.dot(y, w_ref[...], preferred_element_type=F32).reshape(bsz, tmt, d)
    for b in range(bsz):
        xn = _sequence_rows(slab_ref, b, tmt, 0, d) + gate_ref[b:b + 1, :] * out[b]
        o_ref[b] = _rms(xn) * fw_ref[...]


def _gdn_out(y, x_tm, mod, layer, w_out, final_w):
    e, d = w_out.shape
    seqlen = y.shape[0]
    bsz = y.shape[1] // e
    tmt = min(GDN_OUT_ROWS // bsz, seqlen)
    return pl.pallas_call(
        _gdn_out_kernel,
        out_shape=jax.ShapeDtypeStruct((bsz, seqlen, d), F32),
        grid=(seqlen // tmt,),
        in_specs=[
            pl.BlockSpec((tmt, bsz * e), lambda i: (i, 0)),
            pl.BlockSpec((tmt * bsz, d), lambda i: (i, 0)),
            _resident((bsz, d), lambda i: (layer, MOD_GATE)),
            _resident((e, d), lambda i: (0, 0)),
            _resident((1, d), lambda i: (0, 0)),
        ],
        out_specs=pl.BlockSpec((bsz, tmt, d), lambda i: (0, i, 0)),
        scratch_shapes=[pltpu.VMEM((d // LANES, tmt * bsz, LANES), F32)],
        compiler_params=_params("parallel"),
        name="gdn_out",
    )(y, x_tm, mod, w_out.astype(BF16), final_w.reshape(1, d))


def kernel(x, c, ada_w, ada_b, norm_w, s5_w_in, s5_lambda_re, s5_lambda_im, s5_log_dt, s5_b_re, s5_b_im,
           s5_c_re, s5_c_im, s5_d, s5_w_glu, s5_w_out, gdn_w_in, gdn_conv_w, gdn_a_log, gdn_dt_bias,
           gdn_norm_w, gdn_w_out, final_norm_w):
    bsz, seqlen, d = x.shape
    assert bsz == SUBLANES, "the time-major layout keeps one batch row per vector sublane"
    assert ada_w.shape[0] == 2 and s5_w_in.shape[0] == 1 and gdn_w_in.shape[0] == 1

    mod = _modulation(c, ada_w, ada_b)

    w_b, w_i, w_c, aq = _s5_weights(s5_lambda_re[0], s5_lambda_im[0], s5_log_dt[0], s5_b_re[0], s5_b_im[0],
                                    s5_c_re[0], s5_c_im[0])
    x_tm, u, z = _s5_in(x, norm_w[0], mod, 0, s5_w_in[0])
    runs = min(S5_RUNS, seqlen // S5_RUN)
    chunks = min(S5_CHUNKS, seqlen // (S5_RUN * runs))
    y = _s5_scan(u, w_b, w_i, w_c, aq, s5_d[0], runs, chunks)
    x1_tm = _s5_out(y, z, x_tm, mod, 0, s5_w_glu[0], s5_w_out[0])

    heads = gdn_a_log.shape[1]
    q, k, v, zg, gates = _gdn_in(x1_tm, norm_w[1], mod, 1, gdn_w_in, 0, gdn_conv_w[0],
                                 gdn_a_log[0], gdn_dt_bias[0])
    yg = _gdn_core(q, k, v, zg, gates, gdn_norm_w[0], heads)
    return _gdn_out(yg, x1_tm, mod, 1, gdn_w_out[0], final_norm_w)
```

```python
import functools
import math

import jax
import jax.numpy as jnp
from jax import lax
from jax.experimental import pallas as pl
from jax.experimental.pallas import tpu as pltpu

F32 = jnp.float32
BF16 = jnp.bfloat16

NORM_EPS = 1e-6
S5_GROUP = 16
S5_STATE = 64
S5_TILE_GROUPS = 16
S5_RUN = 4
S5_BLOCK_GROUPS = S5_TILE_GROUPS // S5_RUN
S5_TILE_BLOCKS = S5_TILE_GROUPS // S5_BLOCK_GROUPS
S5_RUNS = 64
S5_CHUNKS = 4
GDN_HEADS = 8
GDN_CONV = 4
GDN_CHUNK = 128
GDN_BASE_BLOCK = 8
GDN_BATCH_PER_STEP = 4
ROW_TILE = 512
GDN_IN_ROWS = 256
SUBLANES = 8
LANES = 128
VMEM_LIMIT_BYTES = 56 * 1024 * 1024


def _sigmoid(x):
    return 1.0 / (1.0 + jnp.exp(-x))


def _silu(x):
    return x * _sigmoid(x)


def _gelu_tanh(x):
    c = math.sqrt(2.0 / math.pi)
    return 0.5 * x * (1.0 + jnp.tanh(c * (x + 0.044715 * (x * x * x))))


def _softplus(x):
    return jnp.maximum(x, 0.0) + jnp.log(1.0 + jnp.exp(-jnp.abs(x)))


def _rms(x):
    return x * lax.rsqrt(jnp.mean(x * x, axis=-1, keepdims=True) + NORM_EPS)


def _bdot(a, b):
    return jnp.dot(a.astype(BF16), b.astype(BF16), preferred_element_type=F32)


def _bdot_nt(a, b):
    return lax.dot_general(a.astype(BF16), b.astype(BF16), (((1,), (1,)), ((), ())),
                           preferred_element_type=F32)


def _bdot_tn(a, b):
    return lax.dot_general(a.astype(BF16), b.astype(BF16), (((0,), (0,)), ((), ())),
                           preferred_element_type=F32)


def _split3(x):
    hi = x.astype(BF16)
    r = x - hi.astype(F32)
    mid = r.astype(BF16)
    lo = (r - mid.astype(F32)).astype(BF16)
    return hi, mid, lo


def _params(*sem):
    return pltpu.CompilerParams(dimension_semantics=sem, vmem_limit_bytes=VMEM_LIMIT_BYTES)


def _resident(shape, index_map):
    return pl.BlockSpec(shape, index_map, pipeline_mode=pl.Buffered(1))


MOD_SHIFT, MOD_SCALE, MOD_GATE = 0, 1, 2


def _mod_kernel(c_ref, w_ref, b_ref, o_ref):
    o_ref[...] = _bdot(_silu(c_ref[...]), w_ref[0]) + b_ref[0]


def _modulation(c, ada_w, ada_b):
    depth, d, d3 = ada_w.shape
    bsz = c.shape[0]
    nb = d3 // d
    return pl.pallas_call(
        _mod_kernel,
        out_shape=jax.ShapeDtypeStruct((depth * bsz, d3), F32),
        grid=(depth, nb),
        in_specs=[
            pl.BlockSpec((bsz, d), lambda l, j: (0, 0)),
            pl.BlockSpec((1, d, d), lambda l, j: (l, 0, j)),
            pl.BlockSpec((1, 1, d), lambda l, j: (l, 0, j)),
        ],
        out_specs=pl.BlockSpec((bsz, d), lambda l, j: (l, j)),
        compiler_params=_params("parallel", "parallel"),
        name="adaln_mod",
    )(c, ada_w.astype(BF16), ada_b.reshape(depth, 1, d3))


def _cmul(xr, xi, yr, yi):
    return xr * yr - xi * yi, xr * yi + xi * yr


def _place_nt(sel, vals):
    return sum(lax.dot_general(sel, part, (((1,), (1,)), ((), ())), preferred_element_type=F32)
               for part in _split3(vals))


def _s5_prep_kernel(lre_ref, lim_ref, ldt_ref, bre_ref, bim_ref, cre_ref, cim_ref, wb_ref, wi_ref, wc_ref, aq_ref,
                    *, q):
    tg, p, m = bre_ref.shape
    ns, nc = tg * p, tg * m
    lre, lim = lre_ref[0], lim_ref[0]
    dt = jnp.exp(ldt_ref[0])
    mag = jnp.exp(lre * dt)
    a_re = mag * jnp.cos(lim * dt)
    a_im = mag * jnp.sin(lim * dt)
    den = lre * lre + lim * lim
    nr = a_re - 1.0
    q_re = (nr * lre + a_im * lim) / den
    q_im = (a_im * lre - nr * lim) / den

    def iota(shape, axis):
        return lax.broadcasted_iota(jnp.int32, shape, axis)

    sel_n = (iota((nc, m), 0) % m == iota((nc, m), 1)).astype(BF16)
    zb_re = _place_nt(sel_n, bre_ref[...].reshape(ns, m))
    zb_im = _place_nt(sel_n, bim_ref[...].reshape(ns, m))
    same_in = iota((nc, ns), 0) // m == iota((nc, ns), 1) // p
    bb_re, bb_im = _cmul(q_re, q_im, zb_re, zb_im)
    bb = [(jnp.where(same_in, bb_re, 0.0), jnp.where(same_in, bb_im, 0.0))]
    pw_re, pw_im = a_re, a_im
    for _ in range(1, q):
        bb.append(_cmul(pw_re, pw_im, *bb[0]))
        pw_re, pw_im = _cmul(pw_re, pw_im, a_re, a_im)
    aq_ref[0] = jnp.broadcast_to(jnp.concatenate([pw_re, pw_im], axis=1), aq_ref.shape[1:])
    wb = jnp.concatenate([jnp.concatenate(bb[q - 1 - s], axis=1) for s in range(q)], axis=0).astype(BF16)
    wb_ref[0] = wb

    sel_p = (iota((ns, p), 0) % p == iota((ns, p), 1)).astype(BF16)
    same_out = iota((ns, nc), 0) // p == iota((ns, nc), 1) // m
    c_re = jnp.where(same_out, _place_nt(sel_p, cre_ref[...].reshape(nc, p)), 0.0)
    c_im = jnp.where(same_out, _place_nt(sel_p, cim_ref[...].reshape(nc, p)), 0.0)
    eye_s = (iota((ns, ns), 0) == iota((ns, ns), 1)).astype(BF16)
    col_re = _place_nt(eye_s, jnp.broadcast_to(a_re, (nc, ns)))
    col_im = _place_nt(eye_s, jnp.broadcast_to(a_im, (nc, ns)))
    ca = [(c_re, c_im)]
    for _ in range(q):
        ca.append(_cmul(*ca[-1], col_re, col_im))
    wc_ref[0] = jnp.concatenate([jnp.concatenate([ca[k][0] for k in range(1, q + 1)], axis=1),
                                 jnp.concatenate([-ca[k][1] for k in range(1, q + 1)], axis=1)], axis=0).astype(BF16)

    wc0 = jnp.concatenate([c_re, -c_im], axis=0).astype(BF16)
    kd = [jnp.dot(wb[(q - 1 - d) * nc:(q - d) * nc], wc0, preferred_element_type=F32) for d in range(q)]
    zero = jnp.zeros_like(kd[0])
    wi_ref[0] = jnp.concatenate(
        [jnp.concatenate([kd[so - si] if so >= si else zero for so in range(q)], axis=1) for si in range(q)],
        axis=0).astype(BF16)


def _s5_weights(lam_re, lam_im, log_dt, b_re, b_im, c_re, c_im):
    g, p = lam_re.shape
    m = b_re.shape[-1]
    q = S5_RUN
    tg = S5_BLOCK_GROUPS
    nb = g // tg
    ns, nc = tg * p, tg * m
    row = lambda a: a.reshape(nb, 1, ns)
    row_spec = pl.BlockSpec((1, 1, ns), lambda i: (i, 0, 0))
    w_b, w_i, w_c, aq = pl.pallas_call(
        functools.partial(_s5_prep_kernel, q=q),
        out_shape=[jax.ShapeDtypeStruct((nb, q * nc, 2 * ns), BF16),
                   jax.ShapeDtypeStruct((nb, q * nc, q * nc), BF16),
                   jax.ShapeDtypeStruct((nb, 2 * ns, q * nc), BF16),
                   jax.ShapeDtypeStruct((nb, SUBLANES, 2 * ns), F32)],
        grid=(nb,),
        in_specs=[row_spec, row_spec, row_spec,
                  pl.BlockSpec((tg, p, m), lambda i: (i, 0, 0)), pl.BlockSpec((tg, p, m), lambda i: (i, 0, 0)),
                  pl.BlockSpec((tg, m, p), lambda i: (i, 0, 0)), pl.BlockSpec((tg, m, p), lambda i: (i, 0, 0))],
        out_specs=[pl.BlockSpec((1, q * nc, 2 * ns), lambda i: (i, 0, 0)),
                   pl.BlockSpec((1, q * nc, q * nc), lambda i: (i, 0, 0)),
                   pl.BlockSpec((1, 2 * ns, q * nc), lambda i: (i, 0, 0)),
                   pl.BlockSpec((1, SUBLANES, 2 * ns), lambda i: (i, 0, 0))],
        compiler_params=_params("parallel"),
        name="s5_prep",
    )(row(lam_re), row(lam_im), row(jnp.broadcast_to(log_dt[:, None], (g, p))), b_re, b_im, c_re, c_im)
    per_tile = lambda w: w.reshape((nb // S5_TILE_BLOCKS, S5_TILE_BLOCKS) + w.shape[1:])
    return per_tile(w_b), per_tile(w_i), per_tile(w_c), per_tile(aq)


def _modulated_norm_rows(x, nw, scale, shift):
    r, d = x.shape
    y = (_rms(x) * nw).reshape(r // SUBLANES, SUBLANES, d)
    return (y * (1.0 + scale)[None] + shift[None]).reshape(r, d)


def _sequence_rows(slab_ref, b, tmt, lo, hi):
    return jnp.concatenate([slab_ref[j, pl.ds(b, tmt, stride=SUBLANES), :]
                            for j in range(lo // LANES, hi // LANES)], axis=1)


def _s5_in_kernel(x_ref, nw_ref, sc_ref, sh_ref, w_ref, xt_ref, u_ref, z_ref, slab_ref):
    bsz, tmt, d = x_ref.shape
    for j in range(d // LANES):
        for b in range(bsz):
            slab_ref[j, pl.ds(b, tmt, stride=bsz), :] = x_ref[b, :, j * LANES:(j + 1) * LANES]
    x_tm = jnp.concatenate([slab_ref[j] for j in range(d // LANES)], axis=1)
    xt_ref[...] = x_tm
    h = _modulated_norm_rows(x_tm, nw_ref[...], sc_ref[...], sh_ref[...]).astype(BF16)
    e = z_ref.shape[1]
    nt, _, cw = u_ref.shape
    u = jnp.dot(h, w_ref[:, :e], preferred_element_type=F32).astype(BF16)
    for k in range(nt):
        u_ref[k] = u[:, k * cw:(k + 1) * cw]
    z_ref[...] = jnp.dot(h, w_ref[:, e:], preferred_element_type=F32).astype(BF16)


def _s5_in(x, norm_w, mod, layer, w_in):
    bsz, seqlen, d = x.shape
    e = w_in.shape[1] // 2
    tmt = min(ROW_TILE // bsz, seqlen)
    rows = tmt * bsz
    t = seqlen * bsz
    cw = S5_TILE_GROUPS * S5_GROUP
    nt = e // cw
    return pl.pallas_call(
        _s5_in_kernel,
        out_shape=[jax.ShapeDtypeStruct((t, d), F32), jax.ShapeDtypeStruct((nt, t, cw), BF16),
                   jax.ShapeDtypeStruct((t, e), BF16)],
        grid=(seqlen // tmt,),
        in_specs=[
            pl.BlockSpec((bsz, tmt, d), lambda i: (0, i, 0)),
            _resident((1, d), lambda i: (0, 0)),
            _resident((bsz, d), lambda i: (layer, MOD_SCALE)),
            _resident((bsz, d), lambda i: (layer, MOD_SHIFT)),
            _resident((d, 2 * e), lambda i: (0, 0)),
        ],
        out_specs=[pl.BlockSpec((rows, d), lambda i: (i, 0)), pl.BlockSpec((nt, rows, cw), lambda i: (0, i, 0)),
                   pl.BlockSpec((rows, e), lambda i: (i, 0))],
        scratch_shapes=[pltpu.VMEM((d // LANES, rows, LANES), F32)],
        compiler_params=_params("parallel"),
        name="s5_in",
    )(x, norm_w.reshape(1, d), mod, mod, w_in.astype(BF16))


def _s5_scan_kernel(u_ref, wb_ref, wi_ref, wc_ref, a_ref, d_ref, y_ref, sin_ref, xs_ref, st_ref, *, runs, chunks, q):
    @pl.when(pl.program_id(1) == 0)
    def _():
        st_ref[...] = jnp.zeros_like(st_ref)

    nblk = wb_ref.shape[1]
    bw = wi_ref.shape[2] // q
    half = wb_ref.shape[3] // 2
    nl = half // LANES
    rows_c = runs * SUBLANES
    a_re = [[a_ref[0, k, :, j * LANES:(j + 1) * LANES] for j in range(nl)] for k in range(nblk)]
    a_im = [[a_ref[0, k, :, half + j * LANES:half + (j + 1) * LANES] for j in range(nl)] for k in range(nblk)]
    x_re = [[st_ref[k, 0, :, j * LANES:(j + 1) * LANES] for j in range(nl)] for k in range(nblk)]
    x_im = [[st_ref[k, 1, :, j * LANES:(j + 1) * LANES] for j in range(nl)] for k in range(nblk)]
    intra = {}

    def project_in(ci):
        u = u_ref[0, ci * q * rows_c:(ci + 1) * q * rows_c, :].astype(F32).reshape(runs, q, SUBLANES, nblk * bw)
        us = [u[:, s].reshape(rows_c, nblk * bw) for s in range(q)]
        slot = ci % 2
        for k in range(nblk):
            uu = jnp.concatenate([us[s][:, k * bw:(k + 1) * bw] for s in range(q)], axis=1).astype(BF16)
            sin_ref[slot, k] = jnp.dot(uu, wb_ref[0, k], preferred_element_type=F32)
            intra[ci, k] = jnp.dot(uu, wi_ref[0, k], preferred_element_type=F32)

    def recur(ci):
        slot = ci % 2
        for pp in range(runs // 2):
            r0 = 2 * pp * SUBLANES
            for k in range(nblk):
                for j in range(nl):
                    lo, hi = j * LANES, (j + 1) * LANES
                    before_re, before_im = [], []
                    for s in range(2):
                        rr = r0 + s * SUBLANES
                        before_re.append(x_re[k][j])
                        before_im.append(x_im[k][j])
                        b_re = sin_ref[slot, k, rr:rr + SUBLANES, lo:hi]
                        b_im = sin_ref[slot, k, rr:rr + SUBLANES, half + lo:half + hi]
                        v_re = a_re[k][j] * x_re[k][j] - a_im[k][j] * x_im[k][j] + b_re
                        v_im = a_re[k][j] * x_im[k][j] + a_im[k][j] * x_re[k][j] + b_im
                        x_re[k][j], x_im[k][j] = v_re, v_im
                    xs_ref[slot, k, r0:r0 + 2 * SUBLANES, lo:hi] = jnp.concatenate(before_re, axis=0).astype(BF16)
                    xs_ref[slot, k, r0:r0 + 2 * SUBLANES, half + lo:half + hi] = (
                        jnp.concatenate(before_im, axis=0).astype(BF16))

    def project_out(ci):
        slot = ci % 2
        yk = [intra.pop((ci, k)) + jnp.dot(xs_ref[slot, k], wc_ref[0, k], preferred_element_type=F32)
              for k in range(nblk)]
        steps = [jnp.concatenate([yk[k][:, s * bw:(s + 1) * bw] for k in range(nblk)], axis=1)
                 .reshape(runs, 1, SUBLANES, nblk * bw) for s in range(q)]
        y = jnp.concatenate(steps, axis=1).reshape(q * rows_c, nblk * bw)
        u = u_ref[0, ci * q * rows_c:(ci + 1) * q * rows_c, :].astype(F32)
        y_ref[0, ci * q * rows_c:(ci + 1) * q * rows_c, :] = _gelu_tanh(y + d_ref[0] * u).astype(BF16)

    project_in(0)
    for ci in range(chunks):
        if ci + 1 < chunks:
            project_in(ci + 1)
        recur(ci)
        project_out(ci)

    for k in range(nblk):
        for j in range(nl):
            st_ref[k, 0, :, j * LANES:(j + 1) * LANES] = x_re[k][j]
            st_ref[k, 1, :, j * LANES:(j + 1) * LANES] = x_im[k][j]


def _s5_scan(u_tiles, w_b, w_i, w_c, aq, d_skip, runs, chunks):
    nt, nblk, kw, sw = w_b.shape
    q = S5_RUN
    cw = u_tiles.shape[2]
    t = u_tiles.shape[1]
    rows_c = runs * SUBLANES
    rows = q * rows_c * chunks
    return pl.pallas_call(
        functools.partial(_s5_scan_kernel, runs=runs, chunks=chunks, q=q),
        out_shape=jax.ShapeDtypeStruct((nt, t, cw), BF16),
        grid=(nt, t // rows),
        in_specs=[
            pl.BlockSpec((1, rows, cw), lambda k, c: (k, c, 0)),
            _resident((1, nblk, kw, sw), lambda k, c: (k, 0, 0, 0)),
            _resident((1, nblk, kw, kw), lambda k, c: (k, 0, 0, 0)),
            _resident((1, nblk, sw, kw), lambda k, c: (k, 0, 0, 0)),
            _resident((1, nblk, SUBLANES, sw), lambda k, c: (k, 0, 0, 0)),
            _resident((1, 1, cw), lambda k, c: (k, 0, 0)),
        ],
        out_specs=pl.BlockSpec((1, rows, cw), lambda k, c: (k, c, 0)),
        scratch_shapes=[
            pltpu.VMEM((2, nblk, rows_c, sw), F32),
            pltpu.VMEM((2, nblk, rows_c, sw), BF16),
            pltpu.VMEM((nblk, 2, SUBLANES, sw // 2), F32),
        ],
        compiler_params=_params("parallel", "arbitrary"),
        name="s5_scan",
    )(u_tiles, w_b, w_i, w_c, aq, d_skip.reshape(nt, 1, cw))


def _s5_out_kernel(y_ref, z_ref, x_ref, gate_ref, wg_ref, wo_ref, o_ref):
    y = jnp.concatenate([y_ref[k] for k in range(y_ref.shape[0])], axis=1)
    yf = y.astype(F32)
    y2 = yf * _sigmoid(jnp.dot(y, wg_ref[...], preferred_element_type=F32)) * _silu(z_ref[...].astype(F32))
    out = jnp.dot(y2.astype(BF16), wo_ref[...], preferred_element_type=F32)
    r, d = out.shape
    x = x_ref[...].reshape(r // SUBLANES, SUBLANES, d)
    o_ref[...] = (x + gate_ref[...][None] * out.reshape(r // SUBLANES, SUBLANES, d)).reshape(r, d)


def _s5_out(y, z, x_tm, mod, layer, w_glu, w_out):
    t, d = x_tm.shape
    e = w_glu.shape[0]
    nt, _, cw = y.shape
    rows = min(ROW_TILE, t)
    row_spec = lambda c: pl.BlockSpec((rows, c), lambda i: (i, 0))
    return pl.pallas_call(
        _s5_out_kernel,
        out_shape=jax.ShapeDtypeStruct((t, d), F32),
        grid=(t // rows,),
        in_specs=[
            pl.BlockSpec((nt, rows, cw), lambda i: (0, i, 0)), row_spec(e), row_spec(d),
            _resident((SUBLANES, d), lambda i: (layer, MOD_GATE)),
            _resident((e, e), lambda i: (0, 0)),
            _resident((e, d), lambda i: (0, 0)),
        ],
        out_specs=row_spec(d),
        compiler_params=_params("parallel"),
        name="s5_out",
    )(y, z, x_tm, mod, w_glu.astype(BF16), w_out.astype(BF16))


def _gdn_in_kernel(x_ref, nw_ref, sc_ref, sh_ref, w_ref, wg_ref, cw_ref, gp_ref,
                   q_ref, k_ref, v_ref, z_ref, g_ref, ext_ref, st_ref, *, heads):
    halo = (GDN_CONV - 1) * SUBLANES
    r, d = x_ref.shape
    cc = ext_ref.shape[1]
    bsz = SUBLANES
    tmt = r // bsz
    qk = q_ref.shape[1] // bsz
    e = v_ref.shape[1] // bsz
    dk = qk // heads

    @pl.when(pl.program_id(0) == 0)
    def _():
        ext_ref[0:halo, :] = jnp.zeros((halo, cc), F32)

    h = _modulated_norm_rows(x_ref[...], nw_ref[...], sc_ref[...], sh_ref[...]).astype(BF16)

    ext_ref[halo:halo + r, :] = jnp.dot(h, w_ref[:, :cc], preferred_element_type=F32)
    conv = cw_ref[0:1, :] * ext_ref[0:r, :]
    for j in range(1, GDN_CONV):
        conv = conv + cw_ref[j:j + 1, :] * ext_ref[j * SUBLANES:j * SUBLANES + r, :]
    ext_ref[0:halo, :] = ext_ref[r:r + halo, :]
    act = _silu(conv)

    def stage(lo, val):
        for j in range(val.shape[1] // LANES):
            st_ref[lo // LANES + j] = val[:, j * LANES:(j + 1) * LANES]

    for hh in range(heads):
        qh = act[:, hh * dk:(hh + 1) * dk]
        kh = act[:, qk + hh * dk:qk + (hh + 1) * dk]
        stage(hh * dk, qh * lax.rsqrt(jnp.sum(qh * qh, axis=-1, keepdims=True) + NORM_EPS) * (dk ** -0.5))
        stage(qk + hh * dk, kh * lax.rsqrt(jnp.sum(kh * kh, axis=-1, keepdims=True) + NORM_EPS))
    stage(2 * qk, act[:, 2 * qk:])
    stage(cc, jnp.dot(h, w_ref[:, cc:], preferred_element_type=F32))

    logit = jnp.dot(h, wg_ref[...], preferred_element_type=F32)
    beta = _sigmoid(logit)
    g = -jnp.exp(gp_ref[0:1, :]) * _softplus(logit + gp_ref[1:2, :])
    lane = lax.broadcasted_iota(jnp.int32, logit.shape, 1)
    stage(cc + e, jnp.where(lane < heads, beta, g))

    for b in range(bsz):
        seq = lambda lo, hi: _sequence_rows(st_ref, b, tmt, lo, hi)
        q_ref[:, b * qk:(b + 1) * qk] = seq(0, qk).astype(BF16)
        k_ref[:, b * qk:(b + 1) * qk] = seq(qk, 2 * qk).astype(BF16)
        v_ref[:, b * e:(b + 1) * e] = seq(2 * qk, cc).astype(BF16)
        z_ref[:, b * e:(b + 1) * e] = seq(cc, cc + e).astype(BF16)
        g_ref[:, b * LANES:(b + 1) * LANES] = seq(cc + e, cc + e + LANES)


def _gdn_in(x_tm, norm_w, mod, layer, w_in, conv_w, a_log, dt_bias):
    t, d = x_tm.shape
    bsz = SUBLANES
    seqlen = t // bsz
    heads = a_log.shape[0]
    cc = conv_w.shape[1]
    e = (w_in.shape[1] - cc - 2 * heads)
    qk = (cc - e) // 2
    rows = min(GDN_IN_ROWS, t)
    tmt = rows // bsz
    w_main = w_in[:, :cc + e].astype(BF16)
    w_gate = jnp.pad(w_in[:, cc + e:], ((0, 0), (0, LANES - 2 * heads))).astype(BF16)
    gate_par = jnp.pad(jnp.stack([a_log, dt_bias]), ((0, 0), (heads, LANES - 2 * heads)))
    seq_spec = lambda c: pl.BlockSpec((tmt, bsz * c), lambda i: (i, 0))
    return pl.pallas_call(
        functools.partial(_gdn_in_kernel, heads=heads),
        out_shape=[
            jax.ShapeDtypeStruct((seqlen, bsz * qk), BF16),
            jax.ShapeDtypeStruct((seqlen, bsz * qk), BF16),
            jax.ShapeDtypeStruct((seqlen, bsz * e), BF16),
            jax.ShapeDtypeStruct((seqlen, bsz * e), BF16),
            jax.ShapeDtypeStruct((seqlen, bsz * LANES), F32),
        ],
        grid=(t // rows,),
        in_specs=[
            pl.BlockSpec((rows, d), lambda i: (i, 0)),
            _resident((1, d), lambda i: (0, 0)),
            _resident((SUBLANES, d), lambda i: (layer, MOD_SCALE)),
            _resident((SUBLANES, d), lambda i: (layer, MOD_SHIFT)),
            _resident((d, cc + e), lambda i: (0, 0)),
            _resident((d, LANES), lambda i: (0, 0)),
            _resident((GDN_CONV, cc), lambda i: (0, 0)),
            _resident((2, LANES), lambda i: (0, 0)),
        ],
        out_specs=[seq_spec(qk), seq_spec(qk), seq_spec(e), seq_spec(e), seq_spec(LANES)],
        scratch_shapes=[
            pltpu.VMEM((rows + (GDN_CONV - 1) * SUBLANES, cc), F32),
            pltpu.VMEM(((cc + e + LANES) // LANES, rows, LANES), F32),
        ],
        compiler_params=_params("arbitrary"),
        name="gdn_in",
    )(x_tm, norm_w.reshape(1, d), mod, mod, w_main, w_gate, conv_w, gate_par)


def _pair_block_diag(x):
    c = x.shape[0]
    lane = lax.broadcasted_iota(jnp.int32, x.shape, 1)
    zero = jnp.zeros_like(x)
    return jnp.concatenate([jnp.where(lane < c, x, zero), jnp.where(lane >= c, x, zero)], axis=0)


def _pair_mm(x, y):
    return jnp.dot(x.astype(BF16), _pair_block_diag(y.astype(BF16)), preferred_element_type=F32)


def _gdn_core_kernel(q_ref, k_ref, v_ref, g_ref, nw_ref, y_ref, s_ref, *, heads, nb):
    @pl.when(pl.program_id(1) == 0)
    def _():
        s_ref[...] = jnp.zeros_like(s_ref)

    c = q_ref.shape[0]
    qk_w = q_ref.shape[1] // nb
    e_w = v_ref.shape[1] // nb
    dk = qk_w // heads
    dv = e_w // heads
    units = [(b, p) for b in range(nb) for p in range(heads // 2)]

    row = lax.broadcasted_iota(jnp.int32, (c, 2 * c), 0)
    col = lax.broadcasted_iota(jnp.int32, (c, 2 * c), 1) & (c - 1)
    causal = row >= col
    strict = row > col
    eye = (row == col).astype(F32)
    row1 = lax.broadcasted_iota(jnp.int32, (c, c), 0)
    col1 = lax.broadcasted_iota(jnp.int32, (c, c), 1)
    tri = (row1 >= col1).astype(BF16)
    nh2 = 2 * heads
    eye_h = (lax.broadcasted_iota(jnp.int32, (nh2, nh2), 0)
             == lax.broadcasted_iota(jnp.int32, (nh2, nh2), 1)).astype(BF16)

    def same_block(shift):
        return (row >> shift) == (col >> shift)

    gb, gc, gc_t = [], [], []
    for b in range(nb):
        g = g_ref[:, b * LANES:b * LANES + nh2]
        cs = sum(jnp.dot(tri, part, preferred_element_type=F32) for part in _split3(g))
        gb.append(g)
        gc.append(cs)
        gc_t.append(sum(lax.dot_general(eye_h, part, (((1,), (1,)), ((), ())), preferred_element_type=F32)
                        for part in _split3(cs)))

    def lanes2(ref, b, p, width, per_batch):
        return ref[:, b * per_batch + 2 * p * width:b * per_batch + 2 * (p + 1) * width]

    def pair_cols(b, p, fn, width):
        return jnp.concatenate([jnp.broadcast_to(fn(b, 2 * p + i), (c, width)) for i in range(2)], axis=1)

    beta_c = lambda b, h: gb[b][:, h:h + 1]
    gcol = lambda b, h: gc[b][:, heads + h:heads + h + 1]
    grow = lambda b, h: gc_t[b][heads + h:heads + h + 1, :]

    a_mat, qk_mat = {}, {}
    for (b, p) in units:
        qp = lanes2(q_ref, b, p, dk, qk_w)
        kp = lanes2(k_ref, b, p, dk, qk_w)
        res = lax.dot_general(jnp.concatenate([qp, kp], axis=0), _pair_block_diag(kp),
                              (((1,), (1,)), ((), ())), preferred_element_type=F32)
        decay = jnp.where(causal, jnp.exp(jnp.concatenate(
            [gcol(b, 2 * p + i) - grow(b, 2 * p + i) for i in range(2)], axis=1)), 0.0)
        qk_mat[b, p] = res[:c] * decay
        a_mat[b, p] = jnp.where(strict, pair_cols(b, p, beta_c, c) * res[c:] * decay, 0.0)

    shift0 = GDN_BASE_BLOCK.bit_length() - 1
    dblk = {u: jnp.where(same_block(shift0), a_mat[u], 0.0) for u in units}
    d2 = {u: _pair_mm(dblk[u], dblk[u]) for u in units}
    d4 = {u: _pair_mm(d2[u], d2[u]) for u in units}
    t_inv = {u: eye - dblk[u] for u in units}
    for pw in (d2, d4):
        upd = {u: _pair_mm(t_inv[u], pw[u]) for u in units}
        t_inv = {u: t_inv[u] + upd[u] for u in units}
    shift = shift0
    while (1 << shift) < c:
        level = same_block(shift + 1) & jnp.logical_not(same_block(shift))
        xt = {u: _pair_mm(jnp.where(level, a_mat[u], 0.0), t_inv[u]) for u in units}
        upd = {u: _pair_mm(t_inv[u], xt[u]) for u in units}
        t_inv = {u: t_inv[u] - upd[u] for u in units}
        shift += 1

    w_pk, u_pk, gam_pk = {}, {}, {}
    for (b, p) in units:
        gam_pk[b, p] = jnp.exp(pair_cols(b, p, gcol, dk))
        kp = lanes2(k_ref, b, p, dk, qk_w).astype(F32)
        vp = lanes2(v_ref, b, p, dv, e_w).astype(F32)
        rw = (pair_cols(b, p, beta_c, dk) * gam_pk[b, p] * kp).astype(BF16)
        ru = (pair_cols(b, p, beta_c, dv) * vp).astype(BF16)
        zk = jnp.zeros((c, dk), BF16)
        zv = jnp.zeros((c, dv), BF16)
        rhs = jnp.concatenate([
            jnp.concatenate([rw[:, :dk], zk, ru[:, :dv], zv], axis=1),
            jnp.concatenate([zk, rw[:, dk:], zv, ru[:, dv:]], axis=1)], axis=0)
        wu = jnp.dot(t_inv[b, p].astype(BF16), rhs, preferred_element_type=F32)
        w_pk[b, p] = wu[:, :2 * dk]
        u_pk[b, p] = wu[:, 2 * dk:]

    hunits = [(b, h) for b in range(nb) for h in range(heads)]
    ws, states = {}, {}
    for (b, h) in hunits:
        p, i = divmod(h, 2)
        qd = lanes2(q_ref, b, p, dk, qk_w).astype(F32) * gam_pk[b, p]
        lhs = jnp.concatenate([w_pk[b, p][:, i * dk:(i + 1) * dk], qd[:, i * dk:(i + 1) * dk]], axis=0)
        states[b, h] = s_ref[b * heads + h]
        ws[b, h] = _bdot(lhs, states[b, h])
    v_new = {(b, h): u_pk[b, h // 2][:, (h % 2) * dv:(h % 2 + 1) * dv] - ws[b, h][:c] for (b, h) in hunits}
    o_intra = {(b, h): _bdot(qk_mat[b, h // 2][:, (h % 2) * c:(h % 2 + 1) * c], v_new[b, h]) for (b, h) in hunits}
    for (b, h) in hunits:
        g_last = gcol(b, h)[c - 1:c, :]
        kh = k_ref[:, b * qk_w + h * dk:b * qk_w + (h + 1) * dk].astype(F32)
        k_dec = kh * jnp.exp(g_last - gcol(b, h))
        s_ref[b * heads + h] = jnp.exp(g_last) * states[b, h] + _bdot_tn(k_dec, v_new[b, h])
    for (b, h) in hunits:
        o = ws[b, h][c:] + o_intra[b, h]
        y_ref[:, b * e_w + h * dv:b * e_w + (h + 1) * dv] = (_rms(o) * nw_ref[...]).astype(BF16)


def _gdn_core(q, k, v, gates, norm_w, heads):
    seqlen = q.shape[0]
    nb = GDN_BATCH_PER_STEP
    bsz = gates.shape[1] // LANES
    qk = q.shape[1] // bsz
    e = v.shape[1] // bsz
    c = min(GDN_CHUNK, seqlen)
    col_spec = lambda w: pl.BlockSpec((c, nb * w), lambda b, n: (n, b))
    return pl.pallas_call(
        functools.partial(_gdn_core_kernel, heads=heads, nb=nb),
        out_shape=jax.ShapeDtypeStruct((seqlen, bsz * e), BF16),
        grid=(bsz // nb, seqlen // c),
        in_specs=[col_spec(qk), col_spec(qk), col_spec(e), col_spec(LANES),
                  _resident((1, e // heads), lambda b, n: (0, 0))],
        out_specs=col_spec(e),
        scratch_shapes=[pltpu.VMEM((nb * heads, qk // heads, e // heads), F32)],
        compiler_params=_params("parallel", "arbitrary"),
        name="gdn_core",
    )(q, k, v, gates, norm_w.reshape(1, e // heads))


def _gdn_out_kernel(y_ref, z_ref, x_ref, gate_ref, w_ref, fw_ref, o_ref, slab_ref):
    bsz, tmt, d = o_ref.shape
    e = y_ref.shape[1] // bsz
    for j in range(d // LANES):
        slab_ref[j] = x_ref[:, j * LANES:(j + 1) * LANES]
    y = jnp.concatenate([(y_ref[:, b * e:(b + 1) * e].astype(F32)
                          * _silu(z_ref[:, b * e:(b + 1) * e].astype(F32))).astype(BF16)
                         for b in range(bsz)], axis=0)
    out = jnp.dot(y, w_ref[...], preferred_element_type=F32).reshape(bsz, tmt, d)
    for b in range(bsz):
        xn = _sequence_rows(slab_ref, b, tmt, 0, d) + gate_ref[b:b + 1, :] * out[b]
        o_ref[b] = _rms(xn) * fw_ref[...]


def _gdn_out(y, z, x_tm, mod, layer, w_out, final_w):
    e, d = w_out.shape
    seqlen = y.shape[0]
    bsz = y.shape[1] // e
    tmt = min(ROW_TILE // bsz, seqlen)
    return pl.pallas_call(
        _gdn_out_kernel,
        out_shape=jax.ShapeDtypeStruct((bsz, seqlen, d), F32),
        grid=(seqlen // tmt,),
        in_specs=[
            pl.BlockSpec((tmt, bsz * e), lambda i: (i, 0)),
            pl.BlockSpec((tmt, bsz * e), lambda i: (i, 0)),
            pl.BlockSpec((tmt * bsz, d), lambda i: (i, 0)),
            _resident((bsz, d), lambda i: (layer, MOD_GATE)),
            _resident((e, d), lambda i: (0, 0)),
            _resident((1, d), lambda i: (0, 0)),
        ],
        out_specs=pl.BlockSpec((bsz, tmt, d), lambda i: (0, i, 0)),
        scratch_shapes=[pltpu.VMEM((d // LANES, tmt * bsz, LANES), F32)],
        compiler_params=_params("parallel"),
        name="gdn_out",
    )(y, z, x_tm, mod, w_out.astype(BF16), final_w.reshape(1, d))


def kernel(x, c, ada_w, ada_b, norm_w, s5_w_in, s5_lambda_re, s5_lambda_im, s5_log_dt, s5_b_re, s5_b_im,
           s5_c_re, s5_c_im, s5_d, s5_w_glu, s5_w_out, gdn_w_in, gdn_conv_w, gdn_a_log, gdn_dt_bias,
           gdn_norm_w, gdn_w_out, final_norm_w):
    bsz, seqlen, d = x.shape
    assert bsz == SUBLANES, "the time-major layout keeps one batch row per vector sublane"
    assert ada_w.shape[0] == 2 and s5_w_in.shape[0] == 1 and gdn_w_in.shape[0] == 1

    mod = _modulation(c, ada_w, ada_b)

    w_b, w_i, w_c, aq = _s5_weights(s5_lambda_re[0], s5_lambda_im[0], s5_log_dt[0], s5_b_re[0], s5_b_im[0],
                                    s5_c_re[0], s5_c_im[0])
    x_tm, u, z = _s5_in(x, norm_w[0], mod, 0, s5_w_in[0])
    runs = min(S5_RUNS, seqlen // S5_RUN)
    chunks = min(S5_CHUNKS, seqlen // (S5_RUN * runs))
    y = _s5_scan(u, w_b, w_i, w_c, aq, s5_d[0], runs, chunks)
    x1_tm = _s5_out(y, z, x_tm, mod, 0, s5_w_glu[0], s5_w_out[0])

    heads = gdn_a_log.shape[1]
    q, k, v, zg, gates = _gdn_in(x1_tm, norm_w[1], mod, 1, gdn_w_in[0], gdn_conv_w[0],
                                 gdn_a_log[0], gdn_dt_bias[0])
    yg = _gdn_core(q, k, v, gates, gdn_norm_w[0], heads)
    return _gdn_out(yg, zg, x1_tm, mod, 1, gdn_w_out[0], final_norm_w)
```

```python
import functools
import math

import jax
import jax.numpy as jnp
from jax import lax
from jax.experimental import pallas as pl
from jax.experimental.pallas import tpu as pltpu

F32 = jnp.float32
BF16 = jnp.bfloat16

NORM_EPS = 1e-6
S5_GROUP = 16
S5_STATE = 64
S5_TILE_GROUPS = 16
S5_RUN = 4
S5_BLOCK_GROUPS = S5_TILE_GROUPS // S5_RUN
S5_TILE_BLOCKS = S5_TILE_GROUPS // S5_BLOCK_GROUPS
S5_RUNS = 64
S5_CHUNKS = 4
GDN_HEADS = 8
GDN_CONV = 4
GDN_CHUNK = 128
GDN_BASE_BLOCK = 8
GDN_BATCH_PER_STEP = 4
ROW_TILE = 512
GDN_IN_ROWS = 256
SUBLANES = 8
LANES = 128
VMEM_LIMIT_BYTES = 56 * 1024 * 1024


def _sigmoid(x):
    return 1.0 / (1.0 + jnp.exp(-x))


def _silu(x):
    return x * _sigmoid(x)


def _gelu_tanh(x):
    c = math.sqrt(2.0 / math.pi)
    return 0.5 * x * (1.0 + jnp.tanh(c * (x + 0.044715 * (x * x * x))))


def _softplus(x):
    return jnp.maximum(x, 0.0) + jnp.log(1.0 + jnp.exp(-jnp.abs(x)))


def _rms(x):
    return x * lax.rsqrt(jnp.mean(x * x, axis=-1, keepdims=True) + NORM_EPS)


def _bdot(a, b):
    return jnp.dot(a.astype(BF16), b.astype(BF16), preferred_element_type=F32)


def _bdot_nt(a, b):
    return lax.dot_general(a.astype(BF16), b.astype(BF16), (((1,), (1,)), ((), ())),
                           preferred_element_type=F32)


def _bdot_tn(a, b):
    return lax.dot_general(a.astype(BF16), b.astype(BF16), (((0,), (0,)), ((), ())),
                           preferred_element_type=F32)


def _split3(x):
    hi = x.astype(BF16)
    r = x - hi.astype(F32)
    mid = r.astype(BF16)
    lo = (r - mid.astype(F32)).astype(BF16)
    return hi, mid, lo


def _params(*sem):
    return pltpu.CompilerParams(dimension_semantics=sem, vmem_limit_bytes=VMEM_LIMIT_BYTES)


def _resident(shape, index_map):
    return pl.BlockSpec(shape, index_map, pipeline_mode=pl.Buffered(1))


MOD_SHIFT, MOD_SCALE, MOD_GATE = 0, 1, 2


def _mod_kernel(c_ref, w_ref, b_ref, o_ref):
    o_ref[...] = _bdot(_silu(c_ref[...]), w_ref[0]) + b_ref[0]


def _modulation(c, ada_w, ada_b):
    depth, d, d3 = ada_w.shape
    bsz = c.shape[0]
    nb = d3 // d
    return pl.pallas_call(
        _mod_kernel,
        out_shape=jax.ShapeDtypeStruct((depth * bsz, d3), F32),
        grid=(depth, nb),
        in_specs=[
            pl.BlockSpec((bsz, d), lambda l, j: (0, 0)),
            pl.BlockSpec((1, d, d), lambda l, j: (l, 0, j)),
            pl.BlockSpec((1, 1, d), lambda l, j: (l, 0, j)),
        ],
        out_specs=pl.BlockSpec((bsz, d), lambda l, j: (l, j)),
        compiler_params=_params("parallel", "parallel"),
        name="adaln_mod",
    )(c, ada_w.astype(BF16), ada_b.reshape(depth, 1, d3))


def _cmul(xr, xi, yr, yi):
    return xr * yr - xi * yi, xr * yi + xi * yr


def _place_nt(sel, vals):
    return sum(lax.dot_general(sel, part, (((1,), (1,)), ((), ())), preferred_element_type=F32)
               for part in _split3(vals))


def _s5_prep_kernel(lre_ref, lim_ref, ldt_ref, bre_ref, bim_ref, cre_ref, cim_ref, wb_ref, wi_ref, wc_ref, aq_ref,
                    *, q):
    tg, p, m = bre_ref.shape
    ns, nc = tg * p, tg * m
    lre, lim = lre_ref[0], lim_ref[0]
    dt = jnp.exp(ldt_ref[0])
    mag = jnp.exp(lre * dt)
    a_re = mag * jnp.cos(lim * dt)
    a_im = mag * jnp.sin(lim * dt)
    den = lre * lre + lim * lim
    nr = a_re - 1.0
    q_re = (nr * lre + a_im * lim) / den
    q_im = (a_im * lre - nr * lim) / den

    def iota(shape, axis):
        return lax.broadcasted_iota(jnp.int32, shape, axis)

    sel_n = (iota((nc, m), 0) % m == iota((nc, m), 1)).astype(BF16)
    zb_re = _place_nt(sel_n, bre_ref[...].reshape(ns, m))
    zb_im = _place_nt(sel_n, bim_ref[...].reshape(ns, m))
    same_in = iota((nc, ns), 0) // m == iota((nc, ns), 1) // p
    bb_re, bb_im = _cmul(q_re, q_im, zb_re, zb_im)
    bb = [(jnp.where(same_in, bb_re, 0.0), jnp.where(same_in, bb_im, 0.0))]
    pw_re, pw_im = a_re, a_im
    for _ in range(1, q):
        bb.append(_cmul(pw_re, pw_im, *bb[0]))
        pw_re, pw_im = _cmul(pw_re, pw_im, a_re, a_im)
    aq_ref[0] = jnp.broadcast_to(jnp.concatenate([pw_re, pw_im], axis=1), aq_ref.shape[1:])
    wb = jnp.concatenate([jnp.concatenate(bb[q - 1 - s], axis=1) for s in range(q)], axis=0).astype(BF16)
    wb_ref[0] = wb

    sel_p = (iota((ns, p), 0) % p == iota((ns, p), 1)).astype(BF16)
    same_out = iota((ns, nc), 0) // p == iota((ns, nc), 1) // m
    c_re = jnp.where(same_out, _place_nt(sel_p, cre_ref[...].reshape(nc, p)), 0.0)
    c_im = jnp.where(same_out, _place_nt(sel_p, cim_ref[...].reshape(nc, p)), 0.0)
    eye_s = (iota((ns, ns), 0) == iota((ns, ns), 1)).astype(BF16)
    col_re = _place_nt(eye_s, jnp.broadcast_to(a_re, (nc, ns)))
    col_im = _place_nt(eye_s, jnp.broadcast_to(a_im, (nc, ns)))
    ca = [(c_re, c_im)]
    for _ in range(q):
        ca.append(_cmul(*ca[-1], col_re, col_im))
    wc_ref[0] = jnp.concatenate([jnp.concatenate([ca[k][0] for k in range(1, q + 1)], axis=1),
                                 jnp.concatenate([-ca[k][1] for k in range(1, q + 1)], axis=1)], axis=0).astype(BF16)

    wc0 = jnp.concatenate([c_re, -c_im], axis=0).astype(BF16)
    kd = [jnp.dot(wb[(q - 1 - d) * nc:(q - d) * nc], wc0, preferred_element_type=F32) for d in range(q)]
    zero = jnp.zeros_like(kd[0])
    wi_ref[0] = jnp.concatenate(
        [jnp.concatenate([kd[so - si] if so >= si else zero for so in range(q)], axis=1) for si in range(q)],
        axis=0).astype(BF16)


def _s5_weights(lam_re, lam_im, log_dt, b_re, b_im, c_re, c_im):
    g, p = lam_re.shape
    m = b_re.shape[-1]
    q = S5_RUN
    tg = S5_BLOCK_GROUPS
    nb = g // tg
    ns, nc = tg * p, tg * m
    row = lambda a: a.reshape(nb, 1, ns)
    row_spec = pl.BlockSpec((1, 1, ns), lambda i: (i, 0, 0))
    w_b, w_i, w_c, aq = pl.pallas_call(
        functools.partial(_s5_prep_kernel, q=q),
        out_shape=[jax.ShapeDtypeStruct((nb, q * nc, 2 * ns), BF16),
                   jax.ShapeDtypeStruct((nb, q * nc, q * nc), BF16),
                   jax.ShapeDtypeStruct((nb, 2 * ns, q * nc), BF16),
                   jax.ShapeDtypeStruct((nb, SUBLANES, 2 * ns), F32)],
        grid=(nb,),
        in_specs=[row_spec, row_spec, row_spec,
                  pl.BlockSpec((tg, p, m), lambda i: (i, 0, 0)), pl.BlockSpec((tg, p, m), lambda i: (i, 0, 0)),
                  pl.BlockSpec((tg, m, p), lambda i: (i, 0, 0)), pl.BlockSpec((tg, m, p), lambda i: (i, 0, 0))],
        out_specs=[pl.BlockSpec((1, q * nc, 2 * ns), lambda i: (i, 0, 0)),
                   pl.BlockSpec((1, q * nc, q * nc), lambda i: (i, 0, 0)),
                   pl.BlockSpec((1, 2 * ns, q * nc), lambda i: (i, 0, 0)),
                   pl.BlockSpec((1, SUBLANES, 2 * ns), lambda i: (i, 0, 0))],
        compiler_params=_params("parallel"),
        name="s5_prep",
    )(row(lam_re), row(lam_im), row(jnp.broadcast_to(log_dt[:, None], (g, p))), b_re, b_im, c_re, c_im)
    per_tile = lambda w: w.reshape((nb // S5_TILE_BLOCKS, S5_TILE_BLOCKS) + w.shape[1:])
    return per_tile(w_b), per_tile(w_i), per_tile(w_c), per_tile(aq)


def _modulated_norm_rows(x, nw, scale, shift):
    r, d = x.shape
    y = (_rms(x) * nw).reshape(r // SUBLANES, SUBLANES, d)
    return (y * (1.0 + scale)[None] + shift[None]).reshape(r, d)


def _sequence_rows(slab_ref, b, tmt, lo, hi):
    return jnp.concatenate([slab_ref[j, pl.ds(b, tmt, stride=SUBLANES), :]
                            for j in range(lo // LANES, hi // LANES)], axis=1)


def _s5_in_kernel(x_ref, nw_ref, sc_ref, sh_ref, w_ref, xt_ref, u_ref, z_ref, slab_ref):
    bsz, tmt, d = x_ref.shape
    for j in range(d // LANES):
        for b in range(bsz):
            slab_ref[j, pl.ds(b, tmt, stride=bsz), :] = x_ref[b, :, j * LANES:(j + 1) * LANES]
    x_tm = jnp.concatenate([slab_ref[j] for j in range(d // LANES)], axis=1)
    xt_ref[...] = x_tm
    h = _modulated_norm_rows(x_tm, nw_ref[...], sc_ref[...], sh_ref[...]).astype(BF16)
    e = z_ref.shape[1]
    nt, _, cw = u_ref.shape
    u = jnp.dot(h, w_ref[:, :e], preferred_element_type=F32).astype(BF16)
    for k in range(nt):
        u_ref[k] = u[:, k * cw:(k + 1) * cw]
    z_ref[...] = jnp.dot(h, w_ref[:, e:], preferred_element_type=F32).astype(BF16)


def _s5_in(x, norm_w, mod, layer, w_in):
    bsz, seqlen, d = x.shape
    e = w_in.shape[1] // 2
    tmt = min(ROW_TILE // bsz, seqlen)
    rows = tmt * bsz
    t = seqlen * bsz
    cw = S5_TILE_GROUPS * S5_GROUP
    nt = e // cw
    return pl.pallas_call(
        _s5_in_kernel,
        out_shape=[jax.ShapeDtypeStruct((t, d), F32), jax.ShapeDtypeStruct((nt, t, cw), BF16),
                   jax.ShapeDtypeStruct((t, e), BF16)],
        grid=(seqlen // tmt,),
        in_specs=[
            pl.BlockSpec((bsz, tmt, d), lambda i: (0, i, 0)),
            _resident((1, d), lambda i: (0, 0)),
            _resident((bsz, d), lambda i: (layer, MOD_SCALE)),
            _resident((bsz, d), lambda i: (layer, MOD_SHIFT)),
            _resident((d, 2 * e), lambda i: (0, 0)),
        ],
        out_specs=[pl.BlockSpec((rows, d), lambda i: (i, 0)), pl.BlockSpec((nt, rows, cw), lambda i: (0, i, 0)),
                   pl.BlockSpec((rows, e), lambda i: (i, 0))],
        scratch_shapes=[pltpu.VMEM((d // LANES, rows, LANES), F32)],
        compiler_params=_params("parallel"),
        name="s5_in",
    )(x, norm_w.reshape(1, d), mod, mod, w_in.astype(BF16))


def _s5_scan_kernel(u_ref, wb_ref, wi_ref, wc_ref, a_ref, d_ref, y_ref, sin_ref, xs_ref, st_ref, *, runs, chunks, q):
    @pl.when(pl.program_id(1) == 0)
    def _():
        st_ref[...] = jnp.zeros_like(st_ref)

    nblk = wb_ref.shape[1]
    bw = wi_ref.shape[2] // q
    half = wb_ref.shape[3] // 2
    nl = half // LANES
    rows_c = runs * SUBLANES
    a_re = [[a_ref[0, k, :, j * LANES:(j + 1) * LANES] for j in range(nl)] for k in range(nblk)]
    a_im = [[a_ref[0, k, :, half + j * LANES:half + (j + 1) * LANES] for j in range(nl)] for k in range(nblk)]
    x_re = [[st_ref[k, 0, :, j * LANES:(j + 1) * LANES] for j in range(nl)] for k in range(nblk)]
    x_im = [[st_ref[k, 1, :, j * LANES:(j + 1) * LANES] for j in range(nl)] for k in range(nblk)]
    intra = {}

    def project_in(ci):
        u = u_ref[0, ci * q * rows_c:(ci + 1) * q * rows_c, :].astype(F32).reshape(runs, q, SUBLANES, nblk * bw)
        us = [u[:, s].reshape(rows_c, nblk * bw) for s in range(q)]
        slot = ci % 2
        for k in range(nblk):
            uu = jnp.concatenate([us[s][:, k * bw:(k + 1) * bw] for s in range(q)], axis=1).astype(BF16)
            sin_ref[slot, k] = jnp.dot(uu, wb_ref[0, k], preferred_element_type=F32)
            intra[ci, k] = jnp.dot(uu, wi_ref[0, k], preferred_element_type=F32)

    def recur(ci):
        slot = ci % 2
        for pp in range(runs // 2):
            r0 = 2 * pp * SUBLANES
            for k in range(nblk):
                for j in range(nl):
                    lo, hi = j * LANES, (j + 1) * LANES
                    before_re, before_im = [], []
                    for s in range(2):
                        rr = r0 + s * SUBLANES
                        before_re.append(x_re[k][j])
                        before_im.append(x_im[k][j])
                        b_re = sin_ref[slot, k, rr:rr + SUBLANES, lo:hi]
                        b_im = sin_ref[slot, k, rr:rr + SUBLANES, half + lo:half + hi]
                        v_re = a_re[k][j] * x_re[k][j] - a_im[k][j] * x_im[k][j] + b_re
                        v_im = a_re[k][j] * x_im[k][j] + a_im[k][j] * x_re[k][j] + b_im
                        x_re[k][j], x_im[k][j] = v_re, v_im
                    xs_ref[slot, k, r0:r0 + 2 * SUBLANES, lo:hi] = jnp.concatenate(before_re, axis=0).astype(BF16)
                    xs_ref[slot, k, r0:r0 + 2 * SUBLANES, half + lo:half + hi] = (
                        jnp.concatenate(before_im, axis=0).astype(BF16))

    def project_out(ci):
        slot = ci % 2
        yk = [intra.pop((ci, k)) + jnp.dot(xs_ref[slot, k], wc_ref[0, k], preferred_element_type=F32)
              for k in range(nblk)]
        steps = [jnp.concatenate([yk[k][:, s * bw:(s + 1) * bw] for k in range(nblk)], axis=1)
                 .reshape(runs, 1, SUBLANES, nblk * bw) for s in range(q)]
        y = jnp.concatenate(steps, axis=1).reshape(q * rows_c, nblk * bw)
        u = u_ref[0, ci * q * rows_c:(ci + 1) * q * rows_c, :].astype(F32)
        y_ref[0, ci * q * rows_c:(ci + 1) * q * rows_c, :] = _gelu_tanh(y + d_ref[0] * u).astype(BF16)

    project_in(0)
    for ci in range(chunks):
        if ci + 1 < chunks:
            project_in(ci + 1)
        recur(ci)
        project_out(ci)

    for k in range(nblk):
        for j in range(nl):
            st_ref[k, 0, :, j * LANES:(j + 1) * LANES] = x_re[k][j]
            st_ref[k, 1, :, j * LANES:(j + 1) * LANES] = x_im[k][j]


def _s5_scan(u_tiles, w_b, w_i, w_c, aq, d_skip, runs, chunks):
    nt, nblk, kw, sw = w_b.shape
    q = S5_RUN
    cw = u_tiles.shape[2]
    t = u_tiles.shape[1]
    rows_c = runs * SUBLANES
    rows = q * rows_c * chunks
    return pl.pallas_call(
        functools.partial(_s5_scan_kernel, runs=runs, chunks=chunks, q=q),
        out_shape=jax.ShapeDtypeStruct((nt, t, cw), BF16),
        grid=(nt, t // rows),
        in_specs=[
            pl.BlockSpec((1, rows, cw), lambda k, c: (k, c, 0)),
            _resident((1, nblk, kw, sw), lambda k, c: (k, 0, 0, 0)),
            _resident((1, nblk, kw, kw), lambda k, c: (k, 0, 0, 0)),
            _resident((1, nblk, sw, kw), lambda k, c: (k, 0, 0, 0)),
            _resident((1, nblk, SUBLANES, sw), lambda k, c: (k, 0, 0, 0)),
            _resident((1, 1, cw), lambda k, c: (k, 0, 0)),
        ],
        out_specs=pl.BlockSpec((1, rows, cw), lambda k, c: (k, c, 0)),
        scratch_shapes=[
            pltpu.VMEM((2, nblk, rows_c, sw), F32),
            pltpu.VMEM((2, nblk, rows_c, sw), BF16),
            pltpu.VMEM((nblk, 2, SUBLANES, sw // 2), F32),
        ],
        compiler_params=_params("parallel", "arbitrary"),
        name="s5_scan",
    )(u_tiles, w_b, w_i, w_c, aq, d_skip.reshape(nt, 1, cw))


def _s5_out_kernel(y_ref, z_ref, x_ref, gate_ref, wg_ref, wo_ref, o_ref):
    y = jnp.concatenate([y_ref[k] for k in range(y_ref.shape[0])], axis=1)
    yf = y.astype(F32)
    y2 = yf * _sigmoid(jnp.dot(y, wg_ref[...], preferred_element_type=F32)) * _silu(z_ref[...].astype(F32))
    out = jnp.dot(y2.astype(BF16), wo_ref[...], preferred_element_type=F32)
    r, d = out.shape
    x = x_ref[...].reshape(r // SUBLANES, SUBLANES, d)
    o_ref[...] = (x + gate_ref[...][None] * out.reshape(r // SUBLANES, SUBLANES, d)).reshape(r, d)


def _s5_out(y, z, x_tm, mod, layer, w_glu, w_out):
    t, d = x_tm.shape
    e = w_glu.shape[0]
    nt, _, cw = y.shape
    rows = min(ROW_TILE, t)
    row_spec = lambda c: pl.BlockSpec((rows, c), lambda i: (i, 0))
    return pl.pallas_call(
        _s5_out_kernel,
        out_shape=jax.ShapeDtypeStruct((t, d), F32),
        grid=(t // rows,),
        in_specs=[
            pl.BlockSpec((nt, rows, cw), lambda i: (0, i, 0)), row_spec(e), row_spec(d),
            _resident((SUBLANES, d), lambda i: (layer, MOD_GATE)),
            _resident((e, e), lambda i: (0, 0)),
            _resident((e, d), lambda i: (0, 0)),
        ],
        out_specs=row_spec(d),
        compiler_params=_params("parallel"),
        name="s5_out",
    )(y, z, x_tm, mod, w_glu.astype(BF16), w_out.astype(BF16))


def _gdn_in_kernel(x_ref, nw_ref, sc_ref, sh_ref, w_ref, wg_ref, cw_ref, gp_ref,
                   q_ref, k_ref, v_ref, g_ref, ext_ref, st_ref, *, heads):
    halo = (GDN_CONV - 1) * SUBLANES
    r, d = x_ref.shape
    cc = ext_ref.shape[1]
    bsz = SUBLANES
    tmt = r // bsz
    qk = q_ref.shape[1] // bsz
    e = v_ref.shape[1] // bsz
    dk = qk // heads

    @pl.when(pl.program_id(0) == 0)
    def _():
        ext_ref[0:halo, :] = jnp.zeros((halo, cc), F32)

    h = _modulated_norm_rows(x_ref[...], nw_ref[...], sc_ref[...], sh_ref[...]).astype(BF16)

    ext_ref[halo:halo + r, :] = jnp.dot(h, w_ref[:, :cc], preferred_element_type=F32)
    conv = cw_ref[0:1, :] * ext_ref[0:r, :]
    for j in range(1, GDN_CONV):
        conv = conv + cw_ref[j:j + 1, :] * ext_ref[j * SUBLANES:j * SUBLANES + r, :]
    ext_ref[0:halo, :] = ext_ref[r:r + halo, :]
    act = _silu(conv)

    def stage(lo, val):
        for j in range(val.shape[1] // LANES):
            st_ref[lo // LANES + j] = val[:, j * LANES:(j + 1) * LANES]

    for hh in range(heads):
        qh = act[:, hh * dk:(hh + 1) * dk]
        kh = act[:, qk + hh * dk:qk + (hh + 1) * dk]
        stage(hh * dk, qh * lax.rsqrt(jnp.sum(qh * qh, axis=-1, keepdims=True) + NORM_EPS) * (dk ** -0.5))
        stage(qk + hh * dk, kh * lax.rsqrt(jnp.sum(kh * kh, axis=-1, keepdims=True) + NORM_EPS))
    stage(2 * qk, act[:, 2 * qk:])

    logit = jnp.dot(h, wg_ref[...], preferred_element_type=F32)
    beta = _sigmoid(logit)
    g = -jnp.exp(gp_ref[0:1, :]) * _softplus(logit + gp_ref[1:2, :])
    lane = lax.broadcasted_iota(jnp.int32, logit.shape, 1)
    stage(cc, jnp.where(lane < heads, beta, g))

    for b in range(bsz):
        seq = lambda lo, hi: _sequence_rows(st_ref, b, tmt, lo, hi)
        q_ref[:, b * qk:(b + 1) * qk] = seq(0, qk).astype(BF16)
        k_ref[:, b * qk:(b + 1) * qk] = seq(qk, 2 * qk).astype(BF16)
        v_ref[:, b * e:(b + 1) * e] = seq(2 * qk, cc).astype(BF16)
        g_ref[:, b * LANES:(b + 1) * LANES] = seq(cc, cc + LANES)


def _gdn_in(x_tm, norm_w, mod, layer, w_in, conv_w, a_log, dt_bias):
    t, d = x_tm.shape
    bsz = SUBLANES
    seqlen = t // bsz
    heads = a_log.shape[0]
    cc = conv_w.shape[1]
    e = (w_in.shape[1] - cc - 2 * heads)
    qk = (cc - e) // 2
    rows = min(GDN_IN_ROWS, t)
    tmt = rows // bsz
    w_main = w_in[:, :cc].astype(BF16)
    w_gate = jnp.pad(w_in[:, cc + e:], ((0, 0), (0, LANES - 2 * heads))).astype(BF16)
    gate_par = jnp.pad(jnp.stack([a_log, dt_bias]), ((0, 0), (heads, LANES - 2 * heads)))
    seq_spec = lambda c: pl.BlockSpec((tmt, bsz * c), lambda i: (i, 0))
    return pl.pallas_call(
        functools.partial(_gdn_in_kernel, heads=heads),
        out_shape=[
            jax.ShapeDtypeStruct((seqlen, bsz * qk), BF16),
            jax.ShapeDtypeStruct((seqlen, bsz * qk), BF16),
            jax.ShapeDtypeStruct((seqlen, bsz * e), BF16),
            jax.ShapeDtypeStruct((seqlen, bsz * LANES), F32),
        ],
        grid=(t // rows,),
        in_specs=[
            pl.BlockSpec((rows, d), lambda i: (i, 0)),
            _resident((1, d), lambda i: (0, 0)),
            _resident((SUBLANES, d), lambda i: (layer, MOD_SCALE)),
            _resident((SUBLANES, d), lambda i: (layer, MOD_SHIFT)),
            _resident((d, cc), lambda i: (0, 0)),
            _resident((d, LANES), lambda i: (0, 0)),
            _resident((GDN_CONV, cc), lambda i: (0, 0)),
            _resident((2, LANES), lambda i: (0, 0)),
        ],
        out_specs=[seq_spec(qk), seq_spec(qk), seq_spec(e), seq_spec(LANES)],
        scratch_shapes=[
            pltpu.VMEM((rows + (GDN_CONV - 1) * SUBLANES, cc), F32),
            pltpu.VMEM(((cc + LANES) // LANES, rows, LANES), F32),
        ],
        compiler_params=_params("arbitrary"),
        name="gdn_in",
    )(x_tm, norm_w.reshape(1, d), mod, mod, w_main, w_gate, conv_w, gate_par)


def _pair_block_diag(x):
    c = x.shape[0]
    lane = lax.broadcasted_iota(jnp.int32, x.shape, 1)
    zero = jnp.zeros_like(x)
    return jnp.concatenate([jnp.where(lane < c, x, zero), jnp.where(lane >= c, x, zero)], axis=0)


def _pair_mm(x, y):
    return jnp.dot(x.astype(BF16), _pair_block_diag(y.astype(BF16)), preferred_element_type=F32)


def _gdn_core_kernel(q_ref, k_ref, v_ref, g_ref, nw_ref, y_ref, s_ref, *, heads, nb):
    @pl.when(pl.program_id(1) == 0)
    def _():
        s_ref[...] = jnp.zeros_like(s_ref)

    c = q_ref.shape[0]
    qk_w = q_ref.shape[1] // nb
    e_w = v_ref.shape[1] // nb
    dk = qk_w // heads
    dv = e_w // heads
    units = [(b, p) for b in range(nb) for p in range(heads // 2)]

    row = lax.broadcasted_iota(jnp.int32, (c, 2 * c), 0)
    col = lax.broadcasted_iota(jnp.int32, (c, 2 * c), 1) & (c - 1)
    causal = row >= col
    strict = row > col
    eye = (row == col).astype(F32)
    row1 = lax.broadcasted_iota(jnp.int32, (c, c), 0)
    col1 = lax.broadcasted_iota(jnp.int32, (c, c), 1)
    tri = (row1 >= col1).astype(BF16)
    nh2 = 2 * heads
    eye_h = (lax.broadcasted_iota(jnp.int32, (nh2, nh2), 0)
             == lax.broadcasted_iota(jnp.int32, (nh2, nh2), 1)).astype(BF16)

    def same_block(shift):
        return (row >> shift) == (col >> shift)

    gb, gc, gc_t = [], [], []
    for b in range(nb):
        g = g_ref[:, b * LANES:b * LANES + nh2]
        cs = sum(jnp.dot(tri, part, preferred_element_type=F32) for part in _split3(g))
        gb.append(g)
        gc.append(cs)
        gc_t.append(sum(lax.dot_general(eye_h, part, (((1,), (1,)), ((), ())), preferred_element_type=F32)
                        for part in _split3(cs)))

    def lanes2(ref, b, p, width, per_batch):
        return ref[:, b * per_batch + 2 * p * width:b * per_batch + 2 * (p + 1) * width]

    def pair_cols(b, p, fn, width):
        return jnp.concatenate([jnp.broadcast_to(fn(b, 2 * p + i), (c, width)) for i in range(2)], axis=1)

    beta_c = lambda b, h: gb[b][:, h:h + 1]
    gcol = lambda b, h: gc[b][:, heads + h:heads + h + 1]
    grow = lambda b, h: gc_t[b][heads + h:heads + h + 1, :]

    a_mat, qk_mat = {}, {}
    for (b, p) in units:
        qp = lanes2(q_ref, b, p, dk, qk_w)
        kp = lanes2(k_ref, b, p, dk, qk_w)
        res = lax.dot_general(jnp.concatenate([qp, kp], axis=0), _pair_block_diag(kp),
                              (((1,), (1,)), ((), ())), preferred_element_type=F32)
        decay = jnp.where(causal, jnp.exp(jnp.concatenate(
            [gcol(b, 2 * p + i) - grow(b, 2 * p + i) for i in range(2)], axis=1)), 0.0)
        qk_mat[b, p] = res[:c] * decay
        a_mat[b, p] = jnp.where(strict, pair_cols(b, p, beta_c, c) * res[c:] * decay, 0.0)

    shift0 = GDN_BASE_BLOCK.bit_length() - 1
    dblk = {u: jnp.where(same_block(shift0), a_mat[u], 0.0) for u in units}
    d2 = {u: _pair_mm(dblk[u], dblk[u]) for u in units}
    d4 = {u: _pair_mm(d2[u], d2[u]) for u in units}
    t_inv = {u: eye - dblk[u] for u in units}
    for pw in (d2, d4):
        upd = {u: _pair_mm(t_inv[u], pw[u]) for u in units}
        t_inv = {u: t_inv[u] + upd[u] for u in units}
    shift = shift0
    while (1 << shift) < c:
        level = same_block(shift + 1) & jnp.logical_not(same_block(shift))
        xt = {u: _pair_mm(jnp.where(level, a_mat[u], 0.0), t_inv[u]) for u in units}
        upd = {u: _pair_mm(t_inv[u], xt[u]) for u in units}
        t_inv = {u: t_inv[u] - upd[u] for u in units}
        shift += 1

    w_pk, u_pk, gam_pk = {}, {}, {}
    for (b, p) in units:
        gam_pk[b, p] = jnp.exp(pair_cols(b, p, gcol, dk))
        kp = lanes2(k_ref, b, p, dk, qk_w).astype(F32)
        vp = lanes2(v_ref, b, p, dv, e_w).astype(F32)
        rw = (pair_cols(b, p, beta_c, dk) * gam_pk[b, p] * kp).astype(BF16)
        ru = (pair_cols(b, p, beta_c, dv) * vp).astype(BF16)
        zk = jnp.zeros((c, dk), BF16)
        zv = jnp.zeros((c, dv), BF16)
        rhs = jnp.concatenate([
            jnp.concatenate([rw[:, :dk], zk, ru[:, :dv], zv], axis=1),
            jnp.concatenate([zk, rw[:, dk:], zv, ru[:, dv:]], axis=1)], axis=0)
        wu = jnp.dot(t_inv[b, p].astype(BF16), rhs, preferred_element_type=F32)
        w_pk[b, p] = wu[:, :2 * dk]
        u_pk[b, p] = wu[:, 2 * dk:]

    hunits = [(b, h) for b in range(nb) for h in range(heads)]
    ws, states = {}, {}
    for (b, h) in hunits:
        p, i = divmod(h, 2)
        qd = lanes2(q_ref, b, p, dk, qk_w).astype(F32) * gam_pk[b, p]
        lhs = jnp.concatenate([w_pk[b, p][:, i * dk:(i + 1) * dk], qd[:, i * dk:(i + 1) * dk]], axis=0)
        states[b, h] = s_ref[b * heads + h]
        ws[b, h] = _bdot(lhs, states[b, h])
    v_new = {(b, h): u_pk[b, h // 2][:, (h % 2) * dv:(h % 2 + 1) * dv] - ws[b, h][:c] for (b, h) in hunits}
    o_intra = {(b, h): _bdot(qk_mat[b, h // 2][:, (h % 2) * c:(h % 2 + 1) * c], v_new[b, h]) for (b, h) in hunits}
    for (b, h) in hunits:
        g_last = gcol(b, h)[c - 1:c, :]
        kh = k_ref[:, b * qk_w + h * dk:b * qk_w + (h + 1) * dk].astype(F32)
        k_dec = kh * jnp.exp(g_last - gcol(b, h))
        s_ref[b * heads + h] = jnp.exp(g_last) * states[b, h] + _bdot_tn(k_dec, v_new[b, h])
    for (b, h) in hunits:
        o = ws[b, h][c:] + o_intra[b, h]
        y_ref[:, b * e_w + h * dv:b * e_w + (h + 1) * dv] = (_rms(o) * nw_ref[...]).astype(BF16)


def _gdn_core(q, k, v, gates, norm_w, heads):
    seqlen = q.shape[0]
    nb = GDN_BATCH_PER_STEP
    bsz = gates.shape[1] // LANES
    qk = q.shape[1] // bsz
    e = v.shape[1] // bsz
    c = min(GDN_CHUNK, seqlen)
    col_spec = lambda w: pl.BlockSpec((c, nb * w), lambda b, n: (n, b))
    return pl.pallas_call(
        functools.partial(_gdn_core_kernel, heads=heads, nb=nb),
        out_shape=jax.ShapeDtypeStruct((seqlen, bsz * e), BF16),
        grid=(bsz // nb, seqlen // c),
        in_specs=[col_spec(qk), col_spec(qk), col_spec(e), col_spec(LANES),
                  _resident((1, e // heads), lambda b, n: (0, 0))],
        out_specs=col_spec(e),
        scratch_shapes=[pltpu.VMEM((nb * heads, qk // heads, e // heads), F32)],
        compiler_params=_params("parallel", "arbitrary"),
        name="gdn_core",
    )(q, k, v, gates, norm_w.reshape(1, e // heads))


def _gdn_out_kernel(y_ref, x_ref, nw_ref, sc_ref, sh_ref, gate_ref, wz_ref, w_ref, fw_ref, o_ref, slab_ref):
    bsz, tmt, d = o_ref.shape
    e = y_ref.shape[1] // bsz
    for j in range(d // LANES):
        slab_ref[j] = x_ref[:, j * LANES:(j + 1) * LANES]
    xs = [_sequence_rows(slab_ref, b, tmt, 0, d) for b in range(bsz)]
    h = jnp.concatenate([((_rms(xs[b]) * nw_ref[...]) * (1.0 + sc_ref[b:b + 1, :]) + sh_ref[b:b + 1, :]).astype(BF16)
                         for b in range(bsz)], axis=0)
    z = jnp.dot(h, wz_ref[...], preferred_element_type=F32)
    o = jnp.concatenate([y_ref[:, b * e:(b + 1) * e] for b in range(bsz)], axis=0).astype(F32)
    out = jnp.dot((o * _silu(z)).astype(BF16), w_ref[...], preferred_element_type=F32).reshape(bsz, tmt, d)
    for b in range(bsz):
        xn = xs[b] + gate_ref[b:b + 1, :] * out[b]
        o_ref[b] = _rms(xn) * fw_ref[...]


def _gdn_out(y, x_tm, norm_w, mod, layer, w_z, w_out, final_w):
    e, d = w_out.shape
    seqlen = y.shape[0]
    bsz = y.shape[1] // e
    tmt = min(ROW_TILE // bsz, seqlen)
    return pl.pallas_call(
        _gdn_out_kernel,
        out_shape=jax.ShapeDtypeStruct((bsz, seqlen, d), F32),
        grid=(seqlen // tmt,),
        in_specs=[
            pl.BlockSpec((tmt, bsz * e), lambda i: (i, 0)),
            pl.BlockSpec((tmt * bsz, d), lambda i: (i, 0)),
            _resident((1, d), lambda i: (0, 0)),
            _resident((bsz, d), lambda i: (layer, MOD_SCALE)),
            _resident((bsz, d), lambda i: (layer, MOD_SHIFT)),
            _resident((bsz, d), lambda i: (layer, MOD_GATE)),
            _resident((d, e), lambda i: (0, 0)),
            _resident((e, d), lambda i: (0, 0)),
            _resident((1, d), lambda i: (0, 0)),
        ],
        out_specs=pl.BlockSpec((bsz, tmt, d), lambda i: (0, i, 0)),
        scratch_shapes=[pltpu.VMEM((d // LANES, tmt * bsz, LANES), F32)],
        compiler_params=_params("parallel"),
        name="gdn_out",
    )(y, x_tm, norm_w.reshape(1, d), mod, mod, mod, w_z.astype(BF16), w_out.astype(BF16), final_w.reshape(1, d))


def kernel(x, c, ada_w, ada_b, norm_w, s5_w_in, s5_lambda_re, s5_lambda_im, s5_log_dt, s5_b_re, s5_b_im,
           s5_c_re, s5_c_im, s5_d, s5_w_glu, s5_w_out, gdn_w_in, gdn_conv_w, gdn_a_log, gdn_dt_bias,
           gdn_norm_w, gdn_w_out, final_norm_w):
    bsz, seqlen, d = x.shape
    assert bsz == SUBLANES, "the time-major layout keeps one batch row per vector sublane"
    assert ada_w.shape[0] == 2 and s5_w_in.shape[0] == 1 and gdn_w_in.shape[0] == 1

    mod = _modulation(c, ada_w, ada_b)

    w_b, w_i, w_c, aq = _s5_weights(s5_lambda_re[0], s5_lambda_im[0], s5_log_dt[0], s5_b_re[0], s5_b_im[0],
                                    s5_c_re[0], s5_c_im[0])
    x_tm, u, z = _s5_in(x, norm_w[0], mod, 0, s5_w_in[0])
    runs = min(S5_RUNS, seqlen // S5_RUN)
    chunks = min(S5_CHUNKS, seqlen // (S5_RUN * runs))
    y = _s5_scan(u, w_b, w_i, w_c, aq, s5_d[0], runs, chunks)
    x1_tm = _s5_out(y, z, x_tm, mod, 0, s5_w_glu[0], s5_w_out[0])

    heads = gdn_a_log.shape[1]
    q, k, v, gates = _gdn_in(x1_tm, norm_w[1], mod, 1, gdn_w_in[0], gdn_conv_w[0],
                                 gdn_a_log[0], gdn_dt_bias[0])
    yg = _gdn_core(q, k, v, gates, gdn_norm_w[0], heads)
    cc = gdn_conv_w.shape[2]
    w_z = gdn_w_in[0][:, cc:cc + yg.shape[1] // bsz]
    return _gdn_out(yg, x1_tm, norm_w[1], mod, 1, w_z, gdn_w_out[0], final_norm_w)
```
